```python
import jax, jax.numpy as jnp
from jax import lax
import numpy as np

D_MODEL = 1024
BATCH = 2
SEQ = 8192
DEPTH = 1

HEAD_DIM = 64
N_HEADS_A = 8
N_KV_HEADS_A = 2
N_HEADS_B = 8
MIX_WIDTH = (N_HEADS_A + N_HEADS_B) * HEAD_DIM
DILATED_BRANCHES = ((128, 1), (512, 4), (2048, 16))
Q_BLOCK = 128
GRID_W = 64
ROPE_THETA = 10000.0
N_EXPERTS = 32
TOP_K = 4
D_FF = D_MODEL
SWIGLU_LIMIT = 7.0
SWIGLU_ALPHA = 1.702
EXPERT_BLOCK = 128
EPS = 1e-6

SPLIT_SIZES = (N_HEADS_A * HEAD_DIM, N_KV_HEADS_A * HEAD_DIM, N_KV_HEADS_A * HEAD_DIM,
               N_HEADS_B * HEAD_DIM, N_HEADS_B * HEAD_DIM, N_HEADS_B * HEAD_DIM)
IN_WIDTH = sum(SPLIT_SIZES)
SPLIT_POINTS = tuple(int(v) for v in np.cumsum(SPLIT_SIZES)[:-1])

kernel_name = "hymba_grid_dilated_moe_encoder"


def rms_norm(x, g):
    x32 = x.astype(jnp.float32)
    y = x32 * lax.rsqrt(jnp.mean(x32 * x32, axis=-1, keepdims=True) + EPS)
    return (y * g.astype(jnp.float32)).astype(x.dtype)


def rope_angles(pos, dim):
    inv = ROPE_THETA ** (-jnp.arange(0, dim, 2, dtype=jnp.float32) / dim)
    return pos.astype(jnp.float32)[:, None] * inv[None, :]


def apply_rope(x, ang):
    c = jnp.cos(ang)[None, :, None, :].astype(x.dtype)
    s = jnp.sin(ang)[None, :, None, :].astype(x.dtype)
    x1, x2 = jnp.split(x, 2, axis=-1)
    return jnp.concatenate([x1 * c - x2 * s, x2 * c + x1 * s], axis=-1)


def axial_rope(x, ang_row, ang_col):
    half = HEAD_DIM // 2
    return jnp.concatenate([apply_rope(x[..., :half], ang_row),
                            apply_rope(x[..., half:], ang_col)], axis=-1)


def grid_attention(q, k, v):
    B, S = q.shape[:2]
    G = N_HEADS_A // N_KV_HEADS_A
    nq = S // Q_BLOCK
    scale = HEAD_DIM ** -0.5
    qb = q.reshape(B, nq, Q_BLOCK, N_KV_HEADS_A, G, HEAD_DIM).transpose(1, 0, 2, 3, 4, 5)

    def block(qi):
        s = jnp.einsum('bqhgd,bkhd->bhgqk', qi, k).astype(jnp.float32) * scale
        p = jax.nn.softmax(s, axis=-1).astype(v.dtype)
        return jnp.einsum('bhgqk,bkhd->bqhgd', p, v)

    o = lax.map(block, qb)
    return o.transpose(1, 0, 2, 3, 4, 5).reshape(B, S, N_HEADS_A * HEAD_DIM)


def dilated_attention(q, k, v):
    B, S, H, _ = q.shape
    nq = S // Q_BLOCK
    scale = HEAD_DIM ** -0.5

    def block(i):
        start = i * Q_BLOCK
        qi = lax.dynamic_slice_in_dim(q, start, Q_BLOCK, axis=1)
        t = start + jnp.arange(Q_BLOCK)
        outs, lses = [], []
        for window, dil in DILATED_BRANCHES:
            r = window // (2 * dil)
            idx = t[:, None] + dil * jnp.arange(-r, r + 1)[None, :]
            valid = (idx >= 0) & (idx < S)
            idxc = jnp.clip(idx, 0, S - 1)
            kg = jnp.take(k, idxc, axis=1)
            vg = jnp.take(v, idxc, axis=1)
            s = jnp.einsum('bqhd,bqjhd->bhqj', qi, kg).astype(jnp.float32) * scale
            s = jnp.where(valid[None, None], s, -1e30)
            lse = jax.nn.logsumexp(s, axis=-1, keepdims=True)
            p = jnp.exp(s - lse).astype(v.dtype)
            outs.append(jnp.einsum('bhqj,bqjhd->bqhd', p, vg))
            lses.append(lse[..., 0])
        w = jax.nn.softmax(jnp.stack(lses, axis=0), axis=0)
        w = w.transpose(0, 1, 3, 2)[..., None].astype(v.dtype)
        return jnp.sum(jnp.stack(outs, axis=0) * w, axis=0)

    o = lax.map(block, jnp.arange(nq))
    return o.transpose(1, 0, 2, 3, 4).reshape(B, S, H * HEAD_DIM)


def moe_ffn(h, w_router, b_router, w_gate, b_gate, w_up, b_up, w_down, b_down):
    N, D = h.shape
    logits = (h @ w_router + b_router).astype(jnp.float32)
    top_vals, top_idx = lax.top_k(logits, TOP_K)
    gates = jax.nn.softmax(top_vals, axis=-1)

    NK = N * TOP_K
    e_flat = top_idx.reshape(NK)
    tok_flat = jnp.arange(NK) // TOP_K
    g_flat = gates.reshape(NK)
    order = jnp.argsort(e_flat)
    e_sorted, tok_sorted, g_sorted = e_flat[order], tok_flat[order], g_flat[order]

    counts = jax.ops.segment_sum(jnp.ones((NK,), jnp.int32), e_flat, num_segments=N_EXPERTS)
    starts = jnp.cumsum(counts) - counts
    pad_counts = (counts + EXPERT_BLOCK - 1) // EXPERT_BLOCK * EXPERT_BLOCK
    pad_ends = jnp.cumsum(pad_counts)
    pad_starts = pad_ends - pad_counts
    dest = pad_starts[e_sorted] + (jnp.arange(NK) - starts[e_sorted])

    NB = (NK + EXPERT_BLOCK - 1) // EXPERT_BLOCK + N_EXPERTS
    rows = NB * EXPERT_BLOCK
    row_tok = jnp.zeros((rows,), jnp.int32).at[dest].set(tok_sorted)
    row_w = jnp.zeros((rows,), jnp.float32).at[dest].set(g_sorted)
    block_e = jnp.clip(jnp.searchsorted(pad_ends, jnp.arange(NB) * EXPERT_BLOCK, side='right'),
                       0, N_EXPERTS - 1)

    def expert_block(args):
        e, toks, wts = args
        xb = h[toks]
        g = xb @ w_gate[e] + b_gate[e]
        u = xb @ w_up[e] + b_up[e]
        g = jnp.minimum(g, SWIGLU_LIMIT)
        u = jnp.clip(u, -SWIGLU_LIMIT, SWIGLU_LIMIT)
        act = (u + 1.0) * (g * jax.nn.sigmoid(SWIGLU_ALPHA * g))
        return (act @ w_down[e] + b_down[e]) * wts[:, None].astype(h.dtype)

    ys = lax.map(expert_block, (block_e, row_tok.reshape(NB, EXPERT_BLOCK),
                                row_w.reshape(NB, EXPERT_BLOCK)))
    return jax.ops.segment_sum(ys.reshape(rows, D), row_tok, num_segments=N)


def setup_inputs(seed: int = 0) -> dict:
    key = jax.random.key(seed)
    ks = jax.random.split(key, 20)
    f32 = jnp.float32
    nrm = lambda k, shape, s: jax.random.normal(k, shape, f32) * s
    gain = lambda k, shape: 1.0 + 0.1 * jax.random.normal(k, shape, f32)
    L, E = DEPTH, N_EXPERTS
    return {
        "x": jax.random.normal(ks[0], (BATCH, SEQ, D_MODEL), f32),
        "attn_norm": gain(ks[1], (L, D_MODEL)),
        "w_in": nrm(ks[2], (L, D_MODEL, IN_WIDTH), D_MODEL ** -0.5),
        "q_norm": gain(ks[3], (L, HEAD_DIM)),
        "k_norm": gain(ks[4], (L, HEAD_DIM)),
        "out_norm_a": gain(ks[5], (L, N_HEADS_A * HEAD_DIM)),
        "out_norm_b": gain(ks[6], (L, N_HEADS_B * HEAD_DIM)),
        "w_out": nrm(ks[7], (L, MIX_WIDTH, D_MODEL), MIX_WIDTH ** -0.5),
        "ffn_norm": gain(ks[8], (L, D_MODEL)),
        "w_router": nrm(ks[9], (L, D_MODEL, E), D_MODEL ** -0.5),
        "b_router": nrm(ks[10], (L, E), 0.01),
        "w_gate": nrm(ks[11], (L, E, D_MODEL, D_FF), D_MODEL ** -0.5),
        "b_gate": nrm(ks[12], (L, E, D_FF), 0.01),
        "w_up": nrm(ks[13], (L, E, D_MODEL, D_FF), D_MODEL ** -0.5),
        "b_up": nrm(ks[14], (L, E, D_FF), 0.01),
        "w_down": nrm(ks[15], (L, E, D_FF, D_MODEL), D_FF ** -0.5),
        "b_down": nrm(ks[16], (L, E, D_MODEL), 0.01),
        "final_norm": gain(ks[17], (D_MODEL,)),
    }


def reference(x, attn_norm, w_in, q_norm, k_norm, out_norm_a, out_norm_b, w_out,
              ffn_norm, w_router, b_router, w_gate, b_gate, w_up, b_up, w_down, b_down,
              final_norm):
    B, S, D = x.shape
    ROWS = S // GRID_W
    pos = jnp.arange(S)
    row = jnp.repeat(jnp.arange(ROWS), GRID_W)
    col = jnp.tile(jnp.arange(GRID_W), ROWS)
    ang_row = rope_angles(row, HEAD_DIM // 2)
    ang_col = rope_angles(col, HEAD_DIM // 2)
    ang_1d = rope_angles(pos, HEAD_DIM)

    for l in range(DEPTH):
        hn = rms_norm(x, attn_norm[l])
        proj = hn @ w_in[l]
        qa, ka, va, qb, kb, vb = jnp.split(proj, SPLIT_POINTS, axis=-1)
        qa = qa.reshape(B, S, N_HEADS_A, HEAD_DIM)
        ka = ka.reshape(B, S, N_KV_HEADS_A, HEAD_DIM)
        va = va.reshape(B, S, N_KV_HEADS_A, HEAD_DIM)
        qa = axial_rope(rms_norm(qa, q_norm[l]), ang_row, ang_col)
        ka = axial_rope(rms_norm(ka, k_norm[l]), ang_row, ang_col)
        oa = grid_attention(qa, ka, va)

        qb = apply_rope(qb.reshape(B, S, N_HEADS_B, HEAD_DIM), ang_1d)
        kb = apply_rope(kb.reshape(B, S, N_HEADS_B, HEAD_DIM), ang_1d)
        vb = vb.reshape(B, S, N_HEADS_B, HEAD_DIM)
        ob = dilated_attention(qb, kb, vb)

        mix = jnp.concatenate([rms_norm(oa, out_norm_a[l]), rms_norm(ob, out_norm_b[l])], axis=-1)
        x = x + mix @ w_out[l]

        hf = rms_norm(x, ffn_norm[l]).reshape(B * S, D)
        y = moe_ffn(hf, w_router[l], b_router[l], w_gate[l], b_gate[l],
                    w_up[l], b_up[l], w_down[l], b_down[l])
        x = x + y.reshape(B, S, D).astype(x.dtype)

    return rms_norm(x, final_norm)
```

```python
import functools

import jax
import jax.numpy as jnp
import numpy as np
from jax import lax
from jax.experimental import pallas as pl
from jax.experimental.pallas import tpu as pltpu

F32 = jnp.float32
BF16 = jnp.bfloat16

HEAD_DIM = 64
N_HEADS_A = 8
N_KV_HEADS_A = 2
N_HEADS_B = 8
DILATED_BRANCHES = ((128, 1), (512, 4), (2048, 16))
GRID_W = 64
ROPE_THETA = 10000.0
N_EXPERTS = 32
TOP_K = 4
SWIGLU_LIMIT = 7.0
SWIGLU_ALPHA = 1.702
EPS = 1e-6

LANES = 128
QA_W = N_HEADS_A * HEAD_DIM
KA_W = N_KV_HEADS_A * HEAD_DIM
QB_W = N_HEADS_B * HEAD_DIM
VMEM_LIMIT = 56 * 1024 * 1024


def _cparams(*sem):
    return pltpu.CompilerParams(dimension_semantics=sem, vmem_limit_bytes=VMEM_LIMIT)


def _rope_tables(S):
    pos = jnp.arange(S, dtype=F32)
    row = jnp.floor(pos / GRID_W)
    col = pos - row * GRID_W
    half = HEAD_DIM // 2
    inv_a = ROPE_THETA ** (-jnp.arange(0, half, 2, dtype=F32) / half)
    inv_b = ROPE_THETA ** (-jnp.arange(0, HEAD_DIM, 2, dtype=F32) / HEAD_DIM)
    ang_row = row[:, None] * inv_a[None, :]
    ang_col = col[:, None] * inv_a[None, :]
    ang_1d = pos[:, None] * inv_b[None, :]
    cos_a = jnp.concatenate([jnp.cos(ang_row)] * 2 + [jnp.cos(ang_col)] * 2, axis=-1)
    sin_a = jnp.concatenate([-jnp.sin(ang_row), jnp.sin(ang_row),
                             -jnp.sin(ang_col), jnp.sin(ang_col)], axis=-1)
    cos_b = jnp.concatenate([jnp.cos(ang_1d)] * 2, axis=-1)
    sin_b = jnp.concatenate([-jnp.sin(ang_1d), jnp.sin(ang_1d)], axis=-1)
    two = lambda t: jnp.concatenate([t, t], axis=-1)
    return two(cos_a), two(sin_a), two(cos_b), two(sin_b)


def _rope_tile(x, cos, sin_signed, half):
    lane = lax.broadcasted_iota(jnp.int32, x.shape, 1)
    first = (lane % (2 * half)) < half
    partner = jnp.where(first, pltpu.roll(x, LANES - half, 1), pltpu.roll(x, half, 1))
    return x * cos + partner * sin_signed


def _head_rms(x, seg_mean, gain):
    ms = jnp.dot((x * x).astype(BF16), seg_mean, preferred_element_type=F32)
    return x * lax.rsqrt(ms + EPS) * gain


def _inproj_kernel(x_ref, g_ref, w_ref, qn_ref, kn_ref, seg_ref, cosa_ref, sina_ref, cosb_ref,
                   sinb_ref, qa_ref, ka_ref, va_ref, qb_ref, kb_ref, vb_ref):
    x = x_ref[...]
    ms = jnp.mean(x * x, axis=-1, keepdims=True)
    hn = (x * lax.rsqrt(ms + EPS) * g_ref[...]).astype(BF16)
    proj = jnp.dot(hn, w_ref[...], preferred_element_type=F32)
    seg = seg_ref[...]
    cosa, sina, cosb, sinb = cosa_ref[...], sina_ref[...], cosb_ref[...], sinb_ref[...]
    scale = HEAD_DIM ** -0.5
    lane = lax.broadcasted_iota(jnp.int32, (x.shape[0], LANES), 1)
    lo = lane < HEAD_DIM

    off = 0
    for j in range(QA_W // LANES):
        t = proj[:, off + j * LANES: off + (j + 1) * LANES]
        t = _rope_tile(_head_rms(t, seg, qn_ref[...]), cosa, sina, HEAD_DIM // 4)
        qa_ref[:, j * LANES:(j + 1) * LANES] = (t * scale).astype(qa_ref.dtype)
    off += QA_W
    k = _rope_tile(_head_rms(proj[:, off:off + LANES], seg, kn_ref[...]), cosa, sina, HEAD_DIM // 4)
    off += KA_W
    v = proj[:, off:off + LANES]
    off += KA_W
    for name_ref, t in ((ka_ref, k), (va_ref, v)):
        sw = pltpu.roll(t, HEAD_DIM, 1)
        name_ref[:, 0:LANES] = jnp.where(lo, t, sw).astype(name_ref.dtype)
        name_ref[:, LANES:2 * LANES] = jnp.where(lo, sw, t).astype(name_ref.dtype)
    for j in range(QB_W // LANES):
        t = proj[:, off + j * LANES: off + (j + 1) * LANES]
        qb_ref[:, j * LANES:(j + 1) * LANES] = (
            _rope_tile(t, cosb, sinb, HEAD_DIM // 2) * scale).astype(qb_ref.dtype)
    off += QB_W
    for j in range(QB_W // LANES):
        t = proj[:, off + j * LANES: off + (j + 1) * LANES]
        kb_ref[:, j * LANES:(j + 1) * LANES] = _rope_tile(t, cosb, sinb, HEAD_DIM // 2).astype(kb_ref.dtype)
    off += QB_W
    vb_ref[...] = proj[:, off:off + QB_W].astype(vb_ref.dtype)


def _input_projection(x2, attn_norm, w_in, q_norm, k_norm, S, tm=512):
    N, D = x2.shape
    tm = min(tm, S)
    assert S % tm == 0 and N % S == 0
    n_s = S // tm
    cos_a, sin_a, cos_b, sin_b = _rope_tables(S)
    seg = jnp.kron(jnp.eye(2, dtype=F32), jnp.full((HEAD_DIM, HEAD_DIM), 1.0 / HEAD_DIM, F32)).astype(BF16)
    two = lambda g: jnp.tile(g.reshape(1, HEAD_DIM), (1, 2))
    row = lambda i: (i, 0)
    const = lambda i: (0, 0)
    tab = lambda i: (i % n_s, 0)
    w = w_in.astype(BF16)
    out_w = (QA_W, 2 * LANES, 2 * LANES, QB_W, QB_W, QB_W)
    out_dt = (BF16, BF16, BF16, F32, F32, F32)
    return pl.pallas_call(
        _inproj_kernel,
        grid=(N // tm,),
        in_specs=[pl.BlockSpec((tm, D), row), pl.BlockSpec((1, D), const),
                  pl.BlockSpec(w.shape, const), pl.BlockSpec((1, LANES), const),
                  pl.BlockSpec((1, LANES), const), pl.BlockSpec((LANES, LANES), const)]
                 + [pl.BlockSpec((tm, LANES), tab)] * 4,
        out_specs=[pl.BlockSpec((tm, wd), row) for wd in out_w],
        out_shape=[jax.ShapeDtypeStruct((N, wd), dt) for wd, dt in zip(out_w, out_dt)],
        compiler_params=_cparams("parallel"),
        name="input_projection",
    )(x2, attn_norm.reshape(1, D), w, two(q_norm), two(k_norm), seg, cos_a, sin_a, cos_b, sin_b)


def _attn_a_kernel(q_ref, k_ref, v_ref, o_ref, *, tk):
    tq = q_ref.shape[0]
    S = k_ref.shape[0]
    lane = lax.broadcasted_iota(jnp.int32, (tq, LANES), 1)
    lo = lane < HEAD_DIM
    q0 = q_ref[:, 0:LANES]
    q1 = q_ref[:, LANES:2 * LANES]
    zero = jnp.zeros_like(q0)
    qs = jnp.concatenate([jnp.where(lo, q0, zero), jnp.where(lo, zero, q0),
                          jnp.where(lo, q1, zero), jnp.where(lo, zero, q1)], axis=0)
    rows = 4 * tq

    def body(j, carry):
        m, l, acc = carry
        start = pl.multiple_of(j * tk, tk)
        ks = k_ref[pl.ds(start, tk), :]
        vs = v_ref[pl.ds(start, tk), :]
        s = lax.dot_general(qs, ks, (((1,), (1,)), ((), ())), preferred_element_type=F32)
        m_new = jnp.maximum(m, jnp.max(s, axis=-1, keepdims=True))
        alpha = jnp.exp(m - m_new)
        p = jnp.exp(s - m_new)
        l = alpha * l + jnp.sum(p, axis=-1, keepdims=True)
        acc = alpha * acc + jnp.dot(p.astype(BF16), vs, preferred_element_type=F32)
        return m_new, l, acc

    init = (jnp.full((rows, 1), -jnp.inf, F32), jnp.zeros((rows, 1), F32),
            jnp.zeros((rows, LANES), F32))
    _, l, acc = lax.fori_loop(0, S // tk, body, init)
    o = acc / l
    o_ref[:, 0:LANES] = jnp.where(lo, o[0:tq], o[tq:2 * tq]).astype(o_ref.dtype)
    o_ref[:, LANES:2 * LANES] = jnp.where(lo, o[2 * tq:3 * tq], o[3 * tq:4 * tq]).astype(o_ref.dtype)


def _grid_attention(qa, ka2, va2, B, S, tq=256, tk=512):
    N = qa.shape[0]
    tq, tk = min(tq, S), min(tk, S)
    assert S % tq == 0 and S % tk == 0
    nq = S // tq
    return pl.pallas_call(
        functools.partial(_attn_a_kernel, tk=tk),
        grid=(B, N_KV_HEADS_A, nq),
        in_specs=[pl.BlockSpec((tq, 2 * LANES), lambda b, h, i: (b * nq + i, h)),
                  pl.BlockSpec((S, LANES), lambda b, h, i: (b, h)),
                  pl.BlockSpec((S, LANES), lambda b, h, i: (b, h))],
        out_specs=pl.BlockSpec((tq, 2 * LANES), lambda b, h, i: (b * nq + i, h)),
        out_shape=jax.ShapeDtypeStruct((N, QA_W), F32),
        compiler_params=_cparams("parallel", "parallel", "parallel"),
        name="grid_attention",
    )(qa, ka2, va2)


DIL_QB = 128
DIL_R = 64
DIL_KW = DIL_QB + 2 * DIL_R
DIL_SB = 2048


def _dilated_kernel(q_ref, k_ref, v_ref, o_ref, acc_scr, m_scr, l_scr, *, seq):
    c = pl.program_id(2)
    lane = lax.broadcasted_iota(jnp.int32, (DIL_QB, LANES), 1)
    lo = lane < HEAD_DIM
    rel = (lax.broadcasted_iota(jnp.int32, (2 * DIL_QB, DIL_KW), 1)
           - lax.broadcasted_iota(jnp.int32, (2 * DIL_QB, DIL_KW), 0) % DIL_QB)

    for bi, (window, d) in enumerate(DILATED_BRANCHES):
        assert window // (2 * d) == DIL_R
        nb = DIL_SB // (DIL_QB * d)
        n_m = seq // d

        def block(it, _, d=d, nb=nb, n_m=n_m, first=(bi == 0)):
            r = it // nb
            i = it % nb
            row0 = r + d * DIL_QB * i
            m0 = (c * DIL_SB) // d + DIL_QB * i
            ks = jnp.clip(m0 - DIL_R, 0, n_m - DIL_KW)
            q = q_ref[pl.ds(row0, DIL_QB, stride=d), :]
            k = k_ref[pl.ds(r + d * ks, DIL_KW, stride=d), :].astype(BF16)
            v = v_ref[pl.ds(r + d * ks, DIL_KW, stride=d), :].astype(BF16)
            zero = jnp.zeros_like(q)
            qs = jnp.concatenate([jnp.where(lo, q, zero), jnp.where(lo, zero, q)], axis=0).astype(BF16)
            s = lax.dot_general(qs, k, (((1,), (1,)), ((), ())), preferred_element_type=F32)
            s = jnp.where(jnp.abs(rel + (ks - m0)) <= DIL_R, s, -1e30)
            mb = jnp.max(s, axis=-1, keepdims=True)
            p = jnp.exp(s - mb)
            lb = jnp.sum(p, axis=-1, keepdims=True)
            pv = jnp.dot(p.astype(BF16), v, preferred_element_type=F32)
            acc_b = jnp.where(lo, pv[0:DIL_QB], pv[DIL_QB:])
            m_b = jnp.where(lo, mb[0:DIL_QB], mb[DIL_QB:])
            l_b = jnp.where(lo, lb[0:DIL_QB], lb[DIL_QB:])
            rows = pl.ds(row0, DIL_QB, stride=d)
            if first:
                acc_scr[rows, :] = acc_b
                m_scr[rows, :] = m_b
                l_scr[rows, :] = l_b
            else:
                m_old = m_scr[rows, :]
                m_new = jnp.maximum(m_old, m_b)
                a_old = jnp.exp(m_old - m_new)
                a_new = jnp.exp(m_b - m_new)
                acc_scr[rows, :] = acc_scr[rows, :] * a_old + acc_b * a_new
                l_scr[rows, :] = l_scr[rows, :] * a_old + l_b * a_new
                m_scr[rows, :] = m_new
            return 0

        lax.fori_loop(0, d * nb, block, 0)

    o_ref[...] = (acc_scr[...] / l_scr[...]).astype(o_ref.dtype)


def _dilated_attention(qb, kb, vb, B, S):
    N = qb.shape[0]
    assert S % DIL_SB == 0 and S // DILATED_BRANCHES[-1][1] >= DIL_KW
    nsb = S // DIL_SB
    return pl.pallas_call(
        functools.partial(_dilated_kernel, seq=S),
        grid=(B, QB_W // LANES, nsb),
        in_specs=[pl.BlockSpec((DIL_SB, LANES), lambda b, h, i: (b * nsb + i, h)),
                  pl.BlockSpec((S, LANES), lambda b, h, i: (b, h)),
                  pl.BlockSpec((S, LANES), lambda b, h, i: (b, h))],
        out_specs=pl.BlockSpec((DIL_SB, LANES), lambda b, h, i: (b * nsb + i, h)),
        out_shape=jax.ShapeDtypeStruct((N, QB_W), F32),
        scratch_shapes=[pltpu.VMEM((DIL_SB, LANES), F32)] * 3,
        compiler_params=_cparams("parallel", "parallel", "parallel"),
        name="dilated_attention",
    )(qb, kb, vb)


def _split_bf16(a):
    hi = a.astype(BF16)
    return hi, (a - hi.astype(F32)).astype(BF16)


def _outproj_router_kernel(x_ref, oa_ref, ob_ref, ga_ref, gb_ref, wo_ref, gf_ref, wr_hi_ref, wr_lo_ref,
                           br_ref, tri_ref, x1_ref, hf_ref, idx_ref, gate_ref, rank_ref, cnt_ref,
                           carry_scr):
    @pl.when(pl.program_id(0) == 0)
    def _():
        carry_scr[...] = jnp.zeros_like(carry_scr)

    def rms(t, g):
        return t * lax.rsqrt(jnp.mean(t * t, axis=-1, keepdims=True) + EPS) * g

    mix = jnp.concatenate([rms(oa_ref[...], ga_ref[...]), rms(ob_ref[...], gb_ref[...])], axis=-1)
    x1 = x_ref[...] + jnp.dot(mix.astype(BF16), wo_ref[...], preferred_element_type=F32)
    x1_ref[...] = x1
    hf = rms(x1, gf_ref[...])
    hf_ref[...] = hf

    h_hi, h_lo = _split_bf16(hf)
    nt = (((1,), (1,)), ((), ()))
    logits = (lax.dot_general(wr_hi_ref[...], h_hi, nt, preferred_element_type=F32)
              + lax.dot_general(wr_hi_ref[...], h_lo, nt, preferred_element_type=F32)
              + lax.dot_general(wr_lo_ref[...], h_hi, nt, preferred_element_type=F32)) + br_ref[...]
    E, tm = logits.shape
    eidx = lax.broadcasted_iota(jnp.int32, (E, tm), 0)
    work = logits
    vals, idxs, sel = [], [], jnp.zeros((E, tm), F32)
    for _ in range(TOP_K):
        mx = jnp.max(work, axis=0, keepdims=True)
        first = jnp.min(jnp.where(work == mx, eidx, E), axis=0, keepdims=True)
        hit = eidx == first
        vals.append(mx)
        idxs.append(first)
        sel = jnp.where(hit, 1.0, sel)
        work = jnp.where(hit, -jnp.inf, work)
    ex = [jnp.exp(v - vals[0]) for v in vals]
    den = ex[0] + ex[1] + ex[2] + ex[3]
    gate_ref[...] = jnp.concatenate(ex, axis=0) / den
    idx_ref[...] = jnp.concatenate(idxs, axis=0)

    before = jnp.dot(sel.astype(BF16), tri_ref[...], preferred_element_type=F32) + carry_scr[...]
    ranks = [jnp.sum(jnp.where(eidx == i, before, 0.0), axis=0, keepdims=True) for i in idxs]
    rank_ref[...] = jnp.concatenate(ranks, axis=0).astype(jnp.int32)
    carry_scr[...] = carry_scr[...] + jnp.sum(sel, axis=1, keepdims=True)
    cnt_ref[...] = jnp.broadcast_to(carry_scr[...], cnt_ref.shape).astype(jnp.int32)


def _outproj_router(x2, oa, ob, out_norm_a, out_norm_b, w_out, ffn_norm, w_router, b_router, tm=512):
    N, D = x2.shape
    tm = min(tm, N)
    assert N % tm == 0
    E = w_router.shape[1]
    wr_hi, wr_lo = _split_bf16(w_router.T)
    tri = (jnp.arange(tm)[:, None] < jnp.arange(tm)[None, :]).astype(BF16)
    row = lambda i: (i, 0)
    col = lambda i: (0, i)
    const = lambda i: (0, 0)
    return pl.pallas_call(
        _outproj_router_kernel,
        grid=(N // tm,),
        in_specs=[pl.BlockSpec((tm, D), row), pl.BlockSpec((tm, QA_W), row), pl.BlockSpec((tm, QB_W), row),
                  pl.BlockSpec((1, QA_W), const), pl.BlockSpec((1, QB_W), const),
                  pl.BlockSpec((QA_W + QB_W, D), const), pl.BlockSpec((1, D), const),
                  pl.BlockSpec((E, D), const), pl.BlockSpec((E, D), const), pl.BlockSpec((E, 1), const),
                  pl.BlockSpec((tm, tm), const)],
        out_specs=[pl.BlockSpec((tm, D), row), pl.BlockSpec((tm, D), row),
                   pl.BlockSpec((TOP_K, tm), col), pl.BlockSpec((TOP_K, tm), col),
                   pl.BlockSpec((TOP_K, tm), col), pl.BlockSpec((E, LANES), const)],
        out_shape=[jax.ShapeDtypeStruct((N, D), F32), jax.ShapeDtypeStruct((N, D), F32),
                   jax.ShapeDtypeStruct((TOP_K, N), jnp.int32), jax.ShapeDtypeStruct((TOP_K, N), F32),
                   jax.ShapeDtypeStruct((TOP_K, N), jnp.int32), jax.ShapeDtypeStruct((E, LANES), jnp.int32)],
        scratch_shapes=[pltpu.VMEM((E, 1), F32)],
        compiler_params=_cparams("arbitrary"),
        name="outproj_router",
    )(x2, oa, ob, out_norm_a.reshape(1, -1), out_norm_b.reshape(1, -1), w_out.astype(BF16),
      ffn_norm.reshape(1, D), wr_hi, wr_lo, b_router.reshape(E, 1), tri)


EXPERT_ROWS = 256
DISPATCH_CHUNK = 64


def _routing_plan(idx_t, rank_t, counts, n_blocks):
    pad_counts = (counts + EXPERT_ROWS - 1) // EXPERT_ROWS * EXPERT_ROWS
    pad_ends = jnp.cumsum(pad_counts)
    pad_starts = pad_ends - pad_counts
    dest = jnp.take(pad_starts, idx_t) + rank_t
    blk_start = jnp.arange(n_blocks, dtype=jnp.int32) * EXPERT_ROWS
    block_e = jnp.minimum(jnp.sum((pad_ends[None, :] <= blk_start[:, None]).astype(jnp.int32), axis=1),
                          N_EXPERTS - 1)
    n_used = (pad_ends[-1] // EXPERT_ROWS).astype(jnp.int32).reshape(1)
    return dest.astype(jnp.int32), block_e, n_used, pad_starts.astype(jnp.int32), pad_counts.astype(jnp.int32)


def _dispatch_kernel(dest_ref, cnt_ref, pstart_ref, pcnt_ref, hf_ref, xs_ref, zero_scr, sem, zsem, *, n_tok):
    def row_copy(t, k):
        return pltpu.make_async_copy(hf_ref.at[pl.ds(t, 1)], xs_ref.at[pl.ds(dest_ref[k * n_tok + t], 1)], sem)

    def chunk_wait():
        rows = TOP_K * DISPATCH_CHUNK
        pltpu.make_async_copy(hf_ref.at[pl.ds(0, rows)], xs_ref.at[pl.ds(0, rows)], sem).wait()

    n_chunks = n_tok // DISPATCH_CHUNK

    def chunk(ci, _):
        def tok(j, _):
            t = ci * DISPATCH_CHUNK + j
            for k in range(TOP_K):
                row_copy(t, k).start()
            return 0
        lax.fori_loop(0, DISPATCH_CHUNK, tok, 0)

        @pl.when(ci > 0)
        def _():
            chunk_wait()
        return 0

    lax.fori_loop(0, n_chunks, chunk, 0)
    chunk_wait()

    zero_scr[...] = jnp.zeros_like(zero_scr)

    def pad_copy(row):
        return pltpu.make_async_copy(zero_scr, xs_ref.at[pl.ds(row, 1)], zsem)

    def expert(e, _):
        base = pstart_ref[e]

        def pad_row(j, _):
            pad_copy(base + j).start()
            return 0
        lax.fori_loop(cnt_ref[e], pcnt_ref[e], pad_row, 0)

        def pad_wait(j, _):
            pad_copy(base + j).wait()
            return 0
        lax.fori_loop(cnt_ref[e], pcnt_ref[e], pad_wait, 0)
        return 0

    lax.fori_loop(0, N_EXPERTS, expert, 0)


def _dispatch(hf, dest, counts, pad_starts, pad_counts, rows_max):
    N, D = hf.shape
    assert N % DISPATCH_CHUNK == 0
    return pl.pallas_call(
        functools.partial(_dispatch_kernel, n_tok=N),
        grid_spec=pltpu.PrefetchScalarGridSpec(
            num_scalar_prefetch=4, grid=(1,),
            in_specs=[pl.BlockSpec(memory_space=pl.ANY)],
            out_specs=pl.BlockSpec(memory_space=pl.ANY),
            scratch_shapes=[pltpu.VMEM((1, D), F32), pltpu.SemaphoreType.DMA(()), pltpu.SemaphoreType.DMA(())]),
        out_shape=jax.ShapeDtypeStruct((rows_max, D), F32),
        compiler_params=_cparams("arbitrary"),
        name="expert_dispatch",
    )(dest.reshape(-1), counts, pad_starts, pad_counts, hf)


def _expert_kernel(be_ref, nu_ref, xs_ref, wg_ref, bg_ref, wu_ref, bu_ref, wd_ref, bd_ref, ys_ref,
                   wg_bf, wu_bf, wd_bf):
    i = pl.program_id(0)
    prev = be_ref[jnp.maximum(i - 1, 0)]

    @pl.when(i < nu_ref[0])
    def _():
        @pl.when((i == 0) | (be_ref[i] != prev))
        def _():
            wg_bf[...] = wg_ref[0].astype(BF16)
            wu_bf[...] = wu_ref[0].astype(BF16)
            wd_bf[...] = wd_ref[0].astype(BF16)

        xb = xs_ref[...].astype(BF16)
        g = jnp.dot(xb, wg_bf[...], preferred_element_type=F32) + bg_ref[0]
        u = jnp.dot(xb, wu_bf[...], preferred_element_type=F32) + bu_ref[0]
        g = jnp.minimum(g, SWIGLU_LIMIT)
        u = jnp.clip(u, -SWIGLU_LIMIT, SWIGLU_LIMIT)
        act = (u + 1.0) * (g * jax.nn.sigmoid(SWIGLU_ALPHA * g))
        ys_ref[...] = jnp.dot(act.astype(BF16), wd_bf[...], preferred_element_type=F32) + bd_ref[0]


def _experts(xs, block_e, n_used, w_gate, b_gate, w_up, b_up, w_down, b_down):
    rows_max, D = xs.shape
    E, _, F = w_gate.shape
    n_blocks = rows_max // EXPERT_ROWS
    xmap = lambda i, be, nu: (jnp.minimum(i, nu[0] - 1), 0)
    wmap = lambda i, be, nu: (be[i], 0, 0)
    return pl.pallas_call(
        _expert_kernel,
        grid_spec=pltpu.PrefetchScalarGridSpec(
            num_scalar_prefetch=2, grid=(n_blocks,),
            in_specs=[pl.BlockSpec((EXPERT_ROWS, D), xmap),
                      pl.BlockSpec((1, D, F), wmap), pl.BlockSpec((1, 1, F), wmap),
                      pl.BlockSpec((1, D, F), wmap), pl.BlockSpec((1, 1, F), wmap),
                      pl.BlockSpec((1, F, D), wmap), pl.BlockSpec((1, 1, D), wmap)],
            out_specs=pl.BlockSpec((EXPERT_ROWS, D), xmap),
            scratch_shapes=[pltpu.VMEM((D, F), BF16), pltpu.VMEM((D, F), BF16), pltpu.VMEM((F, D), BF16)]),
        out_shape=jax.ShapeDtypeStruct((rows_max, D), F32),
        compiler_params=_cparams("arbitrary"),
        name="expert_ffn",
    )(block_e, n_used, xs, w_gate, b_gate.reshape(E, 1, F), w_up, b_up.reshape(E, 1, F),
      w_down, b_down.reshape(E, 1, D))


COMBINE_TM = 256


def _combine_kernel(dest_ref, x1_ref, gate_ref, fn_ref, ys_ref, o_ref, buf, sem, *, n_tok):
    i = pl.program_id(0)
    tm = COMBINE_TM

    def row_copy(j, k):
        t = i * tm + j
        return pltpu.make_async_copy(ys_ref.at[pl.ds(dest_ref[k * n_tok + t], 1)],
                                     buf.at[k, pl.ds(j, 1)], sem)

    def issue(j, _):
        for k in range(TOP_K):
            row_copy(j, k).start()
        return 0
    lax.fori_loop(0, tm, issue, 0)
    for k in range(TOP_K):
        pltpu.make_async_copy(ys_ref.at[pl.ds(0, tm)], buf.at[k], sem).wait()

    gates = gate_ref[...]
    y = x1_ref[...]
    for k in range(TOP_K):
        y = y + buf[k] * gates[:, k:k + 1]
    ms = jnp.mean(y * y, axis=-1, keepdims=True)
    o_ref[...] = y * lax.rsqrt(ms + EPS) * fn_ref[...]


def _combine(ys, dest, x1, gates_t, final_norm):
    N, D = x1.shape
    tm = COMBINE_TM
    assert N % tm == 0
    return pl.pallas_call(
        functools.partial(_combine_kernel, n_tok=N),
        grid_spec=pltpu.PrefetchScalarGridSpec(
            num_scalar_prefetch=1, grid=(N // tm,),
            in_specs=[pl.BlockSpec((tm, D), lambda i, d: (i, 0)),
                      pl.BlockSpec((tm, TOP_K), lambda i, d: (i, 0)),
                      pl.BlockSpec((1, D), lambda i, d: (0, 0)),
                      pl.BlockSpec(memory_space=pl.ANY)],
            out_specs=pl.BlockSpec((tm, D), lambda i, d: (i, 0)),
            scratch_shapes=[pltpu.VMEM((TOP_K, tm, D), F32), pltpu.SemaphoreType.DMA(())]),
        out_shape=jax.ShapeDtypeStruct((N, D), F32),
        compiler_params=_cparams("arbitrary"),
        name="expert_combine",
    )(dest.reshape(-1), x1, gates_t.T, final_norm.reshape(1, D), ys)


def kernel(x, attn_norm, w_in, q_norm, k_norm, out_norm_a, out_norm_b, w_out, ffn_norm, w_router,
           b_router, w_gate, b_gate, w_up, b_up, w_down, b_down, final_norm):
    B, S, D = x.shape
    x2 = x.reshape(B * S, D)
    qa, ka2, va2, qb, kb, vb = _input_projection(x2, attn_norm[0], w_in[0], q_norm[0], k_norm[0], S)
    oa = _grid_attention(qa, ka2, va2, B, S)
    ob = _dilated_attention(qb, kb, vb, B, S)
    x1, hf, idx_t, gates_t, rank_t, cnt = _outproj_router(
        x2, oa, ob, out_norm_a[0], out_norm_b[0], w_out[0], ffn_norm[0], w_router[0], b_router[0])
    N = B * S
    n_blocks = (N * TOP_K) // EXPERT_ROWS + N_EXPERTS
    counts = cnt[:, 0]
    dest, block_e, n_used, pad_starts, pad_counts = _routing_plan(idx_t, rank_t, counts, n_blocks)
    xs = _dispatch(hf, dest, counts, pad_starts, pad_counts, n_blocks * EXPERT_ROWS)
    ys = _experts(xs, block_e, n_used, w_gate[0], b_gate[0], w_up[0], b_up[0], w_down[0], b_down[0])
    out = _combine(ys, dest, x1, gates_t, final_norm)
    return out.reshape(B, S, D)
```

```python
import functools

import jax
import jax.numpy as jnp
import numpy as np
from jax import lax
from jax.experimental import pallas as pl
from jax.experimental.pallas import tpu as pltpu

F32 = jnp.float32
BF16 = jnp.bfloat16

HEAD_DIM = 64
N_HEADS_A = 8
N_KV_HEADS_A = 2
N_HEADS_B = 8
DILATED_BRANCHES = ((128, 1), (512, 4), (2048, 16))
GRID_W = 64
ROPE_THETA = 10000.0
N_EXPERTS = 32
TOP_K = 4
SWIGLU_LIMIT = 7.0
SWIGLU_ALPHA = 1.702
EPS = 1e-6

LANES = 128
QA_W = N_HEADS_A * HEAD_DIM
KA_W = N_KV_HEADS_A * HEAD_DIM
QB_W = N_HEADS_B * HEAD_DIM
VMEM_LIMIT = 56 * 1024 * 1024


def _cparams(*sem):
    return pltpu.CompilerParams(dimension_semantics=sem, vmem_limit_bytes=VMEM_LIMIT)


def _rope_tables(S):
    pos = jnp.arange(S, dtype=F32)
    row = jnp.floor(pos / GRID_W)
    col = pos - row * GRID_W
    half = HEAD_DIM // 2
    inv_a = ROPE_THETA ** (-jnp.arange(0, half, 2, dtype=F32) / half)
    inv_b = ROPE_THETA ** (-jnp.arange(0, HEAD_DIM, 2, dtype=F32) / HEAD_DIM)
    ang_row = row[:, None] * inv_a[None, :]
    ang_col = col[:, None] * inv_a[None, :]
    ang_1d = pos[:, None] * inv_b[None, :]
    cos_a = jnp.concatenate([jnp.cos(ang_row)] * 2 + [jnp.cos(ang_col)] * 2, axis=-1)
    sin_a = jnp.concatenate([-jnp.sin(ang_row), jnp.sin(ang_row),
                             -jnp.sin(ang_col), jnp.sin(ang_col)], axis=-1)
    cos_b = jnp.concatenate([jnp.cos(ang_1d)] * 2, axis=-1)
    sin_b = jnp.concatenate([-jnp.sin(ang_1d), jnp.sin(ang_1d)], axis=-1)
    two = lambda t: jnp.concatenate([t, t], axis=-1)
    return two(cos_a), two(sin_a), two(cos_b), two(sin_b)


def _rope_tile(x, cos, sin_signed, half):
    lane = lax.broadcasted_iota(jnp.int32, x.shape, 1)
    first = (lane % (2 * half)) < half
    partner = jnp.where(first, pltpu.roll(x, LANES - half, 1), pltpu.roll(x, half, 1))
    return x * cos + partner * sin_signed


def _head_rms(x, seg_mean, gain):
    ms = jnp.dot((x * x).astype(BF16), seg_mean, preferred_element_type=F32)
    return x * lax.rsqrt(ms + EPS) * gain


def _inproj_kernel(x_ref, g_ref, w_ref, qn_ref, kn_ref, seg_ref, cosa_ref, sina_ref, cosb_ref,
                   sinb_ref, qa_ref, ka_ref, va_ref, qb_ref, kb_ref, vb_ref):
    x = x_ref[...]
    ms = jnp.mean(x * x, axis=-1, keepdims=True)
    hn = (x * lax.rsqrt(ms + EPS) * g_ref[...]).astype(BF16)
    proj = jnp.dot(hn, w_ref[...], preferred_element_type=F32)
    seg = seg_ref[...]
    cosa, sina, cosb, sinb = cosa_ref[...], sina_ref[...], cosb_ref[...], sinb_ref[...]
    scale = HEAD_DIM ** -0.5
    lane = lax.broadcasted_iota(jnp.int32, (x.shape[0], LANES), 1)
    lo = lane < HEAD_DIM

    off = 0
    for j in range(QA_W // LANES):
        t = proj[:, off + j * LANES: off + (j + 1) * LANES]
        t = _rope_tile(_head_rms(t, seg, qn_ref[...]), cosa, sina, HEAD_DIM // 4)
        qa_ref[:, j * LANES:(j + 1) * LANES] = (t * (scale * LOG2E)).astype(qa_ref.dtype)
    off += QA_W
    k = _rope_tile(_head_rms(proj[:, off:off + LANES], seg, kn_ref[...]), cosa, sina, HEAD_DIM // 4)
    off += KA_W
    v = proj[:, off:off + LANES]
    off += KA_W
    sw = pltpu.roll(k, HEAD_DIM, 1)
    ka_ref[:, 0:LANES] = jnp.where(lo, k, sw).astype(ka_ref.dtype)
    ka_ref[:, LANES:2 * LANES] = jnp.where(lo, sw, k).astype(ka_ref.dtype)
    vt = v.T
    extra = (lax.broadcasted_iota(jnp.int32, (VT_ROWS - HEAD_DIM, v.shape[0]), 0) == 0).astype(F32)
    va_ref[0] = jnp.concatenate([vt[0:HEAD_DIM], extra], axis=0).astype(va_ref.dtype)
    va_ref[1] = jnp.concatenate([vt[HEAD_DIM:2 * HEAD_DIM], extra], axis=0).astype(va_ref.dtype)
    for j in range(QB_W // LANES):
        t = proj[:, off + j * LANES: off + (j + 1) * LANES]
        qb_ref[:, j * LANES:(j + 1) * LANES] = (
            _rope_tile(t, cosb, sinb, HEAD_DIM // 2) * scale).astype(qb_ref.dtype)
    off += QB_W
    for j in range(QB_W // LANES):
        t = proj[:, off + j * LANES: off + (j + 1) * LANES]
        kb_ref[:, j * LANES:(j + 1) * LANES] = _rope_tile(t, cosb, sinb, HEAD_DIM // 2).astype(kb_ref.dtype)
    off += QB_W
    vb_ref[...] = proj[:, off:off + QB_W].astype(vb_ref.dtype)


def _input_projection(x2, attn_norm, w_in, q_norm, k_norm, S, tm=512):
    N, D = x2.shape
    tm = min(tm, S)
    assert S % tm == 0 and N % S == 0
    n_s = S // tm
    cos_a, sin_a, cos_b, sin_b = _rope_tables(S)
    seg = jnp.kron(jnp.eye(2, dtype=F32), jnp.full((HEAD_DIM, HEAD_DIM), 1.0 / HEAD_DIM, F32)).astype(BF16)
    two = lambda g: jnp.tile(g.reshape(1, HEAD_DIM), (1, 2))
    row = lambda i: (i, 0)
    const = lambda i: (0, 0)
    tab = lambda i: (i % n_s, 0)
    w = w_in.astype(BF16)
    out_w = (QA_W, 2 * LANES, None, QB_W, QB_W, QB_W)
    out_dt = (BF16, BF16, BF16, F32, F32, F32)
    vt_spec = pl.BlockSpec((N_KV_HEADS_A, VT_ROWS, tm), lambda i: (0, 0, i))
    vt_shape = jax.ShapeDtypeStruct((N_KV_HEADS_A, VT_ROWS, N), BF16)
    return pl.pallas_call(
        _inproj_kernel,
        grid=(N // tm,),
        in_specs=[pl.BlockSpec((tm, D), row), pl.BlockSpec((1, D), const),
                  pl.BlockSpec(w.shape, const), pl.BlockSpec((1, LANES), const),
                  pl.BlockSpec((1, LANES), const), pl.BlockSpec((LANES, LANES), const)]
                 + [pl.BlockSpec((tm, LANES), tab)] * 4,
        out_specs=[vt_spec if wd is None else pl.BlockSpec((tm, wd), row) for wd in out_w],
        out_shape=[vt_shape if wd is None else jax.ShapeDtypeStruct((N, wd), dt)
                   for wd, dt in zip(out_w, out_dt)],
        compiler_params=_cparams("parallel"),
        name="input_projection",
    )(x2, attn_norm.reshape(1, D), w, two(q_norm), two(k_norm), seg, cos_a, sin_a, cos_b, sin_b)


VT_ROWS = 80
LOG2E = 1.4426950408889634
ATTN_UNROLL = 4


def _attn_a_kernel(q_ref, k_ref, vt_ref, o_ref, st_scr, pt_scr, *, tk):
    tq = q_ref.shape[0]
    S = k_ref.shape[0]
    sub = lax.broadcasted_iota(jnp.int32, (LANES, tq), 0)
    lo = sub < HEAD_DIM
    q0 = q_ref[:, 0:LANES].astype(F32).T
    q1 = q_ref[:, LANES:2 * LANES].astype(F32).T
    zero = jnp.zeros_like(q0)
    qst = jnp.concatenate([jnp.where(lo, q0, zero), jnp.where(lo, zero, q0),
                           jnp.where(lo, q1, zero), jnp.where(lo, zero, q1)], axis=1).astype(BF16)
    cols = 4 * tq
    n_chunks = S // tk
    assert n_chunks % 2 == 0

    def scores(j, slot):
        start = pl.multiple_of(j * tk, tk)
        st_scr[slot] = jnp.dot(k_ref[pl.ds(start, tk), :], qst, preferred_element_type=F32)

    def softmax_pv(j, slot, m, acc):
        mx = st_scr[slot, 0:8, :]
        for r in range(1, tk // 8):
            mx = jnp.maximum(mx, st_scr[slot, 8 * r:8 * r + 8, :])
        m_new = jnp.maximum(m, jnp.max(mx, axis=0, keepdims=True))
        alpha = jnp.exp2(m - m_new)
        mb = jnp.broadcast_to(m_new, (16, cols))
        for r in range(tk // 16):
            blk = st_scr[slot, 16 * r:16 * r + 16, :]
            pt_scr[16 * r:16 * r + 16, :] = jnp.exp2((blk - mb).astype(BF16))
        start = pl.multiple_of(j * tk, tk)
        vt = vt_ref[0, :, pl.ds(start, tk)]
        acc = alpha * acc + jnp.dot(vt, pt_scr[...], preferred_element_type=F32)
        return m_new, acc

    def body(jj, carry):
        m, acc = carry
        j0 = ATTN_UNROLL * jj
        for u in range(ATTN_UNROLL):
            scores(jnp.minimum(j0 + u + 1, n_chunks - 1), (u + 1) % 2)
            m, acc = softmax_pv(j0 + u, u % 2, m, acc)
        return m, acc

    assert n_chunks % ATTN_UNROLL == 0 and ATTN_UNROLL % 2 == 0
    scores(0, 0)
    init = (jnp.full((1, cols), -jnp.inf, F32), jnp.zeros((VT_ROWS, cols), F32))
    _, acc = lax.fori_loop(0, n_chunks // ATTN_UNROLL, body, init)
    ot = acc[0:HEAD_DIM] / acc[HEAD_DIM:HEAD_DIM + 1]
    for t in range(2):
        pair = jnp.concatenate([ot[:, (2 * t) * tq:(2 * t + 1) * tq],
                                ot[:, (2 * t + 1) * tq:(2 * t + 2) * tq]], axis=0)
        o_ref[:, t * LANES:(t + 1) * LANES] = pair.T.astype(o_ref.dtype)


def _grid_attention(qa, ka2, vat, B, S, tq=256, tk=512):
    N = qa.shape[0]
    tq, tk = min(tq, S), min(tk, S)
    assert S % tq == 0 and S % tk == 0
    nq = S // tq
    return pl.pallas_call(
        functools.partial(_attn_a_kernel, tk=tk),
        grid=(B, N_KV_HEADS_A, nq),
        in_specs=[pl.BlockSpec((tq, 2 * LANES), lambda b, h, i: (b * nq + i, h)),
                  pl.BlockSpec((S, LANES), lambda b, h, i: (b, h)),
                  pl.BlockSpec((1, VT_ROWS, S), lambda b, h, i: (h, 0, b))],
        out_specs=pl.BlockSpec((tq, 2 * LANES), lambda b, h, i: (b * nq + i, h)),
        out_shape=jax.ShapeDtypeStruct((N, QA_W), F32),
        scratch_shapes=[pltpu.VMEM((2, tk, 4 * tq), F32), pltpu.VMEM((tk, 4 * tq), BF16)],
        compiler_params=_cparams("parallel", "parallel", "parallel"),
        name="grid_attention",
    )(qa, ka2, vat)


DIL_QB = 128
DIL_R = 64
DIL_KW = DIL_QB + 2 * DIL_R
DIL_SB = 2048
DIL_UNROLL = 4


def _dilated_kernel(q_ref, k_ref, v_ref, o_ref, acc_scr, m_scr, l_scr, *, seq):
    c = pl.program_id(2)
    lane = lax.broadcasted_iota(jnp.int32, (DIL_QB, LANES), 1)
    lo = lane < HEAD_DIM
    rel = (lax.broadcasted_iota(jnp.int32, (2 * DIL_QB, DIL_KW), 1)
           - lax.broadcasted_iota(jnp.int32, (2 * DIL_QB, DIL_KW), 0) % DIL_QB)

    for bi, (window, d) in enumerate(DILATED_BRANCHES):
        assert window // (2 * d) == DIL_R
        nb = DIL_SB // (DIL_QB * d)
        n_m = seq // d

        def block(it, _, d=d, nb=nb, n_m=n_m, first=(bi == 0)):
            r = it // nb
            i = it % nb
            row0 = r + d * DIL_QB * i
            m0 = (c * DIL_SB) // d + DIL_QB * i
            ks = jnp.clip(m0 - DIL_R, 0, n_m - DIL_KW)
            q = q_ref[pl.ds(row0, DIL_QB, stride=d), :]
            k = k_ref[pl.ds(r + d * ks, DIL_KW, stride=d), :].astype(BF16)
            v = v_ref[pl.ds(r + d * ks, DIL_KW, stride=d), :].astype(BF16)
            zero = jnp.zeros_like(q)
            qs = jnp.concatenate([jnp.where(lo, q, zero), jnp.where(lo, zero, q)], axis=0).astype(BF16)
            s = lax.dot_general(qs, k, (((1,), (1,)), ((), ())), preferred_element_type=F32)
            s = jnp.where(jnp.abs(rel + (ks - m0)) <= DIL_R, s, -1e30)
            mb = jnp.max(s, axis=-1, keepdims=True)
            p = jnp.exp(s - mb)
            lb = jnp.sum(p, axis=-1, keepdims=True)
            pv = jnp.dot(p.astype(BF16), v, preferred_element_type=F32)
            acc_b = jnp.where(lo, pv[0:DIL_QB], pv[DIL_QB:])
            m_b = jnp.where(lo, mb[0:DIL_QB], mb[DIL_QB:])
            l_b = jnp.where(lo, lb[0:DIL_QB], lb[DIL_QB:])
            rows = pl.ds(row0, DIL_QB, stride=d)
            if first:
                acc_scr[rows, :] = acc_b
                m_scr[rows, :] = m_b
                l_scr[rows, :] = l_b
            else:
                m_old = m_scr[rows, :]
                m_new = jnp.maximum(m_old, m_b)
                a_old = jnp.exp(m_old - m_new)
                a_new = jnp.exp(m_b - m_new)
                acc_scr[rows, :] = acc_scr[rows, :] * a_old + acc_b * a_new
                l_scr[rows, :] = l_scr[rows, :] * a_old + l_b * a_new
                m_scr[rows, :] = m_new
            return 0

        lax.fori_loop(0, d * nb, block, 0, unroll=DIL_UNROLL)

    o_ref[...] = (acc_scr[...] / l_scr[...]).astype(o_ref.dtype)


def _dilated_attention(qb, kb, vb, B, S):
    N = qb.shape[0]
    assert S % DIL_SB == 0 and S // DILATED_BRANCHES[-1][1] >= DIL_KW
    nsb = S // DIL_SB
    return pl.pallas_call(
        functools.partial(_dilated_kernel, seq=S),
        grid=(B, QB_W // LANES, nsb),
        in_specs=[pl.BlockSpec((DIL_SB, LANES), lambda b, h, i: (b * nsb + i, h)),
                  pl.BlockSpec((S, LANES), lambda b, h, i: (b, h)),
                  pl.BlockSpec((S, LANES), lambda b, h, i: (b, h))],
        out_specs=pl.BlockSpec((DIL_SB, LANES), lambda b, h, i: (b * nsb + i, h)),
        out_shape=jax.ShapeDtypeStruct((N, QB_W), F32),
        scratch_shapes=[pltpu.VMEM((DIL_SB, LANES), F32)] * 3,
        compiler_params=_cparams("parallel", "parallel", "parallel"),
        name="dilated_attention",
    )(qb, kb, vb)


def _split_bf16(a):
    hi = a.astype(BF16)
    return hi, (a - hi.astype(F32)).astype(BF16)


def _outproj_router_kernel(x_ref, oa_ref, ob_ref, ga_ref, gb_ref, wo_ref, gf_ref, wr_hi_ref, wr_lo_ref,
                           br_ref, tri_ref, x1_ref, hf_ref, idx_ref, gate_ref, rank_ref, cnt_ref,
                           carry_scr):
    @pl.when(pl.program_id(0) == 0)
    def _():
        carry_scr[...] = jnp.zeros_like(carry_scr)

    def rms(t, g):
        return t * lax.rsqrt(jnp.mean(t * t, axis=-1, keepdims=True) + EPS) * g

    mix = jnp.concatenate([rms(oa_ref[...], ga_ref[...]), rms(ob_ref[...], gb_ref[...])], axis=-1)
    x1 = x_ref[...] + jnp.dot(mix.astype(BF16), wo_ref[...], preferred_element_type=F32)
    x1_ref[...] = x1
    hf = rms(x1, gf_ref[...])
    hf_ref[...] = hf

    h_hi, h_lo = _split_bf16(hf)
    nt = (((1,), (1,)), ((), ()))
    logits = (lax.dot_general(wr_hi_ref[...], h_hi, nt, preferred_element_type=F32)
              + lax.dot_general(wr_hi_ref[...], h_lo, nt, preferred_element_type=F32)
              + lax.dot_general(wr_lo_ref[...], h_hi, nt, preferred_element_type=F32)) + br_ref[...]
    E, tm = logits.shape
    eidx = lax.broadcasted_iota(jnp.int32, (E, tm), 0)
    work = logits
    vals, idxs, sel = [], [], jnp.zeros((E, tm), F32)
    for _ in range(TOP_K):
        mx = jnp.max(work, axis=0, keepdims=True)
        first = jnp.min(jnp.where(work == mx, eidx, E), axis=0, keepdims=True)
        hit = eidx == first
        vals.append(mx)
        idxs.append(first)
        sel = jnp.where(hit, 1.0, sel)
        work = jnp.where(hit, -jnp.inf, work)
    ex = [jnp.exp(v - vals[0]) for v in vals]
    den = ex[0] + ex[1] + ex[2] + ex[3]
    gate_ref[...] = jnp.concatenate(ex, axis=0) / den
    idx_ref[...] = jnp.concatenate(idxs, axis=0)

    before = jnp.dot(sel.astype(BF16), tri_ref[...], preferred_element_type=F32) + carry_scr[...]
    ranks = [jnp.sum(jnp.where(eidx == i, before, 0.0), axis=0, keepdims=True) for i in idxs]
    rank_ref[...] = jnp.concatenate(ranks, axis=0).astype(jnp.int32)
    carry_scr[...] = carry_scr[...] + jnp.sum(sel, axis=1, keepdims=True)
    cnt_ref[...] = jnp.broadcast_to(carry_scr[...], cnt_ref.shape).astype(jnp.int32)


def _outproj_router(x2, oa, ob, out_norm_a, out_norm_b, w_out, ffn_norm, w_router, b_router, tm=512):
    N, D = x2.shape
    tm = min(tm, N)
    assert N % tm == 0
    E = w_router.shape[1]
    wr_hi, wr_lo = _split_bf16(w_router.T)
    tri = (jnp.arange(tm)[:, None] < jnp.arange(tm)[None, :]).astype(BF16)
    row = lambda i: (i, 0)
    col = lambda i: (0, i)
    const = lambda i: (0, 0)
    return pl.pallas_call(
        _outproj_router_kernel,
        grid=(N // tm,),
        in_specs=[pl.BlockSpec((tm, D), row), pl.BlockSpec((tm, QA_W), row), pl.BlockSpec((tm, QB_W), row),
                  pl.BlockSpec((1, QA_W), const), pl.BlockSpec((1, QB_W), const),
                  pl.BlockSpec((QA_W + QB_W, D), const), pl.BlockSpec((1, D), const),
                  pl.BlockSpec((E, D), const), pl.BlockSpec((E, D), const), pl.BlockSpec((E, 1), const),
                  pl.BlockSpec((tm, tm), const)],
        out_specs=[pl.BlockSpec((tm, D), row), pl.BlockSpec((tm, D), row),
                   pl.BlockSpec((TOP_K, tm), col), pl.BlockSpec((TOP_K, tm), col),
                   pl.BlockSpec((TOP_K, tm), col), pl.BlockSpec((E, LANES), const)],
        out_shape=[jax.ShapeDtypeStruct((N, D), F32), jax.ShapeDtypeStruct((N, D), F32),
                   jax.ShapeDtypeStruct((TOP_K, N), jnp.int32), jax.ShapeDtypeStruct((TOP_K, N), F32),
                   jax.ShapeDtypeStruct((TOP_K, N), jnp.int32), jax.ShapeDtypeStruct((E, LANES), jnp.int32)],
        scratch_shapes=[pltpu.VMEM((E, 1), F32)],
        compiler_params=_cparams("arbitrary"),
        name="outproj_router",
    )(x2, oa, ob, out_norm_a.reshape(1, -1), out_norm_b.reshape(1, -1), w_out.astype(BF16),
      ffn_norm.reshape(1, D), wr_hi, wr_lo, b_router.reshape(E, 1), tri)


EXPERT_ROWS = 256
DISPATCH_TM = 256


def _routing_plan(idx_t, rank_t, counts, n_blocks):
    pad_counts = (counts + EXPERT_ROWS - 1) // EXPERT_ROWS * EXPERT_ROWS
    pad_ends = jnp.cumsum(pad_counts)
    pad_starts = pad_ends - pad_counts
    eids = jnp.arange(N_EXPERTS, dtype=idx_t.dtype)
    dest = rank_t + jnp.sum(jnp.where(idx_t[..., None] == eids, pad_starts, 0), axis=-1)
    blk_start = jnp.arange(n_blocks, dtype=jnp.int32) * EXPERT_ROWS
    block_e = jnp.minimum(jnp.sum((pad_ends[None, :] <= blk_start[:, None]).astype(jnp.int32), axis=1),
                          N_EXPERTS - 1)
    n_used = (pad_ends[-1] // EXPERT_ROWS).astype(jnp.int32).reshape(1)
    return dest.astype(jnp.int32), block_e, n_used, pad_starts.astype(jnp.int32), pad_counts.astype(jnp.int32)


def _dispatch_kernel(dest_ref, cnt_ref, pstart_ref, pcnt_ref, nu_ref, hf_ref, xs_ref, zero_scr, sem, zsem, *,
                     n_tok):
    i = pl.program_id(0)
    tm = hf_ref.shape[0]

    def tok(j, _):
        t = i * tm + j
        for k in range(TOP_K):
            pltpu.make_async_copy(hf_ref.at[pl.ds(j, 1)], xs_ref.at[pl.ds(dest_ref[k * n_tok + t], 1)],
                                  sem).start()
        return 0
    lax.fori_loop(0, tm, tok, 0)

    @pl.when(i == 0)
    def _():
        zero_scr[...] = jnp.zeros_like(zero_scr)

        def pad_copy(row):
            return pltpu.make_async_copy(zero_scr.at[pl.ds(0, 1)], xs_ref.at[pl.ds(row, 1)], zsem)

        def tail_copy(b):
            return pltpu.make_async_copy(zero_scr, xs_ref.at[pl.ds(b * EXPERT_ROWS, EXPERT_ROWS)], zsem)

        n_blocks = xs_ref.shape[0] // EXPERT_ROWS
        lax.fori_loop(nu_ref[0], n_blocks, lambda b, _: (tail_copy(b).start(), 0)[1], 0)
        lax.fori_loop(nu_ref[0], n_blocks, lambda b, _: (tail_copy(b).wait(), 0)[1], 0)

        def expert(e, _):
            base = pstart_ref[e]

            def pad_row(j, _):
                pad_copy(base + j).start()
                return 0
            lax.fori_loop(cnt_ref[e], pcnt_ref[e], pad_row, 0)

            def pad_wait(j, _):
                pad_copy(base + j).wait()
                return 0
            lax.fori_loop(cnt_ref[e], pcnt_ref[e], pad_wait, 0)
            return 0
        lax.fori_loop(0, N_EXPERTS, expert, 0)

    for k in range(TOP_K):
        pltpu.make_async_copy(hf_ref, xs_ref.at[pl.ds(0, tm)], sem).wait()


def _dispatch(hf, dest, counts, pad_starts, pad_counts, n_used, rows_max):
    N, D = hf.shape
    tm = DISPATCH_TM
    assert N % tm == 0
    return pl.pallas_call(
        functools.partial(_dispatch_kernel, n_tok=N),
        grid_spec=pltpu.PrefetchScalarGridSpec(
            num_scalar_prefetch=5, grid=(N // tm,),
            in_specs=[pl.BlockSpec((tm, D), lambda i, *_: (i, 0))],
            out_specs=pl.BlockSpec(memory_space=pl.ANY),
            scratch_shapes=[pltpu.VMEM((EXPERT_ROWS, D), F32), pltpu.SemaphoreType.DMA(()),
                            pltpu.SemaphoreType.DMA(())]),
        out_shape=jax.ShapeDtypeStruct((rows_max, D), F32),
        compiler_params=_cparams("arbitrary"),
        name="expert_dispatch",
    )(dest.reshape(-1), counts, pad_starts, pad_counts, n_used, hf)


def _expert_kernel(be_ref, nu_ref, xs_ref, wg_ref, bg_ref, wu_ref, bu_ref, wd_ref, bd_ref, ys_ref,
                   wg_bf, wu_bf, wd_bf):
    i = pl.program_id(0)
    prev = be_ref[jnp.maximum(i - 1, 0)]

    @pl.when(i < nu_ref[0])
    def _():
        @pl.when((i == 0) | (be_ref[i] != prev))
        def _():
            wg_bf[...] = wg_ref[0].astype(BF16)
            wu_bf[...] = wu_ref[0].astype(BF16)
            wd_bf[...] = wd_ref[0].astype(BF16)

        xb = xs_ref[...].astype(BF16)
        g = jnp.dot(xb, wg_bf[...], preferred_element_type=F32) + bg_ref[0]
        u = jnp.dot(xb, wu_bf[...], preferred_element_type=F32) + bu_ref[0]
        g = jnp.minimum(g, SWIGLU_LIMIT)
        u = jnp.clip(u, -SWIGLU_LIMIT, SWIGLU_LIMIT)
        act = (u + 1.0) * (g * jax.nn.sigmoid(SWIGLU_ALPHA * g))
        ys_ref[...] = jnp.dot(act.astype(BF16), wd_bf[...], preferred_element_type=F32) + bd_ref[0]

    @pl.when(i >= nu_ref[0])
    def _():
        ys_ref[...] = jnp.zeros_like(ys_ref)


def _experts(xs, block_e, n_used, w_gate, b_gate, w_up, b_up, w_down, b_down):
    rows_max, D = xs.shape
    E, _, F = w_gate.shape
    n_blocks = rows_max // EXPERT_ROWS
    xmap = lambda i, be, nu: (jnp.minimum(i, nu[0] - 1), 0)
    wmap = lambda i, be, nu: (be[i], 0, 0)
    return pl.pallas_call(
        _expert_kernel,
        grid_spec=pltpu.PrefetchScalarGridSpec(
            num_scalar_prefetch=2, grid=(n_blocks,),
            in_specs=[pl.BlockSpec((EXPERT_ROWS, D), xmap),
                      pl.BlockSpec((1, D, F), wmap), pl.BlockSpec((1, 1, F), wmap),
                      pl.BlockSpec((1, D, F), wmap), pl.BlockSpec((1, 1, F), wmap),
                      pl.BlockSpec((1, F, D), wmap), pl.BlockSpec((1, 1, D), wmap)],
            out_specs=pl.BlockSpec((EXPERT_ROWS, D), lambda i, be, nu: (i, 0)),
            scratch_shapes=[pltpu.VMEM((D, F), BF16), pltpu.VMEM((D, F), BF16), pltpu.VMEM((F, D), BF16)]),
        out_shape=jax.ShapeDtypeStruct((rows_max, D), F32),
        compiler_params=_cparams("arbitrary"),
        name="expert_ffn",
    )(block_e, n_used, xs, w_gate, b_gate.reshape(E, 1, F), w_up, b_up.reshape(E, 1, F),
      w_down, b_down.reshape(E, 1, D))


COMBINE_TM = 256


def _combine_kernel(dest_ref, x1_ref, gate_ref, fn_ref, ys_ref, o_ref, buf, sem, *, n_tok):
    i = pl.program_id(0)
    tm = COMBINE_TM

    def row_copy(j, k):
        t = i * tm + j
        return pltpu.make_async_copy(ys_ref.at[pl.ds(dest_ref[k * n_tok + t], 1)],
                                     buf.at[k, pl.ds(j, 1)], sem)

    def issue(j, _):
        for k in range(TOP_K):
            row_copy(j, k).start()
        return 0
    lax.fori_loop(0, tm, issue, 0)
    for k in range(TOP_K):
        pltpu.make_async_copy(ys_ref.at[pl.ds(0, tm)], buf.at[k], sem).wait()

    gates = gate_ref[...]
    y = x1_ref[...]
    for k in range(TOP_K):
        y = y + buf[k] * gates[:, k:k + 1]
    ms = jnp.mean(y * y, axis=-1, keepdims=True)
    o_ref[...] = y * lax.rsqrt(ms + EPS) * fn_ref[...]


def _combine(ys, dest, x1, gates_t, final_norm):
    N, D = x1.shape
    tm = COMBINE_TM
    assert N % tm == 0
    return pl.pallas_call(
        functools.partial(_combine_kernel, n_tok=N),
        grid_spec=pltpu.PrefetchScalarGridSpec(
            num_scalar_prefetch=1, grid=(N // tm,),
            in_specs=[pl.BlockSpec((tm, D), lambda i, d: (i, 0)),
                      pl.BlockSpec((tm, TOP_K), lambda i, d: (i, 0)),
                      pl.BlockSpec((1, D), lambda i, d: (0, 0)),
                      pl.BlockSpec(memory_space=pl.ANY)],
            out_specs=pl.BlockSpec((tm, D), lambda i, d: (i, 0)),
            scratch_shapes=[pltpu.VMEM((TOP_K, tm, D), F32), pltpu.SemaphoreType.DMA(())]),
        out_shape=jax.ShapeDtypeStruct((N, D), F32),
        compiler_params=_cparams("arbitrary"),
        name="expert_combine",
    )(dest.reshape(-1), x1, gates_t.T, final_norm.reshape(1, D), ys)


def kernel(x, attn_norm, w_in, q_norm, k_norm, out_norm_a, out_norm_b, w_out, ffn_norm, w_router,
           b_router, w_gate, b_gate, w_up, b_up, w_down, b_down, final_norm):
    B, S, D = x.shape
    x2 = x.reshape(B * S, D)
    qa, ka2, va2, qb, kb, vb = _input_projection(x2, attn_norm[0], w_in[0], q_norm[0], k_norm[0], S)
    oa = _grid_attention(qa, ka2, va2, B, S)
    ob = _dilated_attention(qb, kb, vb, B, S)
    x1, hf, idx_t, gates_t, rank_t, cnt = _outproj_router(
        x2, oa, ob, out_norm_a[0], out_norm_b[0], w_out[0], ffn_norm[0], w_router[0], b_router[0])
    N = B * S
    n_blocks = (N * TOP_K) // EXPERT_ROWS + N_EXPERTS
    counts = cnt[:, 0]
    dest, block_e, n_used, pad_starts, pad_counts = _routing_plan(idx_t, rank_t, counts, n_blocks)
    xs = _dispatch(hf, dest, counts, pad_starts, pad_counts, n_used, n_blocks * EXPERT_ROWS)
    ys = _experts(xs, block_e, n_used, w_gate[0], b_gate[0], w_up[0], b_up[0], w_down[0], b_down[0])
    out = _combine(ys, dest, x1, gates_t, final_norm)
    return out.reshape(B, S, D)
```

```python
import functools

import jax
import jax.numpy as jnp
import numpy as np
from jax import lax
from jax.experimental import pallas as pl
from jax.experimental.pallas import tpu as pltpu

F32 = jnp.float32
BF16 = jnp.bfloat16

HEAD_DIM = 64
N_HEADS_A = 8
N_KV_HEADS_A = 2
N_HEADS_B = 8
DILATED_BRANCHES = ((128, 1), (512, 4), (2048, 16))
GRID_W = 64
ROPE_THETA = 10000.0
N_EXPERTS = 32
TOP_K = 4
SWIGLU_LIMIT = 7.0
SWIGLU_ALPHA = 1.702
EPS = 1e-6

LANES = 128
QA_W = N_HEADS_A * HEAD_DIM
KA_W = N_KV_HEADS_A * HEAD_DIM
QB_W = N_HEADS_B * HEAD_DIM
VMEM_LIMIT = 56 * 1024 * 1024


def _cparams(*sem):
    return pltpu.CompilerParams(dimension_semantics=sem, vmem_limit_bytes=VMEM_LIMIT)


def _rope_tables(S, tm):
    assert tm % GRID_W == 0 and S % tm == 0
    lane = jnp.arange(LANES)
    i = lane % HEAD_DIM
    half = HEAD_DIM // 2
    t0 = (jnp.arange(S // tm) * tm).astype(F32)[:, None]
    j = jnp.arange(tm).astype(F32)[:, None]
    inv_a = (ROPE_THETA ** (-jnp.arange(0, half, 2, dtype=F32) / half))[i % (half // 2)][None, :]
    is_row = (i < half)[None, :]
    base_a = jnp.where(is_row, jnp.floor(t0 / GRID_W) * inv_a, 0.0)
    jrow = jnp.floor(j / GRID_W)
    offs_a = jnp.where(is_row, jrow, j - jrow * GRID_W) * inv_a
    sgn_a = jnp.where((i // (half // 2)) % 2 == 0, -1.0, 1.0)[None, :].astype(F32)
    inv_b = (ROPE_THETA ** (-jnp.arange(0, HEAD_DIM, 2, dtype=F32) / HEAD_DIM))[i % half][None, :]
    base_b = t0 * inv_b
    offs_b = j * inv_b
    sgn_b = jnp.where(i < half, -1.0, 1.0)[None, :].astype(F32)
    cs = lambda a: jnp.concatenate([jnp.cos(a), jnp.sin(a)], axis=-1)
    return (cs(base_a)[:, None, :], cs(offs_a), sgn_a), (cs(base_b)[:, None, :], cs(offs_b), sgn_b)


def _rope_cos_sin(base_ref, offs_ref, sgn_ref):
    cb, sb = base_ref[0, :, 0:LANES], base_ref[0, :, LANES:2 * LANES]
    co, so = offs_ref[:, 0:LANES], offs_ref[:, LANES:2 * LANES]
    return cb * co - sb * so, (sb * co + cb * so) * sgn_ref[...]


def _rope_tile(x, cos, sin_signed, half):
    lane = lax.broadcasted_iota(jnp.int32, x.shape, 1)
    first = (lane % (2 * half)) < half
    partner = jnp.where(first, pltpu.roll(x, LANES - half, 1), pltpu.roll(x, half, 1))
    return x * cos + partner * sin_signed


def _head_rms(x, seg_mean, gain):
    ms = jnp.dot((x * x).astype(BF16), seg_mean, preferred_element_type=F32)
    return x * lax.rsqrt(ms + EPS) * gain


def _inproj_kernel(x_ref, g_ref, w_ref, qn_ref, kn_ref, seg_ref, base_a_ref, offs_a_ref, sgn_a_ref,
                   base_b_ref, offs_b_ref, sgn_b_ref, qa_ref, ka_ref, va_ref, qb_ref, kb_ref, vb_ref):
    x = x_ref[...]
    ms = jnp.mean(x * x, axis=-1, keepdims=True)
    hn = (x * lax.rsqrt(ms + EPS) * g_ref[...]).astype(BF16)
    proj = jnp.dot(hn, w_ref[...], preferred_element_type=F32)
    seg = seg_ref[...]
    cosa, sina = _rope_cos_sin(base_a_ref, offs_a_ref, sgn_a_ref)
    cosb, sinb = _rope_cos_sin(base_b_ref, offs_b_ref, sgn_b_ref)
    scale = HEAD_DIM ** -0.5
    lane = lax.broadcasted_iota(jnp.int32, (x.shape[0], LANES), 1)
    lo = lane < HEAD_DIM

    off = 0
    for j in range(QA_W // LANES):
        t = proj[:, off + j * LANES: off + (j + 1) * LANES]
        t = _rope_tile(_head_rms(t, seg, qn_ref[...]), cosa, sina, HEAD_DIM // 4)
        qa_ref[:, j * LANES:(j + 1) * LANES] = (t * (scale * LOG2E)).astype(qa_ref.dtype)
    off += QA_W
    k = _rope_tile(_head_rms(proj[:, off:off + LANES], seg, kn_ref[...]), cosa, sina, HEAD_DIM // 4)
    off += KA_W
    v = proj[:, off:off + LANES]
    off += KA_W
    sw = pltpu.roll(k, HEAD_DIM, 1)
    ka_ref[:, 0:LANES] = jnp.where(lo, k, sw).astype(ka_ref.dtype)
    ka_ref[:, LANES:2 * LANES] = jnp.where(lo, sw, k).astype(ka_ref.dtype)
    vt = v.T
    extra = (lax.broadcasted_iota(jnp.int32, (VT_ROWS - HEAD_DIM, v.shape[0]), 0) == 0).astype(F32)
    va_ref[0] = jnp.concatenate([vt[0:HEAD_DIM], extra], axis=0).astype(va_ref.dtype)
    va_ref[1] = jnp.concatenate([vt[HEAD_DIM:2 * HEAD_DIM], extra], axis=0).astype(va_ref.dtype)
    for j in range(QB_W // LANES):
        t = proj[:, off + j * LANES: off + (j + 1) * LANES]
        qb_ref[:, j * LANES:(j + 1) * LANES] = (
            _rope_tile(t, cosb, sinb, HEAD_DIM // 2) * scale).astype(qb_ref.dtype)
    off += QB_W
    for j in range(QB_W // LANES):
        t = proj[:, off + j * LANES: off + (j + 1) * LANES]
        kb_ref[:, j * LANES:(j + 1) * LANES] = _rope_tile(t, cosb, sinb, HEAD_DIM // 2).astype(kb_ref.dtype)
    off += QB_W
    vb_ref[...] = proj[:, off:off + QB_W].astype(vb_ref.dtype)


def _input_projection(x2, attn_norm, w_in, q_norm, k_norm, S, tm=512):
    N, D = x2.shape
    tm = min(tm, S)
    assert S % tm == 0 and N % S == 0
    n_s = S // tm
    rope_a, rope_b = _rope_tables(S, tm)
    seg = jnp.kron(jnp.eye(2, dtype=F32), jnp.full((HEAD_DIM, HEAD_DIM), 1.0 / HEAD_DIM, F32)).astype(BF16)
    two = lambda g: jnp.tile(g.reshape(1, HEAD_DIM), (1, 2))
    row = lambda i: (i, 0)
    const = lambda i: (0, 0)
    rope_specs = [pl.BlockSpec((1, 1, 2 * LANES), lambda i: (i % n_s, 0, 0)),
                  pl.BlockSpec((tm, 2 * LANES), const), pl.BlockSpec((1, LANES), const)]
    w = w_in.astype(BF16)
    out_w = (QA_W, 2 * LANES, None, QB_W, QB_W, QB_W)
    out_dt = (BF16, BF16, BF16, F32, F32, F32)
    vt_spec = pl.BlockSpec((N_KV_HEADS_A, VT_ROWS, tm), lambda i: (0, 0, i))
    vt_shape = jax.ShapeDtypeStruct((N_KV_HEADS_A, VT_ROWS, N), BF16)
    return pl.pallas_call(
        _inproj_kernel,
        grid=(N // tm,),
        in_specs=[pl.BlockSpec((tm, D), row), pl.BlockSpec((1, D), const),
                  pl.BlockSpec(w.shape, const), pl.BlockSpec((1, LANES), const),
                  pl.BlockSpec((1, LANES), const), pl.BlockSpec((LANES, LANES), const)]
                 + rope_specs + rope_specs,
        out_specs=[vt_spec if wd is None else pl.BlockSpec((tm, wd), row) for wd in out_w],
        out_shape=[vt_shape if wd is None else jax.ShapeDtypeStruct((N, wd), dt)
                   for wd, dt in zip(out_w, out_dt)],
        compiler_params=_cparams("parallel"),
        name="input_projection",
    )(x2, attn_norm.reshape(1, D), w, two(q_norm), two(k_norm), seg, *rope_a, *rope_b)


VT_ROWS = 80
LOG2E = 1.4426950408889634
ATTN_UNROLL = 4


def _attn_a_kernel(q_ref, k_ref, vt_ref, o_ref, st_scr, pt_scr, *, tk):
    tq = q_ref.shape[0]
    S = k_ref.shape[0]
    sub = lax.broadcasted_iota(jnp.int32, (LANES, tq), 0)
    lo = sub < HEAD_DIM
    q0 = q_ref[:, 0:LANES].astype(F32).T
    q1 = q_ref[:, LANES:2 * LANES].astype(F32).T
    zero = jnp.zeros_like(q0)
    qst = jnp.concatenate([jnp.where(lo, q0, zero), jnp.where(lo, zero, q0),
                           jnp.where(lo, q1, zero), jnp.where(lo, zero, q1)], axis=1).astype(BF16)
    cols = 4 * tq
    n_chunks = S // tk
    assert n_chunks % 2 == 0

    def scores(j, slot):
        start = pl.multiple_of(j * tk, tk)
        st_scr[slot] = jnp.dot(k_ref[pl.ds(start, tk), :], qst, preferred_element_type=F32)

    def softmax_pv(j, slot, m, acc):
        mx = st_scr[slot, 0:8, :]
        for r in range(1, tk // 8):
            mx = jnp.maximum(mx, st_scr[slot, 8 * r:8 * r + 8, :])
        m_new = jnp.maximum(m, jnp.max(mx, axis=0, keepdims=True))
        alpha = jnp.exp2(m - m_new)
        mb = jnp.broadcast_to(m_new, (16, cols))
        for r in range(tk // 16):
            blk = st_scr[slot, 16 * r:16 * r + 16, :]
            pt_scr[16 * r:16 * r + 16, :] = jnp.exp2((blk - mb).astype(BF16))
        start = pl.multiple_of(j * tk, tk)
        vt = vt_ref[0, :, pl.ds(start, tk)]
        acc = alpha * acc + jnp.dot(vt, pt_scr[...], preferred_element_type=F32)
        return m_new, acc

    def body(jj, carry):
        m, acc = carry
        j0 = ATTN_UNROLL * jj
        for u in range(ATTN_UNROLL):
            scores(jnp.minimum(j0 + u + 1, n_chunks - 1), (u + 1) % 2)
            m, acc = softmax_pv(j0 + u, u % 2, m, acc)
        return m, acc

    assert n_chunks % ATTN_UNROLL == 0 and ATTN_UNROLL % 2 == 0
    scores(0, 0)
    init = (jnp.full((1, cols), -jnp.inf, F32), jnp.zeros((VT_ROWS, cols), F32))
    _, acc = lax.fori_loop(0, n_chunks // ATTN_UNROLL, body, init)
    ot = acc[0:HEAD_DIM] / acc[HEAD_DIM:HEAD_DIM + 1]
    for t in range(2):
        pair = jnp.concatenate([ot[:, (2 * t) * tq:(2 * t + 1) * tq],
                                ot[:, (2 * t + 1) * tq:(2 * t + 2) * tq]], axis=0)
        o_ref[:, t * LANES:(t + 1) * LANES] = pair.T.astype(o_ref.dtype)


def _grid_attention(qa, ka2, vat, B, S, tq=256, tk=512):
    N = qa.shape[0]
    tq, tk = min(tq, S), min(tk, S)
    assert S % tq == 0 and S % tk == 0
    nq = S // tq
    return pl.pallas_call(
        functools.partial(_attn_a_kernel, tk=tk),
        grid=(B, N_KV_HEADS_A, nq),
        in_specs=[pl.BlockSpec((tq, 2 * LANES), lambda b, h, i: (b * nq + i, h)),
                  pl.BlockSpec((S, LANES), lambda b, h, i: (b, h)),
                  pl.BlockSpec((1, VT_ROWS, S), lambda b, h, i: (h, 0, b))],
        out_specs=pl.BlockSpec((tq, 2 * LANES), lambda b, h, i: (b * nq + i, h)),
        out_shape=jax.ShapeDtypeStruct((N, QA_W), F32),
        scratch_shapes=[pltpu.VMEM((2, tk, 4 * tq), F32), pltpu.VMEM((tk, 4 * tq), BF16)],
        compiler_params=_cparams("parallel", "parallel", "parallel"),
        name="grid_attention",
    )(qa, ka2, vat)


DIL_QB = 128
DIL_R = 64
DIL_KW = DIL_QB + 2 * DIL_R
DIL_SB = 2048
DIL_UNROLL = 4


def _dilated_kernel(q_ref, k_ref, v_ref, o_ref, acc_scr, m_scr, l_scr, *, seq):
    c = pl.program_id(2)
    lane = lax.broadcasted_iota(jnp.int32, (DIL_QB, LANES), 1)
    lo = lane < HEAD_DIM
    rel = (lax.broadcasted_iota(jnp.int32, (2 * DIL_QB, DIL_KW), 1)
           - lax.broadcasted_iota(jnp.int32, (2 * DIL_QB, DIL_KW), 0) % DIL_QB)

    for bi, (window, d) in enumerate(DILATED_BRANCHES):
        assert window // (2 * d) == DIL_R
        nb = DIL_SB // (DIL_QB * d)
        n_m = seq // d

        def block(it, _, d=d, nb=nb, n_m=n_m, first=(bi == 0)):
            r = it // nb
            i = it % nb
            row0 = r + d * DIL_QB * i
            m0 = (c * DIL_SB) // d + DIL_QB * i
            ks = jnp.clip(m0 - DIL_R, 0, n_m - DIL_KW)
            q = q_ref[pl.ds(row0, DIL_QB, stride=d), :]
            k = k_ref[pl.ds(r + d * ks, DIL_KW, stride=d), :].astype(BF16)
            v = v_ref[pl.ds(r + d * ks, DIL_KW, stride=d), :].astype(BF16)
            zero = jnp.zeros_like(q)
            qs = jnp.concatenate([jnp.where(lo, q, zero), jnp.where(lo, zero, q)], axis=0).astype(BF16)
            s = lax.dot_general(qs, k, (((1,), (1,)), ((), ())), preferred_element_type=F32)
            s = jnp.where(jnp.abs(rel + (ks - m0)) <= DIL_R, s, -1e30)
            mb = jnp.max(s, axis=-1, keepdims=True)
            p = jnp.exp(s - mb)
            lb = jnp.sum(p, axis=-1, keepdims=True)
            pv = jnp.dot(p.astype(BF16), v, preferred_element_type=F32)
            acc_b = jnp.where(lo, pv[0:DIL_QB], pv[DIL_QB:])
            m_b = jnp.where(lo, mb[0:DIL_QB], mb[DIL_QB:])
            l_b = jnp.where(lo, lb[0:DIL_QB], lb[DIL_QB:])
            rows = pl.ds(row0, DIL_QB, stride=d)
            if first:
                acc_scr[rows, :] = acc_b
                m_scr[rows, :] = m_b
                l_scr[rows, :] = l_b
            else:
                m_old = m_scr[rows, :]
                m_new = jnp.maximum(m_old, m_b)
                a_old = jnp.exp(m_old - m_new)
                a_new = jnp.exp(m_b - m_new)
                acc_scr[rows, :] = acc_scr[rows, :] * a_old + acc_b * a_new
                l_scr[rows, :] = l_scr[rows, :] * a_old + l_b * a_new
                m_scr[rows, :] = m_new
            return 0

        lax.fori_loop(0, d * nb, block, 0, unroll=DIL_UNROLL)

    o_ref[...] = (acc_scr[...] / l_scr[...]).astype(o_ref.dtype)


def _dilated_attention(qb, kb, vb, B, S):
    N = qb.shape[0]
    assert S % DIL_SB == 0 and S // DILATED_BRANCHES[-1][1] >= DIL_KW
    nsb = S // DIL_SB
    return pl.pallas_call(
        functools.partial(_dilated_kernel, seq=S),
        grid=(B, QB_W // LANES, nsb),
        in_specs=[pl.BlockSpec((DIL_SB, LANES), lambda b, h, i: (b * nsb + i, h)),
                  pl.BlockSpec((S, LANES), lambda b, h, i: (b, h)),
                  pl.BlockSpec((S, LANES), lambda b, h, i: (b, h))],
        out_specs=pl.BlockSpec((DIL_SB, LANES), lambda b, h, i: (b * nsb + i, h)),
        out_shape=jax.ShapeDtypeStruct((N, QB_W), F32),
        scratch_shapes=[pltpu.VMEM((DIL_SB, LANES), F32)] * 3,
        compiler_params=_cparams("parallel", "parallel", "parallel"),
        name="dilated_attention",
    )(qb, kb, vb)


SUBLANES = 8


def _store_token_tiles(ref, val):
    rows, d = val.shape
    assert d == SUBLANES * LANES
    for j in range(SUBLANES):
        ref[pl.ds(j, rows, stride=SUBLANES), :] = val[:, j * LANES:(j + 1) * LANES]


def _load_token_tiles(ref, rows, lead=None):
    idx = (lambda j: (pl.ds(j, rows, stride=SUBLANES), slice(None))) if lead is None else (
        lambda j: (lead, pl.ds(j, rows, stride=SUBLANES), slice(None)))
    return jnp.concatenate([ref[idx(j)] for j in range(SUBLANES)], axis=-1)


def _split_bf16(a):
    hi = a.astype(BF16)
    return hi, (a - hi.astype(F32)).astype(BF16)


def _outproj_router_kernel(x_ref, oa_ref, ob_ref, ga_ref, gb_ref, wo_ref, gf_ref, wr_hi_ref, wr_lo_ref,
                           br_ref, tri_ref, x1_ref, hf_ref, idx_ref, gate_ref, rank_ref, cnt_ref,
                           carry_scr):
    @pl.when(pl.program_id(0) == 0)
    def _():
        carry_scr[...] = jnp.zeros_like(carry_scr)

    def rms(t, g):
        return t * lax.rsqrt(jnp.mean(t * t, axis=-1, keepdims=True) + EPS) * g

    mix = jnp.concatenate([rms(oa_ref[...], ga_ref[...]), rms(ob_ref[...], gb_ref[...])], axis=-1)
    x1 = x_ref[...] + jnp.dot(mix.astype(BF16), wo_ref[...], preferred_element_type=F32)
    x1_ref[...] = x1
    hf = rms(x1, gf_ref[...])
    _store_token_tiles(hf_ref, hf)

    h_hi, h_lo = _split_bf16(hf)
    nt = (((1,), (1,)), ((), ()))
    logits = (lax.dot_general(wr_hi_ref[...], h_hi, nt, preferred_element_type=F32)
              + lax.dot_general(wr_hi_ref[...], h_lo, nt, preferred_element_type=F32)
              + lax.dot_general(wr_lo_ref[...], h_hi, nt, preferred_element_type=F32)) + br_ref[...]
    E, tm = logits.shape
    eidx = lax.broadcasted_iota(jnp.int32, (E, tm), 0)
    work = logits
    vals, idxs, sel = [], [], jnp.zeros((E, tm), F32)
    for _ in range(TOP_K):
        mx = jnp.max(work, axis=0, keepdims=True)
        first = jnp.min(jnp.where(work == mx, eidx, E), axis=0, keepdims=True)
        hit = eidx == first
        vals.append(mx)
        idxs.append(first)
        sel = jnp.where(hit, 1.0, sel)
        work = jnp.where(hit, -jnp.inf, work)
    ex = [jnp.exp(v - vals[0]) for v in vals]
    den = ex[0] + ex[1] + ex[2] + ex[3]
    gate_ref[...] = jnp.concatenate(ex, axis=0) / den
    idx_ref[...] = jnp.concatenate(idxs, axis=0)

    before = jnp.dot(sel.astype(BF16), tri_ref[...], preferred_element_type=F32) + carry_scr[...]
    ranks = [jnp.sum(jnp.where(eidx == i, before, 0.0), axis=0, keepdims=True) for i in idxs]
    rank_ref[...] = jnp.concatenate(ranks, axis=0).astype(jnp.int32)
    carry_scr[...] = carry_scr[...] + jnp.sum(sel, axis=1, keepdims=True)
    cnt_ref[...] = jnp.broadcast_to(carry_scr[...], cnt_ref.shape).astype(jnp.int32)


def _outproj_router(x2, oa, ob, out_norm_a, out_norm_b, w_out, ffn_norm, w_router, b_router, tm=512):
    N, D = x2.shape
    tm = min(tm, N)
    assert N % tm == 0
    E = w_router.shape[1]
    wr_hi, wr_lo = _split_bf16(w_router.T)
    tri = (jnp.arange(tm)[:, None] < jnp.arange(tm)[None, :]).astype(BF16)
    row = lambda i: (i, 0)
    col = lambda i: (0, i)
    const = lambda i: (0, 0)
    return pl.pallas_call(
        _outproj_router_kernel,
        grid=(N // tm,),
        in_specs=[pl.BlockSpec((tm, D), row), pl.BlockSpec((tm, QA_W), row), pl.BlockSpec((tm, QB_W), row),
                  pl.BlockSpec((1, QA_W), const), pl.BlockSpec((1, QB_W), const),
                  pl.BlockSpec((QA_W + QB_W, D), const), pl.BlockSpec((1, D), const),
                  pl.BlockSpec((E, D), const), pl.BlockSpec((E, D), const), pl.BlockSpec((E, 1), const),
                  pl.BlockSpec((tm, tm), const)],
        out_specs=[pl.BlockSpec((tm, D), row), pl.BlockSpec((tm * SUBLANES, LANES), row),
                   pl.BlockSpec((TOP_K, tm), col), pl.BlockSpec((TOP_K, tm), col),
                   pl.BlockSpec((TOP_K, tm), col), pl.BlockSpec((E, LANES), const)],
        out_shape=[jax.ShapeDtypeStruct((N, D), F32), jax.ShapeDtypeStruct((N * SUBLANES, LANES), F32),
                   jax.ShapeDtypeStruct((TOP_K, N), jnp.int32), jax.ShapeDtypeStruct((TOP_K, N), F32),
                   jax.ShapeDtypeStruct((TOP_K, N), jnp.int32), jax.ShapeDtypeStruct((E, LANES), jnp.int32)],
        scratch_shapes=[pltpu.VMEM((E, 1), F32)],
        compiler_params=_cparams("arbitrary"),
        name="outproj_router",
    )(x2, oa, ob, out_norm_a.reshape(1, -1), out_norm_b.reshape(1, -1), w_out.astype(BF16),
      ffn_norm.reshape(1, D), wr_hi, wr_lo, b_router.reshape(E, 1), tri)


EXPERT_ROWS = 256
DISPATCH_TM = 256


def _routing_plan(idx_t, rank_t, counts, n_blocks):
    pad_counts = (counts + EXPERT_ROWS - 1) // EXPERT_ROWS * EXPERT_ROWS
    pad_ends = jnp.cumsum(pad_counts)
    pad_starts = pad_ends - pad_counts
    eids = jnp.arange(N_EXPERTS, dtype=idx_t.dtype)
    dest = rank_t + jnp.sum(jnp.where(idx_t[..., None] == eids, pad_starts, 0), axis=-1)
    blk_start = jnp.arange(n_blocks, dtype=jnp.int32) * EXPERT_ROWS
    block_e = jnp.minimum(jnp.sum((pad_ends[None, :] <= blk_start[:, None]).astype(jnp.int32), axis=1),
                          N_EXPERTS - 1)
    n_used = (pad_ends[-1] // EXPERT_ROWS).astype(jnp.int32).reshape(1)
    return dest.astype(jnp.int32), block_e, n_used, pad_starts.astype(jnp.int32), pad_counts.astype(jnp.int32)


def _dispatch_kernel(dest_ref, cnt_ref, pstart_ref, pcnt_ref, nu_ref, hf_ref, xs_ref, zero_scr, sem, zsem, *,
                     n_tok):
    i = pl.program_id(0)
    tm = hf_ref.shape[0] // SUBLANES

    def tile(ref, row):
        return ref.at[pl.ds(pl.multiple_of(row * SUBLANES, SUBLANES), SUBLANES)]

    def tok(j, _):
        t = i * tm + j
        for k in range(TOP_K):
            pltpu.make_async_copy(tile(hf_ref, j), tile(xs_ref, dest_ref[k * n_tok + t]), sem).start(
                priority=k % 2)
        return 0
    lax.fori_loop(0, tm, tok, 0, unroll=2)

    @pl.when(i == 0)
    def _():
        zero_scr[...] = jnp.zeros_like(zero_scr)
        blk = EXPERT_ROWS * SUBLANES

        def pad_copy(row):
            return pltpu.make_async_copy(zero_scr.at[pl.ds(0, SUBLANES)], tile(xs_ref, row), zsem)

        def tail_copy(b):
            return pltpu.make_async_copy(zero_scr, xs_ref.at[pl.ds(pl.multiple_of(b * blk, blk), blk)], zsem)

        n_blocks = xs_ref.shape[0] // blk
        lax.fori_loop(nu_ref[0], n_blocks, lambda b, _: (tail_copy(b).start(), 0)[1], 0)
        lax.fori_loop(nu_ref[0], n_blocks, lambda b, _: (tail_copy(b).wait(), 0)[1], 0)

        def expert(e, _):
            base = pstart_ref[e]

            def pad_row(j, _):
                pad_copy(base + j).start()
                return 0
            lax.fori_loop(cnt_ref[e], pcnt_ref[e], pad_row, 0)

            def pad_wait(j, _):
                pad_copy(base + j).wait()
                return 0
            lax.fori_loop(cnt_ref[e], pcnt_ref[e], pad_wait, 0)
            return 0
        lax.fori_loop(0, N_EXPERTS, expert, 0)

    for k in range(TOP_K):
        pltpu.make_async_copy(hf_ref, xs_ref.at[pl.ds(0, tm * SUBLANES)], sem).wait()


def _dispatch(hf, dest, counts, pad_starts, pad_counts, n_used, rows_max):
    N = hf.shape[0] // SUBLANES
    tm = DISPATCH_TM
    assert N % tm == 0
    return pl.pallas_call(
        functools.partial(_dispatch_kernel, n_tok=N),
        grid_spec=pltpu.PrefetchScalarGridSpec(
            num_scalar_prefetch=5, grid=(N // tm,),
            in_specs=[pl.BlockSpec((tm * SUBLANES, LANES), lambda i, *_: (i, 0))],
            out_specs=pl.BlockSpec(memory_space=pl.ANY),
            scratch_shapes=[pltpu.VMEM((EXPERT_ROWS * SUBLANES, LANES), F32), pltpu.SemaphoreType.DMA(()),
                            pltpu.SemaphoreType.DMA(())]),
        out_shape=jax.ShapeDtypeStruct((rows_max * SUBLANES, LANES), F32),
        compiler_params=_cparams("arbitrary"),
        name="expert_dispatch",
    )(dest.reshape(-1), counts, pad_starts, pad_counts, n_used, hf)


def _expert_kernel(be_ref, nu_ref, xs_ref, wg_ref, bg_ref, wu_ref, bu_ref, wd_ref, bd_ref, ys_ref,
                   wg_bf, wu_bf, wd_bf):
    i = pl.program_id(0)
    prev = be_ref[jnp.maximum(i - 1, 0)]

    @pl.when(i < nu_ref[0])
    def _():
        @pl.when((i == 0) | (be_ref[i] != prev))
        def _():
            wg_bf[...] = wg_ref[0].astype(BF16)
            wu_bf[...] = wu_ref[0].astype(BF16)
            wd_bf[...] = wd_ref[0].astype(BF16)

        xb = _load_token_tiles(xs_ref, EXPERT_ROWS).astype(BF16)
        g = jnp.dot(xb, wg_bf[...], preferred_element_type=F32) + bg_ref[0]
        u = jnp.dot(xb, wu_bf[...], preferred_element_type=F32) + bu_ref[0]
        g = jnp.minimum(g, SWIGLU_LIMIT)
        u = jnp.clip(u, -SWIGLU_LIMIT, SWIGLU_LIMIT)
        act = (u + 1.0) * (g * jax.nn.sigmoid(SWIGLU_ALPHA * g))
        _store_token_tiles(ys_ref, jnp.dot(act.astype(BF16), wd_bf[...], preferred_element_type=F32)
                           + bd_ref[0])

    @pl.when(i >= nu_ref[0])
    def _():
        ys_ref[...] = jnp.zeros_like(ys_ref)


def _experts(xs, block_e, n_used, w_gate, b_gate, w_up, b_up, w_down, b_down):
    E, D, F = w_gate.shape
    blk = EXPERT_ROWS * SUBLANES
    n_blocks = xs.shape[0] // blk
    xmap = lambda i, be, nu: (jnp.minimum(i, nu[0] - 1), 0)
    wmap = lambda i, be, nu: (be[i], 0, 0)
    return pl.pallas_call(
        _expert_kernel,
        grid_spec=pltpu.PrefetchScalarGridSpec(
            num_scalar_prefetch=2, grid=(n_blocks,),
            in_specs=[pl.BlockSpec((blk, LANES), xmap),
                      pl.BlockSpec((1, D, F), wmap), pl.BlockSpec((1, 1, F), wmap),
                      pl.BlockSpec((1, D, F), wmap), pl.BlockSpec((1, 1, F), wmap),
                      pl.BlockSpec((1, F, D), wmap), pl.BlockSpec((1, 1, D), wmap)],
            out_specs=pl.BlockSpec((blk, LANES), lambda i, be, nu: (i, 0)),
            scratch_shapes=[pltpu.VMEM((D, F), BF16), pltpu.VMEM((D, F), BF16), pltpu.VMEM((F, D), BF16)]),
        out_shape=jax.ShapeDtypeStruct(xs.shape, F32),
        compiler_params=_cparams("arbitrary"),
        name="expert_ffn",
    )(block_e, n_used, xs, w_gate, b_gate.reshape(E, 1, F), w_up, b_up.reshape(E, 1, F),
      w_down, b_down.reshape(E, 1, D))


COMBINE_TM = 256


def _combine_kernel(dest_ref, x1_ref, gate_ref, fn_ref, ys_ref, o_ref, buf, sem, *, n_tok):
    i = pl.program_id(0)
    tm = COMBINE_TM

    def tile(row):
        return pl.ds(pl.multiple_of(row * SUBLANES, SUBLANES), SUBLANES)

    def issue(j, _):
        t = i * tm + j
        for k in range(TOP_K):
            pltpu.make_async_copy(ys_ref.at[tile(dest_ref[k * n_tok + t])], buf.at[k, tile(j)], sem).start(
                priority=k % 2)
        return 0
    lax.fori_loop(0, tm, issue, 0, unroll=2)
    for k in range(TOP_K):
        pltpu.make_async_copy(ys_ref.at[pl.ds(0, tm * SUBLANES)], buf.at[k], sem).wait()

    gates = gate_ref[...]
    y = x1_ref[...]
    for k in range(TOP_K):
        y = y + _load_token_tiles(buf, tm, lead=k) * gates[:, k:k + 1]
    ms = jnp.mean(y * y, axis=-1, keepdims=True)
    o_ref[...] = y * lax.rsqrt(ms + EPS) * fn_ref[...]


def _combine(ys, dest, x1, gates_t, final_norm):
    N, D = x1.shape
    tm = COMBINE_TM
    assert N % tm == 0
    return pl.pallas_call(
        functools.partial(_combine_kernel, n_tok=N),
        grid_spec=pltpu.PrefetchScalarGridSpec(
            num_scalar_prefetch=1, grid=(N // tm,),
            in_specs=[pl.BlockSpec((tm, D), lambda i, d: (i, 0)),
                      pl.BlockSpec((tm, TOP_K), lambda i, d: (i, 0)),
                      pl.BlockSpec((1, D), lambda i, d: (0, 0)),
                      pl.BlockSpec(memory_space=pl.ANY)],
            out_specs=pl.BlockSpec((tm, D), lambda i, d: (i, 0)),
            scratch_shapes=[pltpu.VMEM((TOP_K, tm * SUBLANES, LANES), F32), pltpu.SemaphoreType.DMA(())]),
        out_shape=jax.ShapeDtypeStruct((N, D), F32),
        compiler_params=_cparams("arbitrary"),
        name="expert_combine",
    )(dest.reshape(-1), x1, gates_t.T, final_norm.reshape(1, D), ys)


def kernel(x, attn_norm, w_in, q_norm, k_norm, out_norm_a, out_norm_b, w_out, ffn_norm, w_router,
           b_router, w_gate, b_gate, w_up, b_up, w_down, b_down, final_norm):
    B, S, D = x.shape
    x2 = x.reshape(B * S, D)
    qa, ka2, va2, qb, kb, vb = _input_projection(x2, attn_norm[0], w_in[0], q_norm[0], k_norm[0], S)
    oa = _grid_attention(qa, ka2, va2, B, S)
    ob = _dilated_attention(qb, kb, vb, B, S)
    x1, hf, idx_t, gates_t, rank_t, cnt = _outproj_router(
        x2, oa, ob, out_norm_a[0], out_norm_b[0], w_out[0], ffn_norm[0], w_router[0], b_router[0])
    N = B * S
    n_blocks = (N * TOP_K) // EXPERT_ROWS + N_EXPERTS
    counts = cnt[:, 0]
    dest, block_e, n_used, pad_starts, pad_counts = _routing_plan(idx_t, rank_t, counts, n_blocks)
    xs = _dispatch(hf, dest, counts, pad_starts, pad_counts, n_used, n_blocks * EXPERT_ROWS)
    ys = _experts(xs, block_e, n_used, w_gate[0], b_gate[0], w_up[0], b_up[0], w_down[0], b_down[0])
    out = _combine(ys, dest, x1, gates_t, final_norm)
    return out.reshape(B, S, D)
```

```python
import functools

import jax
import jax.numpy as jnp
import numpy as np
from jax import lax
from jax.experimental import pallas as pl
from jax.experimental.pallas import tpu as pltpu

F32 = jnp.float32
BF16 = jnp.bfloat16

HEAD_DIM = 64
N_HEADS_A = 8
N_KV_HEADS_A = 2
N_HEADS_B = 8
DILATED_BRANCHES = ((128, 1), (512, 4), (2048, 16))
GRID_W = 64
ROPE_THETA = 10000.0
N_EXPERTS = 32
TOP_K = 4
SWIGLU_LIMIT = 7.0
SWIGLU_ALPHA = 1.702
EPS = 1e-6

LANES = 128
QA_W = N_HEADS_A * HEAD_DIM
KA_W = N_KV_HEADS_A * HEAD_DIM
QB_W = N_HEADS_B * HEAD_DIM
VMEM_LIMIT = 56 * 1024 * 1024


def _cparams(*sem):
    return pltpu.CompilerParams(dimension_semantics=sem, vmem_limit_bytes=VMEM_LIMIT)


def _rope_tables(S, tm):
    assert tm % GRID_W == 0 and S % tm == 0
    lane = jnp.arange(LANES)
    i = lane % HEAD_DIM
    half = HEAD_DIM // 2
    t0 = (jnp.arange(S // tm) * tm).astype(F32)[:, None]
    j = jnp.arange(tm).astype(F32)[:, None]
    inv_a = (ROPE_THETA ** (-jnp.arange(0, half, 2, dtype=F32) / half))[i % (half // 2)][None, :]
    is_row = (i < half)[None, :]
    base_a = jnp.where(is_row, jnp.floor(t0 / GRID_W) * inv_a, 0.0)
    jrow = jnp.floor(j / GRID_W)
    offs_a = jnp.where(is_row, jrow, j - jrow * GRID_W) * inv_a
    sgn_a = jnp.where((i // (half // 2)) % 2 == 0, -1.0, 1.0)[None, :].astype(F32)
    inv_b = (ROPE_THETA ** (-jnp.arange(0, HEAD_DIM, 2, dtype=F32) / HEAD_DIM))[i % half][None, :]
    base_b = t0 * inv_b
    offs_b = j * inv_b
    sgn_b = jnp.where(i < half, -1.0, 1.0)[None, :].astype(F32)
    cs = lambda a: jnp.concatenate([jnp.cos(a), jnp.sin(a)], axis=-1)
    return (cs(base_a)[:, None, :], cs(offs_a), sgn_a), (cs(base_b)[:, None, :], cs(offs_b), sgn_b)


def _rope_cos_sin(base_ref, offs_ref, sgn_ref):
    cb, sb = base_ref[0, :, 0:LANES], base_ref[0, :, LANES:2 * LANES]
    co, so = offs_ref[:, 0:LANES], offs_ref[:, LANES:2 * LANES]
    return cb * co - sb * so, (sb * co + cb * so) * sgn_ref[...]


def _rope_tile(x, cos, sin_signed, half):
    lane = lax.broadcasted_iota(jnp.int32, x.shape, 1)
    first = (lane % (2 * half)) < half
    partner = jnp.where(first, pltpu.roll(x, LANES - half, 1), pltpu.roll(x, half, 1))
    return x * cos + partner * sin_signed


def _head_rms(x, seg_mean, gain):
    ms = jnp.dot((x * x).astype(BF16), seg_mean, preferred_element_type=F32)
    return x * lax.rsqrt(ms + EPS) * gain


def _inproj_kernel(x_ref, g_ref, w_ref, qn_ref, kn_ref, seg_ref, base_a_ref, offs_a_ref, sgn_a_ref,
                   base_b_ref, offs_b_ref, sgn_b_ref, qa_ref, ka_ref, va_ref, qb_ref, kb_ref, vb_ref):
    x = x_ref[...]
    ms = jnp.mean(x * x, axis=-1, keepdims=True)
    hn = (x * lax.rsqrt(ms + EPS) * g_ref[...]).astype(BF16)
    proj = jnp.dot(hn, w_ref[...], preferred_element_type=F32)
    seg = seg_ref[...]
    cosa, sina = _rope_cos_sin(base_a_ref, offs_a_ref, sgn_a_ref)
    cosb, sinb = _rope_cos_sin(base_b_ref, offs_b_ref, sgn_b_ref)
    scale = HEAD_DIM ** -0.5
    lane = lax.broadcasted_iota(jnp.int32, (x.shape[0], LANES), 1)
    lo = lane < HEAD_DIM

    off = 0
    for j in range(QA_W // LANES):
        t = proj[:, off + j * LANES: off + (j + 1) * LANES]
        t = _rope_tile(_head_rms(t, seg, qn_ref[...]), cosa, sina, HEAD_DIM // 4)
        qa_ref[:, j * LANES:(j + 1) * LANES] = (t * (scale * LOG2E)).astype(qa_ref.dtype)
    off += QA_W
    k = _rope_tile(_head_rms(proj[:, off:off + LANES], seg, kn_ref[...]), cosa, sina, HEAD_DIM // 4)
    off += KA_W
    v = proj[:, off:off + LANES]
    off += KA_W
    sw = pltpu.roll(k, HEAD_DIM, 1)
    ka_ref[:, 0:LANES] = jnp.where(lo, k, sw).astype(ka_ref.dtype)
    ka_ref[:, LANES:2 * LANES] = jnp.where(lo, sw, k).astype(ka_ref.dtype)
    vt = v.T
    extra = (lax.broadcasted_iota(jnp.int32, (VT_ROWS - HEAD_DIM, v.shape[0]), 0) == 0).astype(F32)
    va_ref[0] = jnp.concatenate([vt[0:HEAD_DIM], extra], axis=0).astype(va_ref.dtype)
    va_ref[1] = jnp.concatenate([vt[HEAD_DIM:2 * HEAD_DIM], extra], axis=0).astype(va_ref.dtype)
    for j in range(QB_W // LANES):
        t = proj[:, off + j * LANES: off + (j + 1) * LANES]
        qb_ref[:, j * LANES:(j + 1) * LANES] = (
            _rope_tile(t, cosb, sinb, HEAD_DIM // 2) * (scale * LOG2E)).astype(qb_ref.dtype)
    off += QB_W
    for j in range(QB_W // LANES):
        t = proj[:, off + j * LANES: off + (j + 1) * LANES]
        kb_ref[:, j * LANES:(j + 1) * LANES] = _rope_tile(t, cosb, sinb, HEAD_DIM // 2).astype(kb_ref.dtype)
    off += QB_W
    vb_ref[...] = proj[:, off:off + QB_W].astype(vb_ref.dtype)


def _input_projection(x2, attn_norm, w_in, q_norm, k_norm, S, tm=512):
    N, D = x2.shape
    tm = min(tm, S)
    assert S % tm == 0 and N % S == 0
    n_s = S // tm
    rope_a, rope_b = _rope_tables(S, tm)
    seg = jnp.kron(jnp.eye(2, dtype=F32), jnp.full((HEAD_DIM, HEAD_DIM), 1.0 / HEAD_DIM, F32)).astype(BF16)
    two = lambda g: jnp.tile(g.reshape(1, HEAD_DIM), (1, 2))
    row = lambda i: (i, 0)
    const = lambda i: (0, 0)
    rope_specs = [pl.BlockSpec((1, 1, 2 * LANES), lambda i: (i % n_s, 0, 0)),
                  pl.BlockSpec((tm, 2 * LANES), const), pl.BlockSpec((1, LANES), const)]
    w = w_in.astype(BF16)
    out_w = (QA_W, 2 * LANES, None, QB_W, QB_W, QB_W)
    out_dt = (BF16, BF16, BF16, F32, F32, F32)
    vt_spec = pl.BlockSpec((N_KV_HEADS_A, VT_ROWS, tm), lambda i: (0, 0, i))
    vt_shape = jax.ShapeDtypeStruct((N_KV_HEADS_A, VT_ROWS, N), BF16)
    return pl.pallas_call(
        _inproj_kernel,
        grid=(N // tm,),
        in_specs=[pl.BlockSpec((tm, D), row), pl.BlockSpec((1, D), const),
                  pl.BlockSpec(w.shape, const), pl.BlockSpec((1, LANES), const),
                  pl.BlockSpec((1, LANES), const), pl.BlockSpec((LANES, LANES), const)]
                 + rope_specs + rope_specs,
        out_specs=[vt_spec if wd is None else pl.BlockSpec((tm, wd), row) for wd in out_w],
        out_shape=[vt_shape if wd is None else jax.ShapeDtypeStruct((N, wd), dt)
                   for wd, dt in zip(out_w, out_dt)],
        compiler_params=_cparams("parallel"),
        name="input_projection",
    )(x2, attn_norm.reshape(1, D), w, two(q_norm), two(k_norm), seg, *rope_a, *rope_b)


VT_ROWS = 80
LOG2E = 1.4426950408889634
ATTN_UNROLL = 16


def _attn_a_kernel(q_ref, k_ref, vt_ref, o_ref, st_scr, pt_scr, *, tk):
    tq = q_ref.shape[0]
    S = k_ref.shape[0]
    sub = lax.broadcasted_iota(jnp.int32, (LANES, tq), 0)
    lo = sub < HEAD_DIM
    q0 = q_ref[:, 0:LANES].astype(F32).T
    q1 = q_ref[:, LANES:2 * LANES].astype(F32).T
    zero = jnp.zeros_like(q0)
    qst = jnp.concatenate([jnp.where(lo, q0, zero), jnp.where(lo, zero, q0),
                           jnp.where(lo, q1, zero), jnp.where(lo, zero, q1)], axis=1).astype(BF16)
    cols = 4 * tq
    n_chunks = S // tk
    assert n_chunks % 2 == 0

    def scores(j, slot):
        start = pl.multiple_of(j * tk, tk)
        st_scr[slot] = jnp.dot(k_ref[pl.ds(start, tk), :], qst, preferred_element_type=F32)

    def softmax_pv(j, slot, m, acc):
        mx = st_scr[slot, 0:8, :]
        for r in range(1, tk // 8):
            mx = jnp.maximum(mx, st_scr[slot, 8 * r:8 * r + 8, :])
        m_new = jnp.maximum(m, jnp.max(mx, axis=0, keepdims=True))
        alpha = jnp.exp2(m - m_new)
        mb = jnp.broadcast_to(m_new, (16, cols))
        for r in range(tk // 16):
            blk = st_scr[slot, 16 * r:16 * r + 16, :]
            pt_scr[slot, 16 * r:16 * r + 16, :] = jnp.exp2((blk - mb).astype(BF16))
        start = pl.multiple_of(j * tk, tk)
        vt = vt_ref[0, :, pl.ds(start, tk)]
        acc = alpha * acc + jnp.dot(vt, pt_scr[slot], preferred_element_type=F32)
        return m_new, acc

    def body(jj, carry):
        m, acc = carry
        j0 = ATTN_UNROLL * jj
        for u in range(ATTN_UNROLL):
            scores(jnp.minimum(j0 + u + 1, n_chunks - 1), (u + 1) % 2)
            m, acc = softmax_pv(j0 + u, u % 2, m, acc)
        return m, acc

    assert n_chunks % ATTN_UNROLL == 0 and ATTN_UNROLL % 2 == 0
    scores(0, 0)
    init = (jnp.full((1, cols), -jnp.inf, F32), jnp.zeros((VT_ROWS, cols), F32))
    _, acc = lax.fori_loop(0, n_chunks // ATTN_UNROLL, body, init)
    ot = acc[0:HEAD_DIM] / acc[HEAD_DIM:HEAD_DIM + 1]
    for t in range(2):
        pair = jnp.concatenate([ot[:, (2 * t) * tq:(2 * t + 1) * tq],
                                ot[:, (2 * t + 1) * tq:(2 * t + 2) * tq]], axis=0)
        o_ref[:, t * LANES:(t + 1) * LANES] = pair.T.astype(o_ref.dtype)


def _grid_attention(qa, ka2, vat, B, S, tq=256, tk=256):
    N = qa.shape[0]
    tq, tk = min(tq, S), min(tk, S)
    assert S % tq == 0 and S % tk == 0
    nq = S // tq
    return pl.pallas_call(
        functools.partial(_attn_a_kernel, tk=tk),
        grid=(B, N_KV_HEADS_A, nq),
        in_specs=[pl.BlockSpec((tq, 2 * LANES), lambda b, h, i: (b * nq + i, h)),
                  pl.BlockSpec((S, LANES), lambda b, h, i: (b, h)),
                  pl.BlockSpec((1, VT_ROWS, S), lambda b, h, i: (h, 0, b))],
        out_specs=pl.BlockSpec((tq, 2 * LANES), lambda b, h, i: (b * nq + i, h)),
        out_shape=jax.ShapeDtypeStruct((N, QA_W), F32),
        scratch_shapes=[pltpu.VMEM((2, tk, 4 * tq), F32), pltpu.VMEM((2, tk, 4 * tq), BF16)],
        compiler_params=_cparams("parallel", "parallel", "parallel"),
        name="grid_attention",
    )(qa, ka2, vat)


DIL_QB = 128
DIL_R = 64
DIL_KW = DIL_QB + 2 * DIL_R
DIL_SB = 2048
DIL_UNROLL = 8


DIL_WINDOW_OFFS = (-DIL_R, 0, -2 * DIL_R)


def _dilated_kernel(q_ref, k_ref, v_ref, o_ref, acc_scr, m_scr, l_scr, mask_scr, *, seq):
    c = pl.program_id(2)
    lane = lax.broadcasted_iota(jnp.int32, (DIL_QB, LANES), 1)
    lo = lane < HEAD_DIM
    rel = (lax.broadcasted_iota(jnp.int32, (2 * DIL_QB, DIL_KW), 1)
           - lax.broadcasted_iota(jnp.int32, (2 * DIL_QB, DIL_KW), 0) % DIL_QB)
    for n, off in enumerate(DIL_WINDOW_OFFS):
        mask_scr[n] = jnp.where(jnp.abs(rel + off) <= DIL_R, 0.0, -1e30)
    ones = jnp.ones((DIL_KW, LANES), BF16)

    for bi, (window, d) in enumerate(DILATED_BRANCHES):
        assert window // (2 * d) == DIL_R
        nb = DIL_SB // (DIL_QB * d)
        n_m = seq // d

        def block(it, _, d=d, nb=nb, n_m=n_m, first=(bi == 0)):
            r = it // nb
            i = it % nb
            row0 = r + d * DIL_QB * i
            m0 = (c * DIL_SB) // d + DIL_QB * i
            ks = jnp.clip(m0 - DIL_R, 0, n_m - DIL_KW)
            q = q_ref[pl.ds(row0, DIL_QB, stride=d), :]
            k = k_ref[pl.ds(r + d * ks, DIL_KW, stride=d), :].astype(BF16)
            v = v_ref[pl.ds(r + d * ks, DIL_KW, stride=d), :].astype(BF16)
            zero = jnp.zeros_like(q)
            qs = jnp.concatenate([jnp.where(lo, q, zero), jnp.where(lo, zero, q)], axis=0).astype(BF16)
            s = lax.dot_general(qs, k, (((1,), (1,)), ((), ())), preferred_element_type=F32)
            off = ks - m0
            s = s + mask_scr[jnp.where(off == DIL_WINDOW_OFFS[0], 0, jnp.where(off == DIL_WINDOW_OFFS[1], 1, 2))]
            mb = jnp.max(s, axis=-1, keepdims=True)
            p = jnp.exp2((s - mb).astype(BF16))
            pv = jnp.dot(p, jnp.concatenate([v, ones], axis=1), preferred_element_type=F32)
            acc_b = jnp.where(lo, pv[0:DIL_QB, 0:LANES], pv[DIL_QB:, 0:LANES])
            m_b = jnp.where(lo, mb[0:DIL_QB], mb[DIL_QB:])
            l_b = jnp.where(lo, pv[0:DIL_QB, LANES:], pv[DIL_QB:, LANES:])
            rows = pl.ds(row0, DIL_QB, stride=d)
            if first:
                acc_scr[rows, :] = acc_b
                m_scr[rows, :] = m_b
                l_scr[rows, :] = l_b
            else:
                m_old = m_scr[rows, :]
                m_new = jnp.maximum(m_old, m_b)
                a_old = jnp.exp2(m_old - m_new)
                a_new = jnp.exp2(m_b - m_new)
                acc_scr[rows, :] = acc_scr[rows, :] * a_old + acc_b * a_new
                l_scr[rows, :] = l_scr[rows, :] * a_old + l_b * a_new
                m_scr[rows, :] = m_new
            return 0

        lax.fori_loop(0, d * nb, block, 0, unroll=DIL_UNROLL)

    o_ref[...] = (acc_scr[...] / l_scr[...]).astype(o_ref.dtype)


def _dilated_attention(qb, kb, vb, B, S):
    N = qb.shape[0]
    assert S % DIL_SB == 0 and S // DILATED_BRANCHES[-1][1] >= DIL_KW
    nsb = S // DIL_SB
    return pl.pallas_call(
        functools.partial(_dilated_kernel, seq=S),
        grid=(B, QB_W // LANES, nsb),
        in_specs=[pl.BlockSpec((DIL_SB, LANES), lambda b, h, i: (b * nsb + i, h)),
                  pl.BlockSpec((S, LANES), lambda b, h, i: (b, h)),
                  pl.BlockSpec((S, LANES), lambda b, h, i: (b, h))],
        out_specs=pl.BlockSpec((DIL_SB, LANES), lambda b, h, i: (b * nsb + i, h)),
        out_shape=jax.ShapeDtypeStruct((N, QB_W), F32),
        scratch_shapes=[pltpu.VMEM((DIL_SB, LANES), F32)] * 3
                       + [pltpu.VMEM((len(DIL_WINDOW_OFFS), 2 * DIL_QB, DIL_KW), F32)],
        compiler_params=_cparams("parallel", "parallel", "parallel"),
        name="dilated_attention",
    )(qb, kb, vb)


SUBLANES = 8


def _store_token_tiles(ref, val):
    rows, d = val.shape
    assert d == SUBLANES * LANES
    for j in range(SUBLANES):
        ref[pl.ds(j, rows, stride=SUBLANES), :] = val[:, j * LANES:(j + 1) * LANES]


def _load_token_tiles(ref, rows, lead=None):
    idx = (lambda j: (pl.ds(j, rows, stride=SUBLANES), slice(None))) if lead is None else (
        lambda j: (lead, pl.ds(j, rows, stride=SUBLANES), slice(None)))
    return jnp.concatenate([ref[idx(j)] for j in range(SUBLANES)], axis=-1)


def _split_bf16(a):
    hi = a.astype(BF16)
    return hi, (a - hi.astype(F32)).astype(BF16)


def _outproj_router_kernel(x_ref, oa_ref, ob_ref, ga_ref, gb_ref, wo_ref, gf_ref, wr_hi_ref, wr_lo_ref,
                           br_ref, tri_ref, x1_ref, hf_ref, idx_ref, gate_ref, rank_ref, cnt_ref,
                           carry_scr):
    @pl.when(pl.program_id(0) == 0)
    def _():
        carry_scr[...] = jnp.zeros_like(carry_scr)

    def rms(t, g):
        return t * lax.rsqrt(jnp.mean(t * t, axis=-1, keepdims=True) + EPS) * g

    mix = jnp.concatenate([rms(oa_ref[...], ga_ref[...]), rms(ob_ref[...], gb_ref[...])], axis=-1)
    x1 = x_ref[...] + jnp.dot(mix.astype(BF16), wo_ref[...], preferred_element_type=F32)
    x1_ref[...] = x1
    hf = rms(x1, gf_ref[...])
    _store_token_tiles(hf_ref, hf)

    h_hi, h_lo = _split_bf16(hf)
    nt = (((1,), (1,)), ((), ()))
    logits = (lax.dot_general(wr_hi_ref[...], h_hi, nt, preferred_element_type=F32)
              + lax.dot_general(wr_hi_ref[...], h_lo, nt, preferred_element_type=F32)
              + lax.dot_general(wr_lo_ref[...], h_hi, nt, preferred_element_type=F32)) + br_ref[...]
    E, tm = logits.shape
    eidx = lax.broadcasted_iota(jnp.int32, (E, tm), 0)
    work = logits
    vals, idxs, sel = [], [], jnp.zeros((E, tm), F32)
    for _ in range(TOP_K):
        mx = jnp.max(work, axis=0, keepdims=True)
        first = jnp.min(jnp.where(work == mx, eidx, E), axis=0, keepdims=True)
        hit = eidx == first
        vals.append(mx)
        idxs.append(first)
        sel = jnp.where(hit, 1.0, sel)
        work = jnp.where(hit, -jnp.inf, work)
    ex = [jnp.exp(v - vals[0]) for v in vals]
    den = ex[0] + ex[1] + ex[2] + ex[3]
    gate_ref[...] = jnp.concatenate(ex, axis=0) / den
    idx_ref[...] = jnp.concatenate(idxs, axis=0)

    before = jnp.dot(sel.astype(BF16), tri_ref[...], preferred_element_type=F32) + carry_scr[...]
    ranks = [jnp.sum(jnp.where(eidx == i, before, 0.0), axis=0, keepdims=True) for i in idxs]
    rank_ref[...] = jnp.concatenate(ranks, axis=0).astype(jnp.int32)
    carry_scr[...] = carry_scr[...] + jnp.sum(sel, axis=1, keepdims=True)
    cnt_ref[...] = jnp.broadcast_to(carry_scr[...], cnt_ref.shape).astype(jnp.int32)


def _outproj_router(x2, oa, ob, out_norm_a, out_norm_b, w_out, ffn_norm, w_router, b_router, tm=512):
    N, D = x2.shape
    tm = min(tm, N)
    assert N % tm == 0
    E = w_router.shape[1]
    wr_hi, wr_lo = _split_bf16(w_router.T)
    tri = (jnp.arange(tm)[:, None] < jnp.arange(tm)[None, :]).astype(BF16)
    row = lambda i: (i, 0)
    col = lambda i: (0, i)
    const = lambda i: (0, 0)
    return pl.pallas_call(
        _outproj_router_kernel,
        grid=(N // tm,),
        in_specs=[pl.BlockSpec((tm, D), row), pl.BlockSpec((tm, QA_W), row), pl.BlockSpec((tm, QB_W), row),
                  pl.BlockSpec((1, QA_W), const), pl.BlockSpec((1, QB_W), const),
                  pl.BlockSpec((QA_W + QB_W, D), const), pl.BlockSpec((1, D), const),
                  pl.BlockSpec((E, D), const), pl.BlockSpec((E, D), const), pl.BlockSpec((E, 1), const),
                  pl.BlockSpec((tm, tm), const)],
        out_specs=[pl.BlockSpec((tm, D), row), pl.BlockSpec((tm * SUBLANES, LANES), row),
                   pl.BlockSpec((TOP_K, tm), col), pl.BlockSpec((TOP_K, tm), col),
                   pl.BlockSpec((TOP_K, tm), col), pl.BlockSpec((E, LANES), const)],
        out_shape=[jax.ShapeDtypeStruct((N, D), F32), jax.ShapeDtypeStruct((N * SUBLANES, LANES), F32),
                   jax.ShapeDtypeStruct((TOP_K, N), jnp.int32), jax.ShapeDtypeStruct((TOP_K, N), F32),
                   jax.ShapeDtypeStruct((TOP_K, N), jnp.int32), jax.ShapeDtypeStruct((E, LANES), jnp.int32)],
        scratch_shapes=[pltpu.VMEM((E, 1), F32)],
        compiler_params=_cparams("arbitrary"),
        name="outproj_router",
    )(x2, oa, ob, out_norm_a.reshape(1, -1), out_norm_b.reshape(1, -1), w_out.astype(BF16),
      ffn_norm.reshape(1, D), wr_hi, wr_lo, b_router.reshape(E, 1), tri)


EXPERT_ROWS = 256
DISPATCH_TM = 256


def _routing_plan(idx_t, rank_t, counts, n_blocks):
    pad_counts = (counts + EXPERT_ROWS - 1) // EXPERT_ROWS * EXPERT_ROWS
    pad_ends = jnp.cumsum(pad_counts)
    pad_starts = pad_ends - pad_counts
    eids = jnp.arange(N_EXPERTS, dtype=idx_t.dtype)
    dest = rank_t + jnp.sum(jnp.where(idx_t[..., None] == eids, pad_starts, 0), axis=-1)
    blk_start = jnp.arange(n_blocks, dtype=jnp.int32) * EXPERT_ROWS
    block_e = jnp.minimum(jnp.sum((pad_ends[None, :] <= blk_start[:, None]).astype(jnp.int32), axis=1),
                          N_EXPERTS - 1)
    n_used = (pad_ends[-1] // EXPERT_ROWS).astype(jnp.int32).reshape(1)
    nonempty = pad_counts > 0
    w_slot = ((jnp.cumsum(nonempty) - nonempty) % 2).astype(jnp.int32)
    later = (eids[None, :] > eids[:, None]) & nonempty[None, :]
    w_next = jnp.min(jnp.where(later, eids[None, :], N_EXPERTS), axis=1)
    w_next = jnp.where(w_next == N_EXPERTS, -1, w_next).astype(jnp.int32)
    return (dest.astype(jnp.int32), block_e, n_used, pad_starts.astype(jnp.int32), pad_counts.astype(jnp.int32),
            w_slot, w_next)


def _dispatch_kernel(dest_ref, cnt_ref, pstart_ref, pcnt_ref, nu_ref, hf_ref, xs_ref, zero_scr, sem, zsem, *,
                     n_tok):
    i = pl.program_id(0)
    tm = hf_ref.shape[0] // SUBLANES

    def tile(ref, row):
        return ref.at[pl.ds(pl.multiple_of(row * SUBLANES, SUBLANES), SUBLANES)]

    def tok(j, _):
        t = i * tm + j
        for k in range(TOP_K):
            pltpu.make_async_copy(tile(hf_ref, j), tile(xs_ref, dest_ref[k * n_tok + t]), sem).start(
                priority=k % 2)
        return 0
    lax.fori_loop(0, tm, tok, 0, unroll=2)

    @pl.when(i == 0)
    def _():
        zero_scr[...] = jnp.zeros_like(zero_scr)
        blk = EXPERT_ROWS * SUBLANES

        def pad_copy(row):
            return pltpu.make_async_copy(zero_scr.at[pl.ds(0, SUBLANES)], tile(xs_ref, row), zsem)

        def tail_copy(b):
            return pltpu.make_async_copy(zero_scr, xs_ref.at[pl.ds(pl.multiple_of(b * blk, blk), blk)], zsem)

        n_blocks = xs_ref.shape[0] // blk
        lax.fori_loop(nu_ref[0], n_blocks, lambda b, _: (tail_copy(b).start(), 0)[1], 0)
        lax.fori_loop(nu_ref[0], n_blocks, lambda b, _: (tail_copy(b).wait(), 0)[1], 0)

        def expert(e, _):
            base = pstart_ref[e]

            def pad_row(j, _):
                pad_copy(base + j).start()
                return 0
            lax.fori_loop(cnt_ref[e], pcnt_ref[e], pad_row, 0)

            def pad_wait(j, _):
                pad_copy(base + j).wait()
                return 0
            lax.fori_loop(cnt_ref[e], pcnt_ref[e], pad_wait, 0)
            return 0
        lax.fori_loop(0, N_EXPERTS, expert, 0)

    for k in range(TOP_K):
        pltpu.make_async_copy(hf_ref, xs_ref.at[pl.ds(0, tm * SUBLANES)], sem).wait()


def _dispatch(hf, dest, counts, pad_starts, pad_counts, n_used, rows_max):
    N = hf.shape[0] // SUBLANES
    tm = DISPATCH_TM
    assert N % tm == 0
    return pl.pallas_call(
        functools.partial(_dispatch_kernel, n_tok=N),
        grid_spec=pltpu.PrefetchScalarGridSpec(
            num_scalar_prefetch=5, grid=(N // tm,),
            in_specs=[pl.BlockSpec((tm * SUBLANES, LANES), lambda i, *_: (i, 0))],
            out_specs=pl.BlockSpec(memory_space=pl.ANY),
            scratch_shapes=[pltpu.VMEM((EXPERT_ROWS * SUBLANES, LANES), F32), pltpu.SemaphoreType.DMA(()),
                            pltpu.SemaphoreType.DMA(())]),
        out_shape=jax.ShapeDtypeStruct((rows_max * SUBLANES, LANES), F32),
        compiler_params=_cparams("arbitrary"),
        name="expert_dispatch",
    )(dest.reshape(-1), counts, pad_starts, pad_counts, n_used, hf)


def _expert_kernel(be_ref, nu_ref, pstart_ref, wslot_ref, wnext_ref, xs_ref, wg_hbm, bg_ref, wu_hbm, bu_ref,
                   wd_hbm, bd_ref, ys_ref, wg_buf, wu_buf, wd_buf, wg_bf, wu_bf, wd_bf, sem):
    i = pl.program_id(0)
    e = be_ref[i]

    def weight_copies(expert, slot):
        return [pltpu.make_async_copy(src.at[expert], dst.at[slot], sem.at[slot, n])
                for n, (src, dst) in enumerate(((wg_hbm, wg_buf), (wu_hbm, wu_buf), (wd_hbm, wd_buf)))]

    @pl.when(i < nu_ref[0])
    def _():
        @pl.when(pstart_ref[e] == i * EXPERT_ROWS)
        def _():
            slot = wslot_ref[e]

            @pl.when(i == 0)
            def _():
                for c in weight_copies(e, slot):
                    c.start()
            for c in weight_copies(e, slot):
                c.wait()

            @pl.when(wnext_ref[e] >= 0)
            def _():
                for c in weight_copies(wnext_ref[e], 1 - slot):
                    c.start()
            wg_bf[...] = wg_buf[slot].astype(BF16)
            wu_bf[...] = wu_buf[slot].astype(BF16)
            wd_bf[...] = wd_buf[slot].astype(BF16)

        xb = _load_token_tiles(xs_ref, EXPERT_ROWS).astype(BF16)
        g = jnp.dot(xb, wg_bf[...], preferred_element_type=F32) + bg_ref[0]
        u = jnp.dot(xb, wu_bf[...], preferred_element_type=F32) + bu_ref[0]
        g = jnp.minimum(g, SWIGLU_LIMIT)
        u = jnp.clip(u, -SWIGLU_LIMIT, SWIGLU_LIMIT)
        act = (u + 1.0) * (g * jax.nn.sigmoid(SWIGLU_ALPHA * g))
        _store_token_tiles(ys_ref, jnp.dot(act.astype(BF16), wd_bf[...], preferred_element_type=F32)
                           + bd_ref[0])

    @pl.when(i >= nu_ref[0])
    def _():
        ys_ref[...] = jnp.zeros_like(ys_ref)


def _experts(xs, block_e, n_used, pad_starts, w_slot, w_next, w_gate, b_gate, w_up, b_up, w_down, b_down):
    E, D, F = w_gate.shape
    blk = EXPERT_ROWS * SUBLANES
    n_blocks = xs.shape[0] // blk
    xmap = lambda i, be, nu, *_: (jnp.minimum(i, nu[0] - 1), 0)
    bmap = lambda i, be, *_: (be[i], 0, 0)
    hbm = pl.BlockSpec(memory_space=pl.ANY)
    return pl.pallas_call(
        _expert_kernel,
        grid_spec=pltpu.PrefetchScalarGridSpec(
            num_scalar_prefetch=5, grid=(n_blocks,),
            in_specs=[pl.BlockSpec((blk, LANES), xmap),
                      hbm, pl.BlockSpec((1, 1, F), bmap),
                      hbm, pl.BlockSpec((1, 1, F), bmap),
                      hbm, pl.BlockSpec((1, 1, D), bmap)],
            out_specs=pl.BlockSpec((blk, LANES), lambda i, *_: (i, 0)),
            scratch_shapes=[pltpu.VMEM((2, D, F), F32), pltpu.VMEM((2, D, F), F32), pltpu.VMEM((2, F, D), F32),
                            pltpu.VMEM((D, F), BF16), pltpu.VMEM((D, F), BF16), pltpu.VMEM((F, D), BF16),
                            pltpu.SemaphoreType.DMA((2, 3))]),
        out_shape=jax.ShapeDtypeStruct(xs.shape, F32),
        compiler_params=_cparams("arbitrary"),
        name="expert_ffn",
    )(block_e, n_used, pad_starts, w_slot, w_next, xs, w_gate, b_gate.reshape(E, 1, F), w_up,
      b_up.reshape(E, 1, F), w_down, b_down.reshape(E, 1, D))


COMBINE_TM = 256


def _combine_kernel(dest_ref, x1_ref, gate_ref, fn_ref, ys_ref, o_ref, buf, sem, *, n_tok):
    i = pl.program_id(0)
    tm = COMBINE_TM

    def tile(row):
        return pl.ds(pl.multiple_of(row * SUBLANES, SUBLANES), SUBLANES)

    def issue(j, _):
        t = i * tm + j
        for k in range(TOP_K):
            pltpu.make_async_copy(ys_ref.at[tile(dest_ref[k * n_tok + t])], buf.at[k, tile(j)], sem).start(
                priority=k % 2)
        return 0
    lax.fori_loop(0, tm, issue, 0, unroll=2)
    for k in range(TOP_K):
        pltpu.make_async_copy(ys_ref.at[pl.ds(0, tm * SUBLANES)], buf.at[k], sem).wait()

    gates = gate_ref[...]
    y = x1_ref[...]
    for k in range(TOP_K):
        y = y + _load_token_tiles(buf, tm, lead=k) * gates[:, k:k + 1]
    ms = jnp.mean(y * y, axis=-1, keepdims=True)
    o_ref[...] = y * lax.rsqrt(ms + EPS) * fn_ref[...]


def _combine(ys, dest, x1, gates_t, final_norm):
    N, D = x1.shape
    tm = COMBINE_TM
    assert N % tm == 0
    return pl.pallas_call(
        functools.partial(_combine_kernel, n_tok=N),
        grid_spec=pltpu.PrefetchScalarGridSpec(
            num_scalar_prefetch=1, grid=(N // tm,),
            in_specs=[pl.BlockSpec((tm, D), lambda i, d: (i, 0)),
                      pl.BlockSpec((tm, TOP_K), lambda i, d: (i, 0)),
                      pl.BlockSpec((1, D), lambda i, d: (0, 0)),
                      pl.BlockSpec(memory_space=pl.ANY)],
            out_specs=pl.BlockSpec((tm, D), lambda i, d: (i, 0)),
            scratch_shapes=[pltpu.VMEM((TOP_K, tm * SUBLANES, LANES), F32), pltpu.SemaphoreType.DMA(())]),
        out_shape=jax.ShapeDtypeStruct((N, D), F32),
        compiler_params=_cparams("arbitrary"),
        name="expert_combine",
    )(dest.reshape(-1), x1, gates_t.T, final_norm.reshape(1, D), ys)


def kernel(x, attn_norm, w_in, q_norm, k_norm, out_norm_a, out_norm_b, w_out, ffn_norm, w_router,
           b_router, w_gate, b_gate, w_up, b_up, w_down, b_down, final_norm):
    B, S, D = x.shape
    x2 = x.reshape(B * S, D)
    qa, ka2, va2, qb, kb, vb = _input_projection(x2, attn_norm[0], w_in[0], q_norm[0], k_norm[0], S)
    oa = _grid_attention(qa, ka2, va2, B, S)
    ob = _dilated_attention(qb, kb, vb, B, S)
    x1, hf, idx_t, gates_t, rank_t, cnt = _outproj_router(
        x2, oa, ob, out_norm_a[0], out_norm_b[0], w_out[0], ffn_norm[0], w_router[0], b_router[0])
    N = B * S
    n_blocks = (N * TOP_K) // EXPERT_ROWS + N_EXPERTS
    counts = cnt[:, 0]
    dest, block_e, n_used, pad_starts, pad_counts, w_slot, w_next = _routing_plan(
        idx_t, rank_t, counts, n_blocks)
    xs = _dispatch(hf, dest, counts, pad_starts, pad_counts, n_used, n_blocks * EXPERT_ROWS)
    ys = _experts(xs, block_e, n_used, pad_starts, w_slot, w_next,
                  w_gate[0], b_gate[0], w_up[0], b_up[0], w_down[0], b_down[0])
    out = _combine(ys, dest, x1, gates_t, final_norm)
    return out.reshape(B, S, D)
```

```python
import functools

import jax
import jax.numpy as jnp
import numpy as np
from jax import lax
from jax.experimental import pallas as pl
from jax.experimental.pallas import tpu as pltpu

F32 = jnp.float32
BF16 = jnp.bfloat16

HEAD_DIM = 64
N_HEADS_A = 8
N_KV_HEADS_A = 2
N_HEADS_B = 8
DILATED_BRANCHES = ((128, 1), (512, 4), (2048, 16))
GRID_W = 64
ROPE_THETA = 10000.0
N_EXPERTS = 32
TOP_K = 4
SWIGLU_LIMIT = 7.0
SWIGLU_ALPHA = 1.702
EPS = 1e-6

LANES = 128
QA_W = N_HEADS_A * HEAD_DIM
KA_W = N_KV_HEADS_A * HEAD_DIM
QB_W = N_HEADS_B * HEAD_DIM
VMEM_LIMIT = 56 * 1024 * 1024


def _cparams(*sem):
    return pltpu.CompilerParams(dimension_semantics=sem, vmem_limit_bytes=VMEM_LIMIT)


def _rope_tables(S, tm):
    assert tm % GRID_W == 0 and S % tm == 0
    lane = jnp.arange(LANES)
    i = lane % HEAD_DIM
    half = HEAD_DIM // 2
    t0 = (jnp.arange(S // tm) * tm).astype(F32)[:, None]
    j = jnp.arange(tm).astype(F32)[:, None]
    inv_a = (ROPE_THETA ** (-jnp.arange(0, half, 2, dtype=F32) / half))[i % (half // 2)][None, :]
    is_row = (i < half)[None, :]
    base_a = jnp.where(is_row, jnp.floor(t0 / GRID_W) * inv_a, 0.0)
    jrow = jnp.floor(j / GRID_W)
    offs_a = jnp.where(is_row, jrow, j - jrow * GRID_W) * inv_a
    sgn_a = jnp.where((i // (half // 2)) % 2 == 0, -1.0, 1.0)[None, :].astype(F32)
    inv_b = (ROPE_THETA ** (-jnp.arange(0, HEAD_DIM, 2, dtype=F32) / HEAD_DIM))[i % half][None, :]
    base_b = t0 * inv_b
    offs_b = j * inv_b
    sgn_b = jnp.where(i < half, -1.0, 1.0)[None, :].astype(F32)
    cs = lambda a: jnp.concatenate([jnp.cos(a), jnp.sin(a)], axis=-1)
    return (cs(base_a)[:, None, :], cs(offs_a), sgn_a), (cs(base_b)[:, None, :], cs(offs_b), sgn_b)


def _rope_cos_sin(base_ref, offs_ref, sgn_ref):
    cb, sb = base_ref[0, :, 0:LANES], base_ref[0, :, LANES:2 * LANES]
    co, so = offs_ref[:, 0:LANES], offs_ref[:, LANES:2 * LANES]
    return cb * co - sb * so, (sb * co + cb * so) * sgn_ref[...]


def _rope_tile(x, cos, sin_signed, half):
    lane = lax.broadcasted_iota(jnp.int32, x.shape, 1)
    first = (lane % (2 * half)) < half
    partner = jnp.where(first, pltpu.roll(x, LANES - half, 1), pltpu.roll(x, half, 1))
    return x * cos + partner * sin_signed


def _head_rms(x, seg_mean, gain):
    ms = jnp.dot((x * x).astype(BF16), seg_mean, preferred_element_type=F32)
    return x * lax.rsqrt(ms + EPS) * gain


def _inproj_kernel(x_ref, g_ref, w_ref, qn_ref, kn_ref, seg_ref, base_a_ref, offs_a_ref, sgn_a_ref,
                   base_b_ref, offs_b_ref, sgn_b_ref, qa_ref, ka_ref, va_ref, qb_ref, kb_ref, vb_ref):
    x = x_ref[...]
    ms = jnp.mean(x * x, axis=-1, keepdims=True)
    hn = (x * lax.rsqrt(ms + EPS) * g_ref[...]).astype(BF16)
    proj = jnp.dot(hn, w_ref[...], preferred_element_type=F32)
    seg = seg_ref[...]
    cosa, sina = _rope_cos_sin(base_a_ref, offs_a_ref, sgn_a_ref)
    cosb, sinb = _rope_cos_sin(base_b_ref, offs_b_ref, sgn_b_ref)
    scale = HEAD_DIM ** -0.5
    lane = lax.broadcasted_iota(jnp.int32, (x.shape[0], LANES), 1)
    lo = lane < HEAD_DIM

    off = 0
    for j in range(QA_W // LANES):
        t = proj[:, off + j * LANES: off + (j + 1) * LANES]
        t = _rope_tile(_head_rms(t, seg, qn_ref[...]), cosa, sina, HEAD_DIM // 4)
        qa_ref[:, j * LANES:(j + 1) * LANES] = (t * (scale * LOG2E)).astype(qa_ref.dtype)
    off += QA_W
    k = _rope_tile(_head_rms(proj[:, off:off + LANES], seg, kn_ref[...]), cosa, sina, HEAD_DIM // 4)
    off += KA_W
    v = proj[:, off:off + LANES]
    off += KA_W
    sw = pltpu.roll(k, HEAD_DIM, 1)
    ka_ref[:, 0:LANES] = jnp.where(lo, k, sw).astype(ka_ref.dtype)
    ka_ref[:, LANES:2 * LANES] = jnp.where(lo, sw, k).astype(ka_ref.dtype)
    vt = v.T
    extra = (lax.broadcasted_iota(jnp.int32, (VT_ROWS - HEAD_DIM, v.shape[0]), 0) == 0).astype(F32)
    va_ref[0] = jnp.concatenate([vt[0:HEAD_DIM], extra], axis=0).astype(va_ref.dtype)
    va_ref[1] = jnp.concatenate([vt[HEAD_DIM:2 * HEAD_DIM], extra], axis=0).astype(va_ref.dtype)
    for j in range(QB_W // LANES):
        t = proj[:, off + j * LANES: off + (j + 1) * LANES]
        qb_ref[:, j * LANES:(j + 1) * LANES] = (
            _rope_tile(t, cosb, sinb, HEAD_DIM // 2) * (scale * LOG2E)).astype(qb_ref.dtype)
    off += QB_W
    for j in range(QB_W // LANES):
        t = proj[:, off + j * LANES: off + (j + 1) * LANES]
        kb_ref[:, j * LANES:(j + 1) * LANES] = _rope_tile(t, cosb, sinb, HEAD_DIM // 2).astype(kb_ref.dtype)
    off += QB_W
    vb_ref[...] = proj[:, off:off + QB_W].astype(vb_ref.dtype)


def _input_projection(x2, attn_norm, w_in, q_norm, k_norm, S, tm=512):
    N, D = x2.shape
    tm = min(tm, S)
    assert S % tm == 0 and N % S == 0
    n_s = S // tm
    rope_a, rope_b = _rope_tables(S, tm)
    seg = jnp.kron(jnp.eye(2, dtype=F32), jnp.full((HEAD_DIM, HEAD_DIM), 1.0 / HEAD_DIM, F32)).astype(BF16)
    two = lambda g: jnp.tile(g.reshape(1, HEAD_DIM), (1, 2))
    row = lambda i: (i, 0)
    const = lambda i: (0, 0)
    rope_specs = [pl.BlockSpec((1, 1, 2 * LANES), lambda i: (i % n_s, 0, 0)),
                  pl.BlockSpec((tm, 2 * LANES), const), pl.BlockSpec((1, LANES), const)]
    w = w_in.astype(BF16)
    out_w = (QA_W, 2 * LANES, None, QB_W, QB_W, QB_W)
    out_dt = (BF16, BF16, BF16, F32, F32, F32)
    vt_spec = pl.BlockSpec((N_KV_HEADS_A, VT_ROWS, tm), lambda i: (0, 0, i))
    vt_shape = jax.ShapeDtypeStruct((N_KV_HEADS_A, VT_ROWS, N), BF16)
    return pl.pallas_call(
        _inproj_kernel,
        grid=(N // tm,),
        in_specs=[pl.BlockSpec((tm, D), row), pl.BlockSpec((1, D), const),
                  pl.BlockSpec(w.shape, const), pl.BlockSpec((1, LANES), const),
                  pl.BlockSpec((1, LANES), const), pl.BlockSpec((LANES, LANES), const)]
                 + rope_specs + rope_specs,
        out_specs=[vt_spec if wd is None else pl.BlockSpec((tm, wd), row) for wd in out_w],
        out_shape=[vt_shape if wd is None else jax.ShapeDtypeStruct((N, wd), dt)
                   for wd, dt in zip(out_w, out_dt)],
        compiler_params=_cparams("parallel"),
        name="input_projection",
    )(x2, attn_norm.reshape(1, D), w, two(q_norm), two(k_norm), seg, *rope_a, *rope_b)


VT_ROWS = 80
LOG2E = 1.4426950408889634
ATTN_UNROLL = 16


def _attn_a_kernel(q_ref, k_ref, vt_ref, o_ref, st_scr, pt_scr, *, tk):
    tq = q_ref.shape[0]
    S = k_ref.shape[0]
    sub = lax.broadcasted_iota(jnp.int32, (LANES, tq), 0)
    lo = sub < HEAD_DIM
    q0 = q_ref[:, 0:LANES].astype(F32).T
    q1 = q_ref[:, LANES:2 * LANES].astype(F32).T
    zero = jnp.zeros_like(q0)
    qst = jnp.concatenate([jnp.where(lo, q0, zero), jnp.where(lo, zero, q0),
                           jnp.where(lo, q1, zero), jnp.where(lo, zero, q1)], axis=1).astype(BF16)
    cols = 4 * tq
    n_chunks = S // tk
    assert n_chunks % 2 == 0

    def scores(j, slot):
        start = pl.multiple_of(j * tk, tk)
        st_scr[slot] = jnp.dot(k_ref[pl.ds(start, tk), :], qst, preferred_element_type=F32)

    def softmax_pv(j, slot, m, acc):
        mx = st_scr[slot, 0:8, :]
        for r in range(1, tk // 8):
            mx = jnp.maximum(mx, st_scr[slot, 8 * r:8 * r + 8, :])
        m_new = jnp.maximum(m, jnp.max(mx, axis=0, keepdims=True))
        alpha = jnp.exp2(m - m_new)
        mb = jnp.broadcast_to(m_new, (16, cols))
        for r in range(tk // 16):
            blk = st_scr[slot, 16 * r:16 * r + 16, :]
            pt_scr[slot, 16 * r:16 * r + 16, :] = jnp.exp2((blk - mb).astype(BF16))
        start = pl.multiple_of(j * tk, tk)
        vt = vt_ref[0, :, pl.ds(start, tk)]
        acc = alpha * acc + jnp.dot(vt, pt_scr[slot], preferred_element_type=F32)
        return m_new, acc

    def body(jj, carry):
        m, acc = carry
        j0 = ATTN_UNROLL * jj
        for u in range(ATTN_UNROLL):
            scores(jnp.minimum(j0 + u + 1, n_chunks - 1), (u + 1) % 2)
            m, acc = softmax_pv(j0 + u, u % 2, m, acc)
        return m, acc

    assert n_chunks % ATTN_UNROLL == 0 and ATTN_UNROLL % 2 == 0
    scores(0, 0)
    init = (jnp.full((1, cols), -jnp.inf, F32), jnp.zeros((VT_ROWS, cols), F32))
    _, acc = lax.fori_loop(0, n_chunks // ATTN_UNROLL, body, init)
    ot = acc[0:HEAD_DIM] / acc[HEAD_DIM:HEAD_DIM + 1]
    for t in range(2):
        pair = jnp.concatenate([ot[:, (2 * t) * tq:(2 * t + 1) * tq],
                                ot[:, (2 * t + 1) * tq:(2 * t + 2) * tq]], axis=0)
        o_ref[:, t * LANES:(t + 1) * LANES] = pair.T.astype(o_ref.dtype)


def _grid_attention(qa, ka2, vat, B, S, tq=256, tk=256):
    N = qa.shape[0]
    tq, tk = min(tq, S), min(tk, S)
    assert S % tq == 0 and S % tk == 0
    nq = S // tq
    return pl.pallas_call(
        functools.partial(_attn_a_kernel, tk=tk),
        grid=(B, N_KV_HEADS_A, nq),
        in_specs=[pl.BlockSpec((tq, 2 * LANES), lambda b, h, i: (b * nq + i, h)),
                  pl.BlockSpec((S, LANES), lambda b, h, i: (b, h)),
                  pl.BlockSpec((1, VT_ROWS, S), lambda b, h, i: (h, 0, b))],
        out_specs=pl.BlockSpec((tq, 2 * LANES), lambda b, h, i: (b * nq + i, h)),
        out_shape=jax.ShapeDtypeStruct((N, QA_W), F32),
        scratch_shapes=[pltpu.VMEM((2, tk, 4 * tq), F32), pltpu.VMEM((2, tk, 4 * tq), BF16)],
        compiler_params=_cparams("parallel", "parallel", "parallel"),
        name="grid_attention",
    )(qa, ka2, vat)


DIL_QB = 128
DIL_R = 64
DIL_KW = DIL_QB + 2 * DIL_R
DIL_SB = 2048
DIL_UNROLL = 8


DIL_WINDOW_OFFS = (-DIL_R, 0, -2 * DIL_R)


def _dilated_kernel(q_ref, k_ref, v_ref, o_ref, acc_scr, m_scr, l_scr, mask_scr, *, seq):
    c = pl.program_id(2)
    lane = lax.broadcasted_iota(jnp.int32, (DIL_QB, LANES), 1)
    lo = lane < HEAD_DIM
    rel = (lax.broadcasted_iota(jnp.int32, (2 * DIL_QB, DIL_KW), 1)
           - lax.broadcasted_iota(jnp.int32, (2 * DIL_QB, DIL_KW), 0) % DIL_QB)
    for n, off in enumerate(DIL_WINDOW_OFFS):
        mask_scr[n] = jnp.where(jnp.abs(rel + off) <= DIL_R, 0.0, -1e30)
    ones = jnp.ones((DIL_KW, LANES), BF16)

    for bi, (window, d) in enumerate(DILATED_BRANCHES):
        assert window // (2 * d) == DIL_R
        nb = DIL_SB // (DIL_QB * d)
        n_m = seq // d

        def block(it, _, d=d, nb=nb, n_m=n_m, first=(bi == 0)):
            r = it // nb
            i = it % nb
            row0 = r + d * DIL_QB * i
            m0 = (c * DIL_SB) // d + DIL_QB * i
            ks = jnp.clip(m0 - DIL_R, 0, n_m - DIL_KW)
            q = q_ref[pl.ds(row0, DIL_QB, stride=d), :]
            k = k_ref[pl.ds(r + d * ks, DIL_KW, stride=d), :].astype(BF16)
            v = v_ref[pl.ds(r + d * ks, DIL_KW, stride=d), :].astype(BF16)
            zero = jnp.zeros_like(q)
            qs = jnp.concatenate([jnp.where(lo, q, zero), jnp.where(lo, zero, q)], axis=0).astype(BF16)
            s = lax.dot_general(qs, k, (((1,), (1,)), ((), ())), preferred_element_type=F32)
            off = ks - m0
            s = s + mask_scr[jnp.where(off == DIL_WINDOW_OFFS[0], 0, jnp.where(off == DIL_WINDOW_OFFS[1], 1, 2))]
            mb = jnp.max(s, axis=-1, keepdims=True)
            p = jnp.exp2((s - mb).astype(BF16))
            pv = jnp.dot(p, jnp.concatenate([v, ones], axis=1), preferred_element_type=F32)
            acc_b = jnp.where(lo, pv[0:DIL_QB, 0:LANES], pv[DIL_QB:, 0:LANES])
            m_b = jnp.where(lo, mb[0:DIL_QB], mb[DIL_QB:])
            l_b = jnp.where(lo, pv[0:DIL_QB, LANES:], pv[DIL_QB:, LANES:])
            rows = pl.ds(row0, DIL_QB, stride=d)
            if first:
                acc_scr[rows, :] = acc_b
                m_scr[rows, :] = m_b
                l_scr[rows, :] = l_b
            else:
                m_old = m_scr[rows, :]
                m_new = jnp.maximum(m_old, m_b)
                a_old = jnp.exp2(m_old - m_new)
                a_new = jnp.exp2(m_b - m_new)
                acc_scr[rows, :] = acc_scr[rows, :] * a_old + acc_b * a_new
                l_scr[rows, :] = l_scr[rows, :] * a_old + l_b * a_new
                m_scr[rows, :] = m_new
            return 0

        lax.fori_loop(0, d * nb, block, 0, unroll=DIL_UNROLL)

    o_ref[...] = (acc_scr[...] / l_scr[...]).astype(o_ref.dtype)


def _dilated_attention(qb, kb, vb, B, S):
    N = qb.shape[0]
    assert S % DIL_SB == 0 and S // DILATED_BRANCHES[-1][1] >= DIL_KW
    nsb = S // DIL_SB
    return pl.pallas_call(
        functools.partial(_dilated_kernel, seq=S),
        grid=(B, QB_W // LANES, nsb),
        in_specs=[pl.BlockSpec((DIL_SB, LANES), lambda b, h, i: (b * nsb + i, h)),
                  pl.BlockSpec((S, LANES), lambda b, h, i: (b, h)),
                  pl.BlockSpec((S, LANES), lambda b, h, i: (b, h))],
        out_specs=pl.BlockSpec((DIL_SB, LANES), lambda b, h, i: (b * nsb + i, h)),
        out_shape=jax.ShapeDtypeStruct((N, QB_W), F32),
        scratch_shapes=[pltpu.VMEM((DIL_SB, LANES), F32)] * 3
                       + [pltpu.VMEM((len(DIL_WINDOW_OFFS), 2 * DIL_QB, DIL_KW), F32)],
        compiler_params=_cparams("parallel", "parallel", "parallel"),
        name="dilated_attention",
    )(qb, kb, vb)


SUBLANES = 8


def _store_token_tiles(ref, val):
    rows, d = val.shape
    assert d == SUBLANES * LANES
    for j in range(SUBLANES):
        ref[pl.ds(j, rows, stride=SUBLANES), :] = val[:, j * LANES:(j + 1) * LANES]


def _load_token_tiles(ref, rows, lead=None):
    idx = (lambda j: (pl.ds(j, rows, stride=SUBLANES), slice(None))) if lead is None else (
        lambda j: (lead, pl.ds(j, rows, stride=SUBLANES), slice(None)))
    return jnp.concatenate([ref[idx(j)] for j in range(SUBLANES)], axis=-1)


def _split_bf16(a):
    hi = a.astype(BF16)
    return hi, (a - hi.astype(F32)).astype(BF16)


def _outproj_router_kernel(x_ref, oa_ref, ob_ref, ga_ref, gb_ref, wo_ref, gf_ref, wr_hi_ref, wr_lo_ref,
                           br_ref, tri_ref, x1_ref, hf_ref, idx_ref, gate_ref, rank_ref, cnt_ref,
                           carry_scr):
    @pl.when(pl.program_id(0) == 0)
    def _():
        carry_scr[...] = jnp.zeros_like(carry_scr)

    def rms(t, g):
        return t * lax.rsqrt(jnp.mean(t * t, axis=-1, keepdims=True) + EPS) * g

    mix = jnp.concatenate([rms(oa_ref[...], ga_ref[...]), rms(ob_ref[...], gb_ref[...])], axis=-1)
    x1 = x_ref[...] + jnp.dot(mix.astype(BF16), wo_ref[...], preferred_element_type=F32)
    x1_ref[...] = x1
    hf = rms(x1, gf_ref[...])
    _store_token_tiles(hf_ref, hf)

    h_hi, h_lo = _split_bf16(hf)
    nt = (((1,), (1,)), ((), ()))
    logits = (lax.dot_general(wr_hi_ref[...], h_hi, nt, preferred_element_type=F32)
              + lax.dot_general(wr_hi_ref[...], h_lo, nt, preferred_element_type=F32)
              + lax.dot_general(wr_lo_ref[...], h_hi, nt, preferred_element_type=F32)) + br_ref[...]
    E, tm = logits.shape
    eidx = lax.broadcasted_iota(jnp.int32, (E, tm), 0)
    work = logits
    vals, idxs, sel = [], [], jnp.zeros((E, tm), F32)
    for _ in range(TOP_K):
        mx = jnp.max(work, axis=0, keepdims=True)
        first = jnp.min(jnp.where(work == mx, eidx, E), axis=0, keepdims=True)
        hit = eidx == first
        vals.append(mx)
        idxs.append(first)
        sel = jnp.where(hit, 1.0, sel)
        work = jnp.where(hit, -jnp.inf, work)
    ex = [jnp.exp(v - vals[0]) for v in vals]
    den = ex[0] + ex[1] + ex[2] + ex[3]
    gate_ref[...] = jnp.concatenate(ex, axis=0) / den
    idx_ref[...] = jnp.concatenate(idxs, axis=0)

    before = jnp.dot(sel.astype(BF16), tri_ref[...], preferred_element_type=F32) + carry_scr[...]
    ranks = [jnp.sum(jnp.where(eidx == i, before, 0.0), axis=0, keepdims=True) for i in idxs]
    rank_ref[...] = jnp.concatenate(ranks, axis=0).astype(jnp.int32)
    carry_scr[...] = carry_scr[...] + jnp.sum(sel, axis=1, keepdims=True)
    cnt_ref[...] = jnp.broadcast_to(carry_scr[...], cnt_ref.shape).astype(jnp.int32)


def _outproj_router(x2, oa, ob, out_norm_a, out_norm_b, w_out, ffn_norm, w_router, b_router, tm=512):
    N, D = x2.shape
    tm = min(tm, N)
    assert N % tm == 0
    E = w_router.shape[1]
    wr_hi, wr_lo = _split_bf16(w_router.T)
    tri = (jnp.arange(tm)[:, None] < jnp.arange(tm)[None, :]).astype(BF16)
    row = lambda i: (i, 0)
    col = lambda i: (0, i)
    const = lambda i: (0, 0)
    return pl.pallas_call(
        _outproj_router_kernel,
        grid=(N // tm,),
        in_specs=[pl.BlockSpec((tm, D), row), pl.BlockSpec((tm, QA_W), row), pl.BlockSpec((tm, QB_W), row),
                  pl.BlockSpec((1, QA_W), const), pl.BlockSpec((1, QB_W), const),
                  pl.BlockSpec((QA_W + QB_W, D), const), pl.BlockSpec((1, D), const),
                  pl.BlockSpec((E, D), const), pl.BlockSpec((E, D), const), pl.BlockSpec((E, 1), const),
                  pl.BlockSpec((tm, tm), const)],
        out_specs=[pl.BlockSpec((tm, D), row), pl.BlockSpec((tm * SUBLANES, LANES), row),
                   pl.BlockSpec((TOP_K, tm), col), pl.BlockSpec((TOP_K, tm), col),
                   pl.BlockSpec((TOP_K, tm), col), pl.BlockSpec((E, LANES), const)],
        out_shape=[jax.ShapeDtypeStruct((N, D), F32), jax.ShapeDtypeStruct((N * SUBLANES, LANES), F32),
                   jax.ShapeDtypeStruct((TOP_K, N), jnp.int32), jax.ShapeDtypeStruct((TOP_K, N), F32),
                   jax.ShapeDtypeStruct((TOP_K, N), jnp.int32), jax.ShapeDtypeStruct((E, LANES), jnp.int32)],
        scratch_shapes=[pltpu.VMEM((E, 1), F32)],
        compiler_params=_cparams("arbitrary"),
        name="outproj_router",
    )(x2, oa, ob, out_norm_a.reshape(1, -1), out_norm_b.reshape(1, -1), w_out.astype(BF16),
      ffn_norm.reshape(1, D), wr_hi, wr_lo, b_router.reshape(E, 1), tri)


EXPERT_ROWS = 256
DISPATCH_TM = 256


def _routing_plan(idx_t, rank_t, counts, n_blocks):
    pad_counts = (counts + EXPERT_ROWS - 1) // EXPERT_ROWS * EXPERT_ROWS
    pad_ends = jnp.cumsum(pad_counts)
    pad_starts = pad_ends - pad_counts
    eids = jnp.arange(N_EXPERTS, dtype=idx_t.dtype)
    dest = rank_t + jnp.sum(jnp.where(idx_t[..., None] == eids, pad_starts, 0), axis=-1)
    blk_start = jnp.arange(n_blocks, dtype=jnp.int32) * EXPERT_ROWS
    block_e = jnp.minimum(jnp.sum((pad_ends[None, :] <= blk_start[:, None]).astype(jnp.int32), axis=1),
                          N_EXPERTS - 1)
    n_used = (pad_ends[-1] // EXPERT_ROWS).astype(jnp.int32).reshape(1)
    nonempty = pad_counts > 0
    w_slot = ((jnp.cumsum(nonempty) - nonempty) % 2).astype(jnp.int32)
    later = (eids[None, :] > eids[:, None]) & nonempty[None, :]
    w_next = jnp.min(jnp.where(later, eids[None, :], N_EXPERTS), axis=1)
    w_next = jnp.where(w_next == N_EXPERTS, -1, w_next).astype(jnp.int32)
    return (dest.astype(jnp.int32), block_e, n_used, pad_starts.astype(jnp.int32), pad_counts.astype(jnp.int32),
            w_slot, w_next)


def _dispatch_kernel(dest_ref, cnt_ref, pstart_ref, pcnt_ref, nu_ref, hf_ref, xs_ref, zero_scr, sem, zsem, *,
                     n_tok):
    i = pl.program_id(0)
    tm = hf_ref.shape[0] // SUBLANES

    def tile(ref, row):
        return ref.at[pl.ds(pl.multiple_of(row * SUBLANES, SUBLANES), SUBLANES)]

    def tok(j, _):
        t = i * tm + j
        for k in range(TOP_K):
            pltpu.make_async_copy(tile(hf_ref, j), tile(xs_ref, dest_ref[k * n_tok + t]), sem).start(
                priority=k % 2)
        return 0
    lax.fori_loop(0, tm, tok, 0, unroll=2)

    @pl.when(i == 0)
    def _():
        zero_scr[...] = jnp.zeros_like(zero_scr)
        blk = EXPERT_ROWS * SUBLANES

        def pad_copy(row):
            return pltpu.make_async_copy(zero_scr.at[pl.ds(0, SUBLANES)], tile(xs_ref, row), zsem)

        def tail_copy(b):
            return pltpu.make_async_copy(zero_scr, xs_ref.at[pl.ds(pl.multiple_of(b * blk, blk), blk)], zsem)

        n_blocks = xs_ref.shape[0] // blk
        lax.fori_loop(nu_ref[0], n_blocks, lambda b, _: (tail_copy(b).start(), 0)[1], 0)
        lax.fori_loop(nu_ref[0], n_blocks, lambda b, _: (tail_copy(b).wait(), 0)[1], 0)

        def expert(e, _):
            base = pstart_ref[e]

            def pad_row(j, _):
                pad_copy(base + j).start()
                return 0
            lax.fori_loop(cnt_ref[e], pcnt_ref[e], pad_row, 0)

            def pad_wait(j, _):
                pad_copy(base + j).wait()
                return 0
            lax.fori_loop(cnt_ref[e], pcnt_ref[e], pad_wait, 0)
            return 0
        lax.fori_loop(0, N_EXPERTS, expert, 0)

    for k in range(TOP_K):
        pltpu.make_async_copy(hf_ref, xs_ref.at[pl.ds(0, tm * SUBLANES)], sem).wait()


def _dispatch(hf, dest, counts, pad_starts, pad_counts, n_used, rows_max):
    N = hf.shape[0] // SUBLANES
    tm = DISPATCH_TM
    assert N % tm == 0
    return pl.pallas_call(
        functools.partial(_dispatch_kernel, n_tok=N),
        grid_spec=pltpu.PrefetchScalarGridSpec(
            num_scalar_prefetch=5, grid=(N // tm,),
            in_specs=[pl.BlockSpec((tm * SUBLANES, LANES), lambda i, *_: (i, 0))],
            out_specs=pl.BlockSpec(memory_space=pl.ANY),
            scratch_shapes=[pltpu.VMEM((EXPERT_ROWS * SUBLANES, LANES), F32), pltpu.SemaphoreType.DMA(()),
                            pltpu.SemaphoreType.DMA(())]),
        out_shape=jax.ShapeDtypeStruct((rows_max * SUBLANES, LANES), F32),
        compiler_params=_cparams("arbitrary"),
        name="expert_dispatch",
    )(dest.reshape(-1), counts, pad_starts, pad_counts, n_used, hf)


def _expert_kernel(be_ref, nu_ref, pstart_ref, wslot_ref, wnext_ref, xs_ref, wg_hbm, bg_ref, wu_hbm, bu_ref,
                   wd_hbm, bd_ref, ys_ref, wg_buf, wu_buf, wd_buf, wg_bf, wu_bf, wd_bf, sem):
    i = pl.program_id(0)
    e = be_ref[i]

    def weight_copies(expert, slot):
        return [pltpu.make_async_copy(src.at[expert], dst.at[slot], sem.at[slot, n])
                for n, (src, dst) in enumerate(((wg_hbm, wg_buf), (wu_hbm, wu_buf), (wd_hbm, wd_buf)))]

    @pl.when(i < nu_ref[0])
    def _():
        @pl.when(pstart_ref[e] == i * EXPERT_ROWS)
        def _():
            slot = wslot_ref[e]

            @pl.when(i == 0)
            def _():
                for c in weight_copies(e, slot):
                    c.start()
            for c in weight_copies(e, slot):
                c.wait()

            @pl.when(wnext_ref[e] >= 0)
            def _():
                for c in weight_copies(wnext_ref[e], 1 - slot):
                    c.start()
            wg_bf[...] = wg_buf[slot].astype(BF16)
            wu_bf[...] = wu_buf[slot].astype(BF16)
            wd_bf[...] = wd_buf[slot].astype(BF16)

        xb = _load_token_tiles(xs_ref, EXPERT_ROWS).astype(BF16)
        g = jnp.dot(xb, wg_bf[...], preferred_element_type=F32) + bg_ref[0]
        u = jnp.dot(xb, wu_bf[...], preferred_element_type=F32) + bu_ref[0]
        g = jnp.minimum(g, SWIGLU_LIMIT)
        u = jnp.clip(u, -SWIGLU_LIMIT, SWIGLU_LIMIT)
        act = (u + 1.0) * (g * jax.nn.sigmoid(SWIGLU_ALPHA * g))
        _store_token_tiles(ys_ref, jnp.dot(act.astype(BF16), wd_bf[...], preferred_element_type=F32)
                           + bd_ref[0])

    @pl.when(i >= nu_ref[0])
    def _():
        ys_ref[...] = jnp.zeros_like(ys_ref)


def _experts(xs, block_e, n_used, pad_starts, w_slot, w_next, w_gate, b_gate, w_up, b_up, w_down, b_down):
    E, D, F = w_gate.shape
    blk = EXPERT_ROWS * SUBLANES
    n_blocks = xs.shape[0] // blk
    xmap = lambda i, be, nu, *_: (jnp.minimum(i, nu[0] - 1), 0)
    bmap = lambda i, be, *_: (be[i], 0, 0)
    hbm = pl.BlockSpec(memory_space=pl.ANY)
    return pl.pallas_call(
        _expert_kernel,
        grid_spec=pltpu.PrefetchScalarGridSpec(
            num_scalar_prefetch=5, grid=(n_blocks,),
            in_specs=[pl.BlockSpec((blk, LANES), xmap),
                      hbm, pl.BlockSpec((1, 1, F), bmap),
                      hbm, pl.BlockSpec((1, 1, F), bmap),
                      hbm, pl.BlockSpec((1, 1, D), bmap)],
            out_specs=pl.BlockSpec((blk, LANES), lambda i, *_: (i, 0)),
            scratch_shapes=[pltpu.VMEM((2, D, F), F32), pltpu.VMEM((2, D, F), F32), pltpu.VMEM((2, F, D), F32),
                            pltpu.VMEM((D, F), BF16), pltpu.VMEM((D, F), BF16), pltpu.VMEM((F, D), BF16),
                            pltpu.SemaphoreType.DMA((2, 3))]),
        out_shape=jax.ShapeDtypeStruct(xs.shape, F32),
        compiler_params=_cparams("arbitrary"),
        name="expert_ffn",
    )(block_e, n_used, pad_starts, w_slot, w_next, xs, w_gate, b_gate.reshape(E, 1, F), w_up,
      b_up.reshape(E, 1, F), w_down, b_down.reshape(E, 1, D))


COMBINE_TM = 256


def _combine_kernel(dest_ref, x1_ref, gate_ref, fn_ref, ys_ref, o_ref, buf, sem, *, n_tok):
    i = pl.program_id(0)
    tm = COMBINE_TM
    slot = i % 2

    def tile(row):
        return pl.ds(pl.multiple_of(row * SUBLANES, SUBLANES), SUBLANES)

    def gather(step, to_slot):
        def issue(j, _):
            t = step * tm + j
            for k in range(TOP_K):
                pltpu.make_async_copy(ys_ref.at[tile(dest_ref[k * n_tok + t])],
                                      buf.at[to_slot * TOP_K + k, tile(j)], sem.at[to_slot]).start(priority=k % 2)
            return 0
        lax.fori_loop(0, tm, issue, 0, unroll=2)

    @pl.when(i == 0)
    def _():
        gather(0, 0)

    @pl.when(i + 1 < pl.num_programs(0))
    def _():
        gather(i + 1, 1 - slot)

    for k in range(TOP_K):
        pltpu.make_async_copy(ys_ref.at[pl.ds(0, tm * SUBLANES)], buf.at[slot * TOP_K + k], sem.at[slot]).wait()

    gates = gate_ref[...]
    y = x1_ref[...]
    for k in range(TOP_K):
        y = y + _load_token_tiles(buf, tm, lead=slot * TOP_K + k) * gates[:, k:k + 1]
    ms = jnp.mean(y * y, axis=-1, keepdims=True)
    o_ref[...] = y * lax.rsqrt(ms + EPS) * fn_ref[...]


def _combine(ys, dest, x1, gates_t, final_norm):
    N, D = x1.shape
    tm = COMBINE_TM
    assert N % tm == 0
    return pl.pallas_call(
        functools.partial(_combine_kernel, n_tok=N),
        grid_spec=pltpu.PrefetchScalarGridSpec(
            num_scalar_prefetch=1, grid=(N // tm,),
            in_specs=[pl.BlockSpec((tm, D), lambda i, d: (i, 0)),
                      pl.BlockSpec((tm, TOP_K), lambda i, d: (i, 0)),
                      pl.BlockSpec((1, D), lambda i, d: (0, 0)),
                      pl.BlockSpec(memory_space=pl.ANY)],
            out_specs=pl.BlockSpec((tm, D), lambda i, d: (i, 0)),
            scratch_shapes=[pltpu.VMEM((2 * TOP_K, tm * SUBLANES, LANES), F32),
                            pltpu.SemaphoreType.DMA((2,))]),
        out_shape=jax.ShapeDtypeStruct((N, D), F32),
        compiler_params=_cparams("arbitrary"),
        name="expert_combine",
    )(dest.reshape(-1), x1, gates_t.T, final_norm.reshape(1, D), ys)


def kernel(x, attn_norm, w_in, q_norm, k_norm, out_norm_a, out_norm_b, w_out, ffn_norm, w_router,
           b_router, w_gate, b_gate, w_up, b_up, w_down, b_down, final_norm):
    B, S, D = x.shape
    x2 = x.reshape(B * S, D)
    qa, ka2, va2, qb, kb, vb = _input_projection(x2, attn_norm[0], w_in[0], q_norm[0], k_norm[0], S)
    oa = _grid_attention(qa, ka2, va2, B, S)
    ob = _dilated_attention(qb, kb, vb, B, S)
    x1, hf, idx_t, gates_t, rank_t, cnt = _outproj_router(
        x2, oa, ob, out_norm_a[0], out_norm_b[0], w_out[0], ffn_norm[0], w_router[0], b_router[0])
    N = B * S
    n_blocks = (N * TOP_K) // EXPERT_ROWS + N_EXPERTS
    counts = cnt[:, 0]
    dest, block_e, n_used, pad_starts, pad_counts, w_slot, w_next = _routing_plan(
        idx_t, rank_t, counts, n_blocks)
    xs = _dispatch(hf, dest, counts, pad_starts, pad_counts, n_used, n_blocks * EXPERT_ROWS)
    ys = _experts(xs, block_e, n_used, pad_starts, w_slot, w_next,
                  w_gate[0], b_gate[0], w_up[0], b_up[0], w_down[0], b_down[0])
    out = _combine(ys, dest, x1, gates_t, final_norm)
    return out.reshape(B, S, D)
```

```python
import functools

import jax
import jax.numpy as jnp
import numpy as np
from jax import lax
from jax.experimental import pallas as pl
from jax.experimental.pallas import tpu as pltpu

F32 = jnp.float32
BF16 = jnp.bfloat16

HEAD_DIM = 64
N_HEADS_A = 8
N_KV_HEADS_A = 2
N_HEADS_B = 8
DILATED_BRANCHES = ((128, 1), (512, 4), (2048, 16))
GRID_W = 64
ROPE_THETA = 10000.0
N_EXPERTS = 32
TOP_K = 4
SWIGLU_LIMIT = 7.0
SWIGLU_ALPHA = 1.702
EPS = 1e-6

LANES = 128
QA_W = N_HEADS_A * HEAD_DIM
KA_W = N_KV_HEADS_A * HEAD_DIM
QB_W = N_HEADS_B * HEAD_DIM
VMEM_LIMIT = 56 * 1024 * 1024


def _cparams(*sem):
    return pltpu.CompilerParams(dimension_semantics=sem, vmem_limit_bytes=VMEM_LIMIT)


def _rope_tables(S, tm):
    assert tm % GRID_W == 0 and S % tm == 0
    lane = jnp.arange(LANES)
    i = lane % HEAD_DIM
    half = HEAD_DIM // 2
    t0 = (jnp.arange(S // tm) * tm).astype(F32)[:, None]
    j = jnp.arange(tm).astype(F32)[:, None]
    inv_a = (ROPE_THETA ** (-jnp.arange(0, half, 2, dtype=F32) / half))[i % (half // 2)][None, :]
    is_row = (i < half)[None, :]
    base_a = jnp.where(is_row, jnp.floor(t0 / GRID_W) * inv_a, 0.0)
    jrow = jnp.floor(j / GRID_W)
    offs_a = jnp.where(is_row, jrow, j - jrow * GRID_W) * inv_a
    sgn_a = jnp.where((i // (half // 2)) % 2 == 0, -1.0, 1.0)[None, :].astype(F32)
    inv_b = (ROPE_THETA ** (-jnp.arange(0, HEAD_DIM, 2, dtype=F32) / HEAD_DIM))[i % half][None, :]
    base_b = t0 * inv_b
    offs_b = j * inv_b
    sgn_b = jnp.where(i < half, -1.0, 1.0)[None, :].astype(F32)
    cs = lambda a: jnp.concatenate([jnp.cos(a), jnp.sin(a)], axis=-1)
    return (cs(base_a)[:, None, :], cs(offs_a), sgn_a), (cs(base_b)[:, None, :], cs(offs_b), sgn_b)


def _rope_cos_sin(base_ref, offs_ref, sgn_ref):
    cb, sb = base_ref[0, :, 0:LANES], base_ref[0, :, LANES:2 * LANES]
    co, so = offs_ref[:, 0:LANES], offs_ref[:, LANES:2 * LANES]
    return cb * co - sb * so, (sb * co + cb * so) * sgn_ref[...]


def _rope_tile(x, cos, sin_signed, half):
    lane = lax.broadcasted_iota(jnp.int32, x.shape, 1)
    first = (lane % (2 * half)) < half
    partner = jnp.where(first, pltpu.roll(x, LANES - half, 1), pltpu.roll(x, half, 1))
    return x * cos + partner * sin_signed


def _head_rms(x, seg_mean, gain):
    ms = jnp.dot((x * x).astype(BF16), seg_mean, preferred_element_type=F32)
    return x * lax.rsqrt(ms + EPS) * gain


def _inproj_kernel(x_ref, g_ref, w_ref, qn_ref, kn_ref, seg_ref, base_a_ref, offs_a_ref, sgn_a_ref,
                   base_b_ref, offs_b_ref, sgn_b_ref, qa_ref, ka_ref, va_ref, qb_ref, kb_ref, vb_ref):
    x = x_ref[...]
    ms = jnp.mean(x * x, axis=-1, keepdims=True)
    hn = (x * lax.rsqrt(ms + EPS) * g_ref[...]).astype(BF16)
    proj = jnp.dot(hn, w_ref[...], preferred_element_type=F32)
    seg = seg_ref[...]
    cosa, sina = _rope_cos_sin(base_a_ref, offs_a_ref, sgn_a_ref)
    cosb, sinb = _rope_cos_sin(base_b_ref, offs_b_ref, sgn_b_ref)
    scale = HEAD_DIM ** -0.5
    lane = lax.broadcasted_iota(jnp.int32, (x.shape[0], LANES), 1)
    lo = lane < HEAD_DIM

    off = 0
    for j in range(QA_W // LANES):
        t = proj[:, off + j * LANES: off + (j + 1) * LANES]
        t = _rope_tile(_head_rms(t, seg, qn_ref[...]), cosa, sina, HEAD_DIM // 4)
        qa_ref[:, j * LANES:(j + 1) * LANES] = (t * (scale * LOG2E)).astype(qa_ref.dtype)
    off += QA_W
    k = _rope_tile(_head_rms(proj[:, off:off + LANES], seg, kn_ref[...]), cosa, sina, HEAD_DIM // 4)
    off += KA_W
    v = proj[:, off:off + LANES]
    off += KA_W
    sw = pltpu.roll(k, HEAD_DIM, 1)
    ka_ref[:, 0:LANES] = jnp.where(lo, k, sw).astype(ka_ref.dtype)
    ka_ref[:, LANES:2 * LANES] = jnp.where(lo, sw, k).astype(ka_ref.dtype)
    vt = v.T
    extra = (lax.broadcasted_iota(jnp.int32, (VT_ROWS - HEAD_DIM, v.shape[0]), 0) == 0).astype(F32)
    va_ref[0] = jnp.concatenate([vt[0:HEAD_DIM], extra], axis=0).astype(va_ref.dtype)
    va_ref[1] = jnp.concatenate([vt[HEAD_DIM:2 * HEAD_DIM], extra], axis=0).astype(va_ref.dtype)
    for j in range(QB_W // LANES):
        t = proj[:, off + j * LANES: off + (j + 1) * LANES]
        qb_ref[:, j * LANES:(j + 1) * LANES] = (
            _rope_tile(t, cosb, sinb, HEAD_DIM // 2) * (scale * LOG2E)).astype(qb_ref.dtype)
    off += QB_W
    for j in range(QB_W // LANES):
        t = proj[:, off + j * LANES: off + (j + 1) * LANES]
        kb_ref[:, j * LANES:(j + 1) * LANES] = _rope_tile(t, cosb, sinb, HEAD_DIM // 2).astype(kb_ref.dtype)
    off += QB_W
    vb_ref[...] = proj[:, off:off + QB_W].astype(vb_ref.dtype)


def _input_projection(x2, attn_norm, w_in, q_norm, k_norm, S, tm=512):
    N, D = x2.shape
    tm = min(tm, S)
    assert S % tm == 0 and N % S == 0
    n_s = S // tm
    rope_a, rope_b = _rope_tables(S, tm)
    seg = jnp.kron(jnp.eye(2, dtype=F32), jnp.full((HEAD_DIM, HEAD_DIM), 1.0 / HEAD_DIM, F32)).astype(BF16)
    two = lambda g: jnp.tile(g.reshape(1, HEAD_DIM), (1, 2))
    row = lambda i: (i, 0)
    const = lambda i: (0, 0)
    rope_specs = [pl.BlockSpec((1, 1, 2 * LANES), lambda i: (i % n_s, 0, 0)),
                  pl.BlockSpec((tm, 2 * LANES), const), pl.BlockSpec((1, LANES), const)]
    w = w_in.astype(BF16)
    out_w = (QA_W, 2 * LANES, None, QB_W, QB_W, QB_W)
    out_dt = (BF16, BF16, BF16, F32, F32, F32)
    vt_spec = pl.BlockSpec((N_KV_HEADS_A, VT_ROWS, tm), lambda i: (0, 0, i))
    vt_shape = jax.ShapeDtypeStruct((N_KV_HEADS_A, VT_ROWS, N), BF16)
    return pl.pallas_call(
        _inproj_kernel,
        grid=(N // tm,),
        in_specs=[pl.BlockSpec((tm, D), row), pl.BlockSpec((1, D), const),
                  pl.BlockSpec(w.shape, const), pl.BlockSpec((1, LANES), const),
                  pl.BlockSpec((1, LANES), const), pl.BlockSpec((LANES, LANES), const)]
                 + rope_specs + rope_specs,
        out_specs=[vt_spec if wd is None else pl.BlockSpec((tm, wd), row) for wd in out_w],
        out_shape=[vt_shape if wd is None else jax.ShapeDtypeStruct((N, wd), dt)
                   for wd, dt in zip(out_w, out_dt)],
        compiler_params=_cparams("parallel"),
        name="input_projection",
    )(x2, attn_norm.reshape(1, D), w, two(q_norm), two(k_norm), seg, *rope_a, *rope_b)


VT_ROWS = 80
LOG2E = 1.4426950408889634
ATTN_UNROLL = 16


def _attn_a_kernel(q_ref, k_ref, vt_ref, o_ref, st_scr, pt_scr, *, tk):
    tq = q_ref.shape[0]
    S = k_ref.shape[0]
    sub = lax.broadcasted_iota(jnp.int32, (LANES, tq), 0)
    lo = sub < HEAD_DIM
    q0 = q_ref[:, 0:LANES].astype(F32).T
    q1 = q_ref[:, LANES:2 * LANES].astype(F32).T
    zero = jnp.zeros_like(q0)
    qst = jnp.concatenate([jnp.where(lo, q0, zero), jnp.where(lo, zero, q0),
                           jnp.where(lo, q1, zero), jnp.where(lo, zero, q1)], axis=1).astype(BF16)
    cols = 4 * tq
    n_chunks = S // tk
    assert n_chunks % 2 == 0

    def scores(j, slot):
        start = pl.multiple_of(j * tk, tk)
        st_scr[slot] = jnp.dot(k_ref[pl.ds(start, tk), :], qst, preferred_element_type=F32)

    def softmax_pv(j, slot, m, acc):
        mx = st_scr[slot, 0:8, :]
        for r in range(1, tk // 8):
            mx = jnp.maximum(mx, st_scr[slot, 8 * r:8 * r + 8, :])
        m_new = jnp.maximum(m, jnp.max(mx, axis=0, keepdims=True))
        alpha = jnp.exp2(m - m_new)
        mb = jnp.broadcast_to(m_new, (16, cols))
        for r in range(tk // 16):
            blk = st_scr[slot, 16 * r:16 * r + 16, :]
            pt_scr[slot, 16 * r:16 * r + 16, :] = jnp.exp2((blk - mb).astype(BF16))
        start = pl.multiple_of(j * tk, tk)
        vt = vt_ref[0, :, pl.ds(start, tk)]
        acc = alpha * acc + jnp.dot(vt, pt_scr[slot], preferred_element_type=F32)
        return m_new, acc

    def body(jj, carry):
        m, acc = carry
        j0 = ATTN_UNROLL * jj
        for u in range(ATTN_UNROLL):
            scores(jnp.minimum(j0 + u + 1, n_chunks - 1), (u + 1) % 2)
            m, acc = softmax_pv(j0 + u, u % 2, m, acc)
        return m, acc

    assert n_chunks % ATTN_UNROLL == 0 and ATTN_UNROLL % 2 == 0
    scores(0, 0)
    init = (jnp.full((1, cols), -jnp.inf, F32), jnp.zeros((VT_ROWS, cols), F32))
    _, acc = lax.fori_loop(0, n_chunks // ATTN_UNROLL, body, init)
    ot = acc[0:HEAD_DIM] / acc[HEAD_DIM:HEAD_DIM + 1]
    for t in range(2):
        pair = jnp.concatenate([ot[:, (2 * t) * tq:(2 * t + 1) * tq],
                                ot[:, (2 * t + 1) * tq:(2 * t + 2) * tq]], axis=0)
        o_ref[:, t * LANES:(t + 1) * LANES] = pair.T.astype(o_ref.dtype)


def _grid_attention(qa, ka2, vat, B, S, tq=256, tk=256):
    N = qa.shape[0]
    tq, tk = min(tq, S), min(tk, S)
    assert S % tq == 0 and S % tk == 0
    nq = S // tq
    return pl.pallas_call(
        functools.partial(_attn_a_kernel, tk=tk),
        grid=(B, N_KV_HEADS_A, nq),
        in_specs=[pl.BlockSpec((tq, 2 * LANES), lambda b, h, i: (b * nq + i, h)),
                  pl.BlockSpec((S, LANES), lambda b, h, i: (b, h)),
                  pl.BlockSpec((1, VT_ROWS, S), lambda b, h, i: (h, 0, b))],
        out_specs=pl.BlockSpec((tq, 2 * LANES), lambda b, h, i: (b * nq + i, h)),
        out_shape=jax.ShapeDtypeStruct((N, QA_W), F32),
        scratch_shapes=[pltpu.VMEM((2, tk, 4 * tq), F32), pltpu.VMEM((2, tk, 4 * tq), BF16)],
        compiler_params=_cparams("parallel", "parallel", "parallel"),
        name="grid_attention",
    )(qa, ka2, vat)


DIL_QB = 128
DIL_R = 64
DIL_KW = DIL_QB + 2 * DIL_R
DIL_SB = 2048
DIL_UNROLL = 8


DIL_WINDOW_OFFS = (-DIL_R, 0, -2 * DIL_R)


def _dilated_kernel(q_ref, k_ref, v_ref, o_ref, acc_scr, m_scr, l_scr, mask_scr, *, seq):
    c = pl.program_id(2)
    lane = lax.broadcasted_iota(jnp.int32, (DIL_QB, LANES), 1)
    lo = lane < HEAD_DIM
    rel = (lax.broadcasted_iota(jnp.int32, (2 * DIL_QB, DIL_KW), 1)
           - lax.broadcasted_iota(jnp.int32, (2 * DIL_QB, DIL_KW), 0) % DIL_QB)
    for n, off in enumerate(DIL_WINDOW_OFFS):
        mask_scr[n] = jnp.where(jnp.abs(rel + off) <= DIL_R, 0.0, -1e30)
    ones = jnp.ones((DIL_KW, LANES), BF16)

    for bi, (window, d) in enumerate(DILATED_BRANCHES):
        assert window // (2 * d) == DIL_R
        nb = DIL_SB // (DIL_QB * d)
        n_m = seq // d

        def block(it, _, d=d, nb=nb, n_m=n_m, first=(bi == 0)):
            r = it // nb
            i = it % nb
            row0 = r + d * DIL_QB * i
            m0 = (c * DIL_SB) // d + DIL_QB * i
            ks = jnp.clip(m0 - DIL_R, 0, n_m - DIL_KW)
            q = q_ref[pl.ds(row0, DIL_QB, stride=d), :]
            k = k_ref[pl.ds(r + d * ks, DIL_KW, stride=d), :].astype(BF16)
            v = v_ref[pl.ds(r + d * ks, DIL_KW, stride=d), :].astype(BF16)
            zero = jnp.zeros_like(q)
            qs = jnp.concatenate([jnp.where(lo, q, zero), jnp.where(lo, zero, q)], axis=0).astype(BF16)
            s = lax.dot_general(qs, k, (((1,), (1,)), ((), ())), preferred_element_type=F32)
            off = ks - m0
            s = s + mask_scr[jnp.where(off == DIL_WINDOW_OFFS[0], 0, jnp.where(off == DIL_WINDOW_OFFS[1], 1, 2))]
            mb = jnp.max(s, axis=-1, keepdims=True)
            p = jnp.exp2((s - mb).astype(BF16))
            pv = jnp.dot(p, jnp.concatenate([v, ones], axis=1), preferred_element_type=F32)
            acc_b = jnp.where(lo, pv[0:DIL_QB, 0:LANES], pv[DIL_QB:, 0:LANES])
            m_b = jnp.where(lo, mb[0:DIL_QB], mb[DIL_QB:])
            l_b = jnp.where(lo, pv[0:DIL_QB, LANES:], pv[DIL_QB:, LANES:])
            rows = pl.ds(row0, DIL_QB, stride=d)
            if first:
                acc_scr[rows, :] = acc_b
                m_scr[rows, :] = m_b
                l_scr[rows, :] = l_b
            else:
                m_old = m_scr[rows, :]
                m_new = jnp.maximum(m_old, m_b)
                a_old = jnp.exp2(m_old - m_new)
                a_new = jnp.exp2(m_b - m_new)
                acc_scr[rows, :] = acc_scr[rows, :] * a_old + acc_b * a_new
                l_scr[rows, :] = l_scr[rows, :] * a_old + l_b * a_new
                m_scr[rows, :] = m_new
            return 0

        lax.fori_loop(0, d * nb, block, 0, unroll=DIL_UNROLL)

    o_ref[...] = (acc_scr[...] / l_scr[...]).astype(o_ref.dtype)


def _dilated_attention(qb, kb, vb, B, S):
    N = qb.shape[0]
    assert S % DIL_SB == 0 and S // DILATED_BRANCHES[-1][1] >= DIL_KW
    nsb = S // DIL_SB
    return pl.pallas_call(
        functools.partial(_dilated_kernel, seq=S),
        grid=(B, QB_W // LANES, nsb),
        in_specs=[pl.BlockSpec((DIL_SB, LANES), lambda b, h, i: (b * nsb + i, h)),
                  pl.BlockSpec((S, LANES), lambda b, h, i: (b, h)),
                  pl.BlockSpec((S, LANES), lambda b, h, i: (b, h))],
        out_specs=pl.BlockSpec((DIL_SB, LANES), lambda b, h, i: (b * nsb + i, h)),
        out_shape=jax.ShapeDtypeStruct((N, QB_W), F32),
        scratch_shapes=[pltpu.VMEM((DIL_SB, LANES), F32)] * 3
                       + [pltpu.VMEM((len(DIL_WINDOW_OFFS), 2 * DIL_QB, DIL_KW), F32)],
        compiler_params=_cparams("parallel", "parallel", "parallel"),
        name="dilated_attention",
    )(qb, kb, vb)


SUBLANES = 8


def _store_token_tiles(ref, val):
    rows, d = val.shape
    assert d == SUBLANES * LANES
    for j in range(SUBLANES):
        ref[pl.ds(j, rows, stride=SUBLANES), :] = val[:, j * LANES:(j + 1) * LANES]


def _load_token_tiles(ref, rows, lead=None):
    idx = (lambda j: (pl.ds(j, rows, stride=SUBLANES), slice(None))) if lead is None else (
        lambda j: (lead, pl.ds(j, rows, stride=SUBLANES), slice(None)))
    return jnp.concatenate([ref[idx(j)] for j in range(SUBLANES)], axis=-1)


def _split_bf16(a):
    hi = a.astype(BF16)
    return hi, (a - hi.astype(F32)).astype(BF16)


def _outproj_router_kernel(x_ref, oa_ref, ob_ref, ga_ref, gb_ref, wo_ref, gf_ref, wr_hi_ref, wr_lo_ref,
                           br_ref, tri_ref, x1_ref, hf_ref, idx_ref, gate_ref, rank_ref, cnt_ref,
                           carry_scr):
    @pl.when(pl.program_id(0) == 0)
    def _():
        carry_scr[...] = jnp.zeros_like(carry_scr)

    def rms(t, g):
        return t * lax.rsqrt(jnp.mean(t * t, axis=-1, keepdims=True) + EPS) * g

    mix = jnp.concatenate([rms(oa_ref[...], ga_ref[...]), rms(ob_ref[...], gb_ref[...])], axis=-1)
    x1 = x_ref[...] + jnp.dot(mix.astype(BF16), wo_ref[...], preferred_element_type=F32)
    x1_ref[...] = x1
    hf = rms(x1, gf_ref[...])
    _store_token_tiles(hf_ref, hf)

    h_hi, h_lo = _split_bf16(hf)
    nt = (((1,), (1,)), ((), ()))
    logits = (lax.dot_general(wr_hi_ref[...], h_hi, nt, preferred_element_type=F32)
              + lax.dot_general(wr_hi_ref[...], h_lo, nt, preferred_element_type=F32)
              + lax.dot_general(wr_lo_ref[...], h_hi, nt, preferred_element_type=F32)) + br_ref[...]
    E, tm = logits.shape
    eidx = lax.broadcasted_iota(jnp.int32, (E, tm), 0)
    work = logits
    vals, idxs, sel = [], [], jnp.zeros((E, tm), F32)
    for _ in range(TOP_K):
        mx = jnp.max(work, axis=0, keepdims=True)
        first = jnp.min(jnp.where(work == mx, eidx, E), axis=0, keepdims=True)
        hit = eidx == first
        vals.append(mx)
        idxs.append(first)
        sel = jnp.where(hit, 1.0, sel)
        work = jnp.where(hit, -jnp.inf, work)
    ex = [jnp.exp(v - vals[0]) for v in vals]
    den = ex[0] + ex[1] + ex[2] + ex[3]
    gate_ref[...] = jnp.concatenate(ex, axis=0) / den
    idx_ref[...] = jnp.concatenate(idxs, axis=0)

    before = jnp.dot(sel.astype(BF16), tri_ref[...], preferred_element_type=F32) + carry_scr[...]
    ranks = [jnp.sum(jnp.where(eidx == i, before, 0.0), axis=0, keepdims=True) for i in idxs]
    rank_ref[...] = jnp.concatenate(ranks, axis=0).astype(jnp.int32)
    carry_scr[...] = carry_scr[...] + jnp.sum(sel, axis=1, keepdims=True)
    cnt_ref[...] = jnp.broadcast_to(carry_scr[...], cnt_ref.shape).astype(jnp.int32)


def _outproj_router(x2, oa, ob, out_norm_a, out_norm_b, w_out, ffn_norm, w_router, b_router, tm=512):
    N, D = x2.shape
    tm = min(tm, N)
    assert N % tm == 0
    E = w_router.shape[1]
    wr_hi, wr_lo = _split_bf16(w_router.T)
    tri = (jnp.arange(tm)[:, None] < jnp.arange(tm)[None, :]).astype(BF16)
    row = lambda i: (i, 0)
    col = lambda i: (0, i)
    const = lambda i: (0, 0)
    return pl.pallas_call(
        _outproj_router_kernel,
        grid=(N // tm,),
        in_specs=[pl.BlockSpec((tm, D), row), pl.BlockSpec((tm, QA_W), row), pl.BlockSpec((tm, QB_W), row),
                  pl.BlockSpec((1, QA_W), const), pl.BlockSpec((1, QB_W), const),
                  pl.BlockSpec((QA_W + QB_W, D), const), pl.BlockSpec((1, D), const),
                  pl.BlockSpec((E, D), const), pl.BlockSpec((E, D), const), pl.BlockSpec((E, 1), const),
                  pl.BlockSpec((tm, tm), const)],
        out_specs=[pl.BlockSpec((tm, D), row), pl.BlockSpec((tm * SUBLANES, LANES), row),
                   pl.BlockSpec((TOP_K, tm), col), pl.BlockSpec((TOP_K, tm), col),
                   pl.BlockSpec((TOP_K, tm), col), pl.BlockSpec((E, LANES), const)],
        out_shape=[jax.ShapeDtypeStruct((N, D), F32), jax.ShapeDtypeStruct((N * SUBLANES, LANES), F32),
                   jax.ShapeDtypeStruct((TOP_K, N), jnp.int32), jax.ShapeDtypeStruct((TOP_K, N), F32),
                   jax.ShapeDtypeStruct((TOP_K, N), jnp.int32), jax.ShapeDtypeStruct((E, LANES), jnp.int32)],
        scratch_shapes=[pltpu.VMEM((E, 1), F32)],
        compiler_params=_cparams("arbitrary"),
        name="outproj_router",
    )(x2, oa, ob, out_norm_a.reshape(1, -1), out_norm_b.reshape(1, -1), w_out.astype(BF16),
      ffn_norm.reshape(1, D), wr_hi, wr_lo, b_router.reshape(E, 1), tri)


EXPERT_ROWS = 256
DISPATCH_TM = 256


def _routing_plan(idx_t, rank_t, counts, n_blocks):
    pad_counts = (counts + EXPERT_ROWS - 1) // EXPERT_ROWS * EXPERT_ROWS
    pad_ends = jnp.cumsum(pad_counts)
    pad_starts = pad_ends - pad_counts
    eids = jnp.arange(N_EXPERTS, dtype=idx_t.dtype)
    dest = rank_t + jnp.sum(jnp.where(idx_t[..., None] == eids, pad_starts, 0), axis=-1)
    blk_start = jnp.arange(n_blocks, dtype=jnp.int32) * EXPERT_ROWS
    block_e = jnp.minimum(jnp.sum((pad_ends[None, :] <= blk_start[:, None]).astype(jnp.int32), axis=1),
                          N_EXPERTS - 1)
    n_used = (pad_ends[-1] // EXPERT_ROWS).astype(jnp.int32).reshape(1)
    nonempty = pad_counts > 0
    w_slot = ((jnp.cumsum(nonempty) - nonempty) % 2).astype(jnp.int32)
    later = (eids[None, :] > eids[:, None]) & nonempty[None, :]
    w_next = jnp.min(jnp.where(later, eids[None, :], N_EXPERTS), axis=1)
    w_next = jnp.where(w_next == N_EXPERTS, -1, w_next).astype(jnp.int32)
    return (dest.astype(jnp.int32), block_e, n_used, pad_starts.astype(jnp.int32), pad_counts.astype(jnp.int32),
            w_slot, w_next)


def _dispatch_kernel(dest_ref, cnt_ref, cend_ref, pstart_ref, pcnt_ref, nu_ref, hf_ref, xs_ref, inv_ref, zero_scr,
                     sem, zsem, *, n_tok):
    i = pl.program_id(0)
    tm = hf_ref.shape[0] // SUBLANES
    n_pairs = n_tok * TOP_K

    def tile(ref, row):
        return ref.at[pl.ds(pl.multiple_of(row * SUBLANES, SUBLANES), SUBLANES)]

    def tok(j, _):
        t = i * tm + j
        for k in range(TOP_K):
            row = dest_ref[k * n_tok + t]
            inv_ref[EXPERT_ROWS + row] = k * n_tok + t
            pltpu.make_async_copy(tile(hf_ref, j), tile(xs_ref, row), sem).start(priority=k % 2)
        return 0
    lax.fori_loop(0, tm, tok, 0, unroll=2)

    @pl.when(i == 0)
    def _():
        zero_scr[...] = jnp.zeros_like(zero_scr)
        blk = EXPERT_ROWS * SUBLANES

        def pad_copy(row):
            return pltpu.make_async_copy(zero_scr.at[pl.ds(0, SUBLANES)], tile(xs_ref, row), zsem)

        def tail_copy(b):
            return pltpu.make_async_copy(zero_scr, xs_ref.at[pl.ds(pl.multiple_of(b * blk, blk), blk)], zsem)

        n_blocks = xs_ref.shape[0] // blk
        lax.fori_loop(nu_ref[0], n_blocks, lambda b, _: (tail_copy(b).start(), 0)[1], 0)
        lax.fori_loop(nu_ref[0], n_blocks, lambda b, _: (tail_copy(b).wait(), 0)[1], 0)

        def tail_inv(r, _):
            inv_ref[EXPERT_ROWS + r] = r
            return 0
        lax.fori_loop(nu_ref[0] * EXPERT_ROWS, n_blocks * EXPERT_ROWS, tail_inv, 0)

        def lead_inv(j, _):
            inv_ref[j] = n_blocks * EXPERT_ROWS + j
            return 0
        lax.fori_loop(0, EXPERT_ROWS, lead_inv, 0)

        def expert(e, _):
            base = pstart_ref[e]

            def pad_row(j, _):
                inv_ref[EXPERT_ROWS + base + j] = n_pairs + base + j - cend_ref[e]
                pad_copy(base + j).start()
                return 0
            lax.fori_loop(cnt_ref[e], pcnt_ref[e], pad_row, 0)

            def pad_wait(j, _):
                pad_copy(base + j).wait()
                return 0
            lax.fori_loop(cnt_ref[e], pcnt_ref[e], pad_wait, 0)
            return 0
        lax.fori_loop(0, N_EXPERTS, expert, 0)

    for k in range(TOP_K):
        pltpu.make_async_copy(hf_ref, xs_ref.at[pl.ds(0, tm * SUBLANES)], sem).wait()


def _dispatch(hf, dest, counts, pad_starts, pad_counts, n_used, rows_max):
    N = hf.shape[0] // SUBLANES
    tm = DISPATCH_TM
    assert N % tm == 0
    return pl.pallas_call(
        functools.partial(_dispatch_kernel, n_tok=N),
        grid_spec=pltpu.PrefetchScalarGridSpec(
            num_scalar_prefetch=6, grid=(N // tm,),
            in_specs=[pl.BlockSpec((tm * SUBLANES, LANES), lambda i, *_: (i, 0))],
            out_specs=[pl.BlockSpec(memory_space=pl.ANY), pl.BlockSpec(memory_space=pltpu.SMEM)],
            scratch_shapes=[pltpu.VMEM((EXPERT_ROWS * SUBLANES, LANES), F32), pltpu.SemaphoreType.DMA(()),
                            pltpu.SemaphoreType.DMA(())]),
        out_shape=[jax.ShapeDtypeStruct((rows_max * SUBLANES, LANES), F32),
                   jax.ShapeDtypeStruct((EXPERT_ROWS + rows_max,), jnp.int32)],
        compiler_params=_cparams("arbitrary"),
        name="expert_dispatch",
    )(dest.reshape(-1), counts, jnp.cumsum(counts).astype(jnp.int32), pad_starts, pad_counts, n_used, hf)


def _expert_kernel(be_ref, nu_ref, pstart_ref, wslot_ref, wnext_ref, inv_ref, xs_ref, wg_hbm, bg_ref, wu_hbm,
                   bu_ref, wd_hbm, bd_ref, out_hbm, wg_buf, wu_buf, wd_buf, wg_bf, wu_bf, wd_bf, ybuf, sem, ysem):
    i = pl.program_id(0)
    n_used = nu_ref[0]
    n_blocks = pl.num_programs(0)
    e = be_ref[i]
    blk = EXPERT_ROWS * SUBLANES

    def weight_copies(expert, slot):
        return [pltpu.make_async_copy(src.at[expert], dst.at[slot], sem.at[slot, n])
                for n, (src, dst) in enumerate(((wg_hbm, wg_buf), (wu_hbm, wu_buf), (wd_hbm, wd_buf)))]

    def out_tile(t):
        return out_hbm.at[pl.ds(pl.multiple_of(t * SUBLANES, SUBLANES), SUBLANES)]

    def scatter_rows(block, slot):
        base = (block + 1) * EXPERT_ROWS
        for j in range(EXPERT_ROWS):
            pltpu.make_async_copy(ybuf.at[slot, pl.ds(j * SUBLANES, SUBLANES)], out_tile(inv_ref[base + j]),
                                  ysem.at[slot]).start(priority=j % 2)

    def scatter_wait(slot):
        pltpu.make_async_copy(ybuf.at[slot], out_hbm.at[pl.ds(0, blk)], ysem.at[slot]).wait()

    @pl.when(i == 0)
    def _():
        ybuf[1] = jnp.zeros((blk, LANES), F32)

    @pl.when(i < n_used)
    def _():
        @pl.when(pstart_ref[e] == i * EXPERT_ROWS)
        def _():
            slot = wslot_ref[e]

            @pl.when(i == 0)
            def _():
                for c in weight_copies(e, slot):
                    c.start()
            for c in weight_copies(e, slot):
                c.wait()

            @pl.when(wnext_ref[e] >= 0)
            def _():
                for c in weight_copies(wnext_ref[e], 1 - slot):
                    c.start()
            wg_bf[...] = wg_buf[slot].astype(BF16)
            wu_bf[...] = wu_buf[slot].astype(BF16)
            wd_bf[...] = wd_buf[slot].astype(BF16)

        cur = i % 2
        scatter_rows(i - 1, 1 - cur)
        xb = _load_token_tiles(xs_ref, EXPERT_ROWS).astype(BF16)
        g = jnp.dot(xb, wg_bf[...], preferred_element_type=F32) + bg_ref[0]
        u = jnp.dot(xb, wu_bf[...], preferred_element_type=F32) + bu_ref[0]
        g = jnp.minimum(g, SWIGLU_LIMIT)
        u = jnp.clip(u, -SWIGLU_LIMIT, SWIGLU_LIMIT)
        act = (u + 1.0) * (g * jax.nn.sigmoid(SWIGLU_ALPHA * g))
        _store_token_tiles(ybuf.at[cur], jnp.dot(act.astype(BF16), wd_bf[...], preferred_element_type=F32)
                           + bd_ref[0])
        scatter_wait(1 - cur)

    @pl.when(i == n_used)
    def _():
        last = (i - 1) % 2
        scatter_rows(i - 1, last)
        scatter_wait(last)
        ybuf[0] = jnp.zeros((blk, LANES), F32)

    @pl.when(i >= n_used)
    def _():
        tail = pltpu.make_async_copy(ybuf.at[0], out_hbm.at[pl.ds(pl.multiple_of(i * blk, blk), blk)], ysem.at[0])
        tail.start()
        tail.wait()


def _experts(xs, inv, block_e, n_used, pad_starts, w_slot, w_next, w_gate, b_gate, w_up, b_up, w_down, b_down):
    E, D, F = w_gate.shape
    blk = EXPERT_ROWS * SUBLANES
    n_blocks = xs.shape[0] // blk
    xmap = lambda i, be, nu, *_: (jnp.minimum(i, nu[0] - 1), 0)
    bmap = lambda i, be, *_: (be[i], 0, 0)
    hbm = pl.BlockSpec(memory_space=pl.ANY)
    return pl.pallas_call(
        _expert_kernel,
        grid_spec=pltpu.PrefetchScalarGridSpec(
            num_scalar_prefetch=6, grid=(n_blocks,),
            in_specs=[pl.BlockSpec((blk, LANES), xmap),
                      hbm, pl.BlockSpec((1, 1, F), bmap),
                      hbm, pl.BlockSpec((1, 1, F), bmap),
                      hbm, pl.BlockSpec((1, 1, D), bmap)],
            out_specs=hbm,
            scratch_shapes=[pltpu.VMEM((2, D, F), F32), pltpu.VMEM((2, D, F), F32), pltpu.VMEM((2, F, D), F32),
                            pltpu.VMEM((D, F), BF16), pltpu.VMEM((D, F), BF16), pltpu.VMEM((F, D), BF16),
                            pltpu.VMEM((2, blk, LANES), F32),
                            pltpu.SemaphoreType.DMA((2, 3)), pltpu.SemaphoreType.DMA((2,))]),
        out_shape=jax.ShapeDtypeStruct((xs.shape[0] + blk, LANES), F32),
        compiler_params=_cparams("arbitrary"),
        name="expert_ffn",
    )(block_e, n_used, pad_starts, w_slot, w_next, inv, xs, w_gate, b_gate.reshape(E, 1, F), w_up,
      b_up.reshape(E, 1, F), w_down, b_down.reshape(E, 1, D))


COMBINE_TM = 256


def _combine_kernel(x1_ref, gate_ref, fn_ref, *refs):
    ys_refs, o_ref = refs[:TOP_K], refs[TOP_K]
    tm = x1_ref.shape[0]
    gates = gate_ref[...]
    y = x1_ref[...]
    for k in range(TOP_K):
        y = y + _load_token_tiles(ys_refs[k], tm) * gates[:, k:k + 1]
    ms = jnp.mean(y * y, axis=-1, keepdims=True)
    o_ref[...] = y * lax.rsqrt(ms + EPS) * fn_ref[...]


def _combine(ys, x1, gates_t, final_norm):
    N, D = x1.shape
    tm = COMBINE_TM
    assert N % tm == 0
    n_t = N // tm
    slot_spec = lambda k: pl.BlockSpec((tm * SUBLANES, LANES), lambda i: (k * n_t + i, 0))
    return pl.pallas_call(
        _combine_kernel,
        grid=(n_t,),
        in_specs=[pl.BlockSpec((tm, D), lambda i: (i, 0)),
                  pl.BlockSpec((tm, TOP_K), lambda i: (i, 0)),
                  pl.BlockSpec((1, D), lambda i: (0, 0))] + [slot_spec(k) for k in range(TOP_K)],
        out_specs=pl.BlockSpec((tm, D), lambda i: (i, 0)),
        out_shape=jax.ShapeDtypeStruct((N, D), F32),
        compiler_params=_cparams("parallel"),
        name="expert_combine",
    )(x1, gates_t.T, final_norm.reshape(1, D), *([ys] * TOP_K))


def kernel(x, attn_norm, w_in, q_norm, k_norm, out_norm_a, out_norm_b, w_out, ffn_norm, w_router,
           b_router, w_gate, b_gate, w_up, b_up, w_down, b_down, final_norm):
    B, S, D = x.shape
    x2 = x.reshape(B * S, D)
    qa, ka2, va2, qb, kb, vb = _input_projection(x2, attn_norm[0], w_in[0], q_norm[0], k_norm[0], S)
    oa = _grid_attention(qa, ka2, va2, B, S)
    ob = _dilated_attention(qb, kb, vb, B, S)
    x1, hf, idx_t, gates_t, rank_t, cnt = _outproj_router(
        x2, oa, ob, out_norm_a[0], out_norm_b[0], w_out[0], ffn_norm[0], w_router[0], b_router[0])
    N = B * S
    n_blocks = (N * TOP_K) // EXPERT_ROWS + N_EXPERTS
    counts = cnt[:, 0]
    dest, block_e, n_used, pad_starts, pad_counts, w_slot, w_next = _routing_plan(
        idx_t, rank_t, counts, n_blocks)
    xs, inv = _dispatch(hf, dest, counts, pad_starts, pad_counts, n_used, n_blocks * EXPERT_ROWS)
    ys = _experts(xs, inv, block_e, n_used, pad_starts, w_slot, w_next,
                  w_gate[0], b_gate[0], w_up[0], b_up[0], w_down[0], b_down[0])
    out = _combine(ys, x1, gates_t, final_norm)
    return out.reshape(B, S, D)
```

```python
import functools

import jax
import jax.numpy as jnp
from jax import lax
from jax.experimental import pallas as pl
from jax.experimental.pallas import tpu as pltpu

F32 = jnp.float32
BF16 = jnp.bfloat16

HEAD_DIM = 64
N_HEADS_A = 8
N_KV_HEADS_A = 2
N_HEADS_B = 8
DILATED_BRANCHES = ((128, 1), (512, 4), (2048, 16))
GRID_W = 64
ROPE_THETA = 10000.0
N_EXPERTS = 32
TOP_K = 4
SWIGLU_LIMIT = 7.0
SWIGLU_ALPHA = 1.702
EPS = 1e-6

LANES = 128
QA_W = N_HEADS_A * HEAD_DIM
KA_W = N_KV_HEADS_A * HEAD_DIM
QB_W = N_HEADS_B * HEAD_DIM
VMEM_LIMIT = 56 * 1024 * 1024


def _cparams(*sem):
    return pltpu.CompilerParams(dimension_semantics=sem, vmem_limit_bytes=VMEM_LIMIT)


def _rope_tables(S, tm):
    assert tm % GRID_W == 0 and S % tm == 0
    lane = jnp.arange(LANES)
    i = lane % HEAD_DIM
    half = HEAD_DIM // 2
    t0 = (jnp.arange(S // tm) * tm).astype(F32)[:, None]
    j = jnp.arange(tm).astype(F32)[:, None]
    inv_a = (ROPE_THETA ** (-jnp.arange(0, half, 2, dtype=F32) / half))[i % (half // 2)][None, :]
    is_row = (i < half)[None, :]
    base_a = jnp.where(is_row, jnp.floor(t0 / GRID_W) * inv_a, 0.0)
    jrow = jnp.floor(j / GRID_W)
    offs_a = jnp.where(is_row, jrow, j - jrow * GRID_W) * inv_a
    sgn_a = jnp.where((i // (half // 2)) % 2 == 0, -1.0, 1.0)[None, :].astype(F32)
    inv_b = (ROPE_THETA ** (-jnp.arange(0, HEAD_DIM, 2, dtype=F32) / HEAD_DIM))[i % half][None, :]
    base_b = t0 * inv_b
    offs_b = j * inv_b
    sgn_b = jnp.where(i < half, -1.0, 1.0)[None, :].astype(F32)
    cs = lambda a: jnp.concatenate([jnp.cos(a), jnp.sin(a)], axis=-1)
    return (cs(base_a)[:, None, :], cs(offs_a), sgn_a), (cs(base_b)[:, None, :], cs(offs_b), sgn_b)


def _rope_cos_sin(base_ref, offs_ref, sgn_ref):
    cb, sb = base_ref[0, :, 0:LANES], base_ref[0, :, LANES:2 * LANES]
    co, so = offs_ref[:, 0:LANES], offs_ref[:, LANES:2 * LANES]
    return cb * co - sb * so, (sb * co + cb * so) * sgn_ref[...]


def _rope_tile(x, cos, sin_signed, half):
    lane = lax.broadcasted_iota(jnp.int32, x.shape, 1)
    first = (lane % (2 * half)) < half
    partner = jnp.where(first, pltpu.roll(x, LANES - half, 1), pltpu.roll(x, half, 1))
    return x * cos + partner * sin_signed


def _head_rms(x, seg_mean, gain):
    ms = jnp.dot((x * x).astype(BF16), seg_mean, preferred_element_type=F32)
    return x * lax.rsqrt(ms + EPS) * gain


def _inproj_kernel(x_ref, g_ref, w_ref, qn_ref, kn_ref, seg_ref, base_a_ref, offs_a_ref, sgn_a_ref,
                   base_b_ref, offs_b_ref, sgn_b_ref, qa_ref, ka_ref, va_ref, qb_ref, kb_ref, vb_ref):
    x = x_ref[...]
    ms = jnp.mean(x * x, axis=-1, keepdims=True)
    hn = (x * lax.rsqrt(ms + EPS) * g_ref[...]).astype(BF16)
    proj = jnp.dot(hn, w_ref[...], preferred_element_type=F32)
    seg = seg_ref[...]
    cosa, sina = _rope_cos_sin(base_a_ref, offs_a_ref, sgn_a_ref)
    cosb, sinb = _rope_cos_sin(base_b_ref, offs_b_ref, sgn_b_ref)
    scale = HEAD_DIM ** -0.5
    lane = lax.broadcasted_iota(jnp.int32, (x.shape[0], LANES), 1)
    lo = lane < HEAD_DIM

    off = 0
    for j in range(QA_W // LANES):
        t = proj[:, off + j * LANES: off + (j + 1) * LANES]
        t = _rope_tile(_head_rms(t, seg, qn_ref[...]), cosa, sina, HEAD_DIM // 4)
        qa_ref[:, j * LANES:(j + 1) * LANES] = (t * (scale * LOG2E)).astype(qa_ref.dtype)
    off += QA_W
    k = _rope_tile(_head_rms(proj[:, off:off + LANES], seg, kn_ref[...]), cosa, sina, HEAD_DIM // 4)
    off += KA_W
    v = proj[:, off:off + LANES]
    off += KA_W
    sw = pltpu.roll(k, HEAD_DIM, 1)
    ka_ref[:, 0:LANES] = jnp.where(lo, k, sw).astype(ka_ref.dtype)
    ka_ref[:, LANES:2 * LANES] = jnp.where(lo, sw, k).astype(ka_ref.dtype)
    vt = v.T
    extra = (lax.broadcasted_iota(jnp.int32, (VT_ROWS - HEAD_DIM, v.shape[0]), 0) == 0).astype(F32)
    va_ref[0] = jnp.concatenate([vt[0:HEAD_DIM], extra], axis=0).astype(va_ref.dtype)
    va_ref[1] = jnp.concatenate([vt[HEAD_DIM:2 * HEAD_DIM], extra], axis=0).astype(va_ref.dtype)
    for j in range(QB_W // LANES):
        t = proj[:, off + j * LANES: off + (j + 1) * LANES]
        qb_ref[:, j * LANES:(j + 1) * LANES] = (
            _rope_tile(t, cosb, sinb, HEAD_DIM // 2) * (scale * LOG2E)).astype(qb_ref.dtype)
    off += QB_W
    for j in range(QB_W // LANES):
        t = proj[:, off + j * LANES: off + (j + 1) * LANES]
        kb_ref[:, j * LANES:(j + 1) * LANES] = _rope_tile(t, cosb, sinb, HEAD_DIM // 2).astype(kb_ref.dtype)
    off += QB_W
    vb_ref[...] = proj[:, off:off + QB_W].astype(vb_ref.dtype)


def _input_projection(x2, attn_norm, w_in, q_norm, k_norm, S, tm=512):
    N, D = x2.shape
    tm = min(tm, S)
    assert S % tm == 0 and N % S == 0
    n_s = S // tm
    rope_a, rope_b = _rope_tables(S, tm)
    seg = jnp.kron(jnp.eye(2, dtype=F32), jnp.full((HEAD_DIM, HEAD_DIM), 1.0 / HEAD_DIM, F32)).astype(BF16)
    two = lambda g: jnp.tile(g.reshape(1, HEAD_DIM), (1, 2))
    row = lambda i: (i, 0)
    const = lambda i: (0, 0)
    rope_specs = [pl.BlockSpec((1, 1, 2 * LANES), lambda i: (i % n_s, 0, 0)),
                  pl.BlockSpec((tm, 2 * LANES), const), pl.BlockSpec((1, LANES), const)]
    w = w_in.astype(BF16)
    out_w = (QA_W, 2 * LANES, None, QB_W, QB_W, QB_W)
    out_dt = (BF16, BF16, BF16, F32, F32, F32)
    vt_spec = pl.BlockSpec((N_KV_HEADS_A, VT_ROWS, tm), lambda i: (0, 0, i))
    vt_shape = jax.ShapeDtypeStruct((N_KV_HEADS_A, VT_ROWS, N), BF16)
    return pl.pallas_call(
        _inproj_kernel,
        grid=(N // tm,),
        in_specs=[pl.BlockSpec((tm, D), row), pl.BlockSpec((1, D), const),
                  pl.BlockSpec(w.shape, const), pl.BlockSpec((1, LANES), const),
                  pl.BlockSpec((1, LANES), const), pl.BlockSpec((LANES, LANES), const)]
                 + rope_specs + rope_specs,
        out_specs=[vt_spec if wd is None else pl.BlockSpec((tm, wd), row) for wd in out_w],
        out_shape=[vt_shape if wd is None else jax.ShapeDtypeStruct((N, wd), dt)
                   for wd, dt in zip(out_w, out_dt)],
        compiler_params=_cparams("parallel"),
        name="input_projection",
    )(x2, attn_norm.reshape(1, D), w, two(q_norm), two(k_norm), seg, *rope_a, *rope_b)


VT_ROWS = 80
LOG2E = 1.4426950408889634
ATTN_UNROLL = 16


def _attn_a_kernel(q_ref, k_ref, vt_ref, o_ref, st_scr, pt_scr, *, tk):
    tq = q_ref.shape[0]
    S = k_ref.shape[0]
    sub = lax.broadcasted_iota(jnp.int32, (LANES, tq), 0)
    lo = sub < HEAD_DIM
    q0 = q_ref[:, 0:LANES].astype(F32).T
    q1 = q_ref[:, LANES:2 * LANES].astype(F32).T
    zero = jnp.zeros_like(q0)
    qst = jnp.concatenate([jnp.where(lo, q0, zero), jnp.where(lo, zero, q0),
                           jnp.where(lo, q1, zero), jnp.where(lo, zero, q1)], axis=1).astype(BF16)
    cols = 4 * tq
    n_chunks = S // tk
    assert n_chunks % 2 == 0

    def scores(j, slot):
        start = pl.multiple_of(j * tk, tk)
        st_scr[slot] = jnp.dot(k_ref[pl.ds(start, tk), :], qst, preferred_element_type=F32)

    def softmax_pv(j, slot, m, acc):
        mx = st_scr[slot, 0:8, :]
        for r in range(1, tk // 8):
            mx = jnp.maximum(mx, st_scr[slot, 8 * r:8 * r + 8, :])
        m_new = jnp.maximum(m, jnp.max(mx, axis=0, keepdims=True))
        alpha = jnp.exp2(m - m_new)
        mb = jnp.broadcast_to(m_new, (16, cols))
        for r in range(tk // 16):
            blk = st_scr[slot, 16 * r:16 * r + 16, :]
            pt_scr[slot, 16 * r:16 * r + 16, :] = jnp.exp2((blk - mb).astype(BF16))
        start = pl.multiple_of(j * tk, tk)
        vt = vt_ref[0, :, pl.ds(start, tk)]
        acc = alpha * acc + jnp.dot(vt, pt_scr[slot], preferred_element_type=F32)
        return m_new, acc

    def body(jj, carry):
        m, acc = carry
        j0 = ATTN_UNROLL * jj
        for u in range(ATTN_UNROLL):
            scores(jnp.minimum(j0 + u + 1, n_chunks - 1), (u + 1) % 2)
            m, acc = softmax_pv(j0 + u, u % 2, m, acc)
        return m, acc

    assert n_chunks % ATTN_UNROLL == 0 and ATTN_UNROLL % 2 == 0
    scores(0, 0)
    init = (jnp.full((1, cols), -jnp.inf, F32), jnp.zeros((VT_ROWS, cols), F32))
    _, acc = lax.fori_loop(0, n_chunks // ATTN_UNROLL, body, init)
    ot = acc[0:HEAD_DIM] / acc[HEAD_DIM:HEAD_DIM + 1]
    for t in range(2):
        pair = jnp.concatenate([ot[:, (2 * t) * tq:(2 * t + 1) * tq],
                                ot[:, (2 * t + 1) * tq:(2 * t + 2) * tq]], axis=0)
        o_ref[:, t * LANES:(t + 1) * LANES] = pair.T.astype(o_ref.dtype)


def _grid_attention(qa, ka2, vat, B, S, tq=256, tk=256):
    N = qa.shape[0]
    tq, tk = min(tq, S), min(tk, S)
    assert S % tq == 0 and S % tk == 0
    nq = S // tq
    return pl.pallas_call(
        functools.partial(_attn_a_kernel, tk=tk),
        grid=(B, N_KV_HEADS_A, nq),
        in_specs=[pl.BlockSpec((tq, 2 * LANES), lambda b, h, i: (b * nq + i, h)),
                  pl.BlockSpec((S, LANES), lambda b, h, i: (b, h)),
                  pl.BlockSpec((1, VT_ROWS, S), lambda b, h, i: (h, 0, b))],
        out_specs=pl.BlockSpec((tq, 2 * LANES), lambda b, h, i: (b * nq + i, h)),
        out_shape=jax.ShapeDtypeStruct((N, QA_W), F32),
        scratch_shapes=[pltpu.VMEM((2, tk, 4 * tq), F32), pltpu.VMEM((2, tk, 4 * tq), BF16)],
        compiler_params=_cparams("parallel", "parallel", "parallel"),
        name="grid_attention",
    )(qa, ka2, vat)


DIL_QB = 128
DIL_R = 64
DIL_KW = DIL_QB + 2 * DIL_R
DIL_SB = 2048
DIL_UNROLL = 8


DIL_WINDOW_OFFS = (-DIL_R, 0, -2 * DIL_R)


def _dilated_kernel(q_ref, k_ref, v_ref, o_ref, acc_scr, m_scr, l_scr, mask_scr, *, seq):
    c = pl.program_id(2)
    lane = lax.broadcasted_iota(jnp.int32, (DIL_QB, LANES), 1)
    lo = lane < HEAD_DIM
    rel = (lax.broadcasted_iota(jnp.int32, (2 * DIL_QB, DIL_KW), 1)
           - lax.broadcasted_iota(jnp.int32, (2 * DIL_QB, DIL_KW), 0) % DIL_QB)
    for n, off in enumerate(DIL_WINDOW_OFFS):
        mask_scr[n] = jnp.where(jnp.abs(rel + off) <= DIL_R, 0.0, -1e30)
    ones = jnp.ones((DIL_KW, LANES), BF16)

    for bi, (window, d) in enumerate(DILATED_BRANCHES[::-1]):
        assert window // (2 * d) == DIL_R
        nb = DIL_SB // (DIL_QB * d)
        n_m = seq // d

        def block(it, _, d=d, nb=nb, n_m=n_m, first=(bi == 0)):
            r = it // nb
            i = it % nb
            row0 = r + d * DIL_QB * i
            m0 = (c * DIL_SB) // d + DIL_QB * i
            ks = jnp.clip(m0 - DIL_R, 0, n_m - DIL_KW)
            q = q_ref[pl.ds(row0, DIL_QB, stride=d), :]
            k = k_ref[pl.ds(r + d * ks, DIL_KW, stride=d), :].astype(BF16)
            v = v_ref[pl.ds(r + d * ks, DIL_KW, stride=d), :].astype(BF16)
            zero = jnp.zeros_like(q)
            qs = jnp.concatenate([jnp.where(lo, q, zero), jnp.where(lo, zero, q)], axis=0).astype(BF16)
            s = lax.dot_general(qs, k, (((1,), (1,)), ((), ())), preferred_element_type=F32)
            off = ks - m0
            s = s + mask_scr[jnp.where(off == DIL_WINDOW_OFFS[0], 0, jnp.where(off == DIL_WINDOW_OFFS[1], 1, 2))]
            mb = jnp.max(s, axis=-1, keepdims=True)
            p = jnp.exp2((s - mb).astype(BF16))
            pv = jnp.dot(p, jnp.concatenate([v, ones], axis=1), preferred_element_type=F32)
            acc_b = jnp.where(lo, pv[0:DIL_QB, 0:LANES], pv[DIL_QB:, 0:LANES])
            m_b = jnp.where(lo, mb[0:DIL_QB], mb[DIL_QB:])
            l_b = jnp.where(lo, pv[0:DIL_QB, LANES:], pv[DIL_QB:, LANES:])
            rows = pl.ds(row0, DIL_QB, stride=d)
            if first:
                acc_scr[rows, :] = acc_b
                m_scr[rows, :] = m_b
                l_scr[rows, :] = l_b
            else:
                m_old = m_scr[rows, :]
                m_new = jnp.maximum(m_old, m_b)
                a_old = jnp.exp2(m_old - m_new)
                a_new = jnp.exp2(m_b - m_new)
                acc_scr[rows, :] = acc_scr[rows, :] * a_old + acc_b * a_new
                l_scr[rows, :] = l_scr[rows, :] * a_old + l_b * a_new
                m_scr[rows, :] = m_new
            return 0

        lax.fori_loop(0, d * nb, block, 0, unroll=DIL_UNROLL)

    o_ref[...] = (acc_scr[...] / l_scr[...]).astype(o_ref.dtype)


def _dilated_attention(qb, kb, vb, B, S):
    N = qb.shape[0]
    assert S % DIL_SB == 0 and S // DILATED_BRANCHES[-1][1] >= DIL_KW
    nsb = S // DIL_SB
    return pl.pallas_call(
        functools.partial(_dilated_kernel, seq=S),
        grid=(B, QB_W // LANES, nsb),
        in_specs=[pl.BlockSpec((DIL_SB, LANES), lambda b, h, i: (b * nsb + i, h)),
                  pl.BlockSpec((S, LANES), lambda b, h, i: (b, h)),
                  pl.BlockSpec((S, LANES), lambda b, h, i: (b, h))],
        out_specs=pl.BlockSpec((DIL_SB, LANES), lambda b, h, i: (b * nsb + i, h)),
        out_shape=jax.ShapeDtypeStruct((N, QB_W), F32),
        scratch_shapes=[pltpu.VMEM((DIL_SB, LANES), F32)] * 3
                       + [pltpu.VMEM((len(DIL_WINDOW_OFFS), 2 * DIL_QB, DIL_KW), F32)],
        compiler_params=_cparams("parallel", "parallel", "parallel"),
        name="dilated_attention",
    )(qb, kb, vb)


SUBLANES = 8


def _store_token_tiles(ref, val):
    rows, d = val.shape
    assert d == SUBLANES * LANES
    for j in range(SUBLANES):
        ref[pl.ds(j, rows, stride=SUBLANES), :] = val[:, j * LANES:(j + 1) * LANES]


def _load_token_tiles(ref, rows, lead=None):
    idx = (lambda j: (pl.ds(j, rows, stride=SUBLANES), slice(None))) if lead is None else (
        lambda j: (lead, pl.ds(j, rows, stride=SUBLANES), slice(None)))
    return jnp.concatenate([ref[idx(j)] for j in range(SUBLANES)], axis=-1)


def _split_bf16(a):
    hi = a.astype(BF16)
    return hi, (a - hi.astype(F32)).astype(BF16)


def _outproj_router_kernel(x_ref, oa_ref, ob_ref, ga_ref, gb_ref, wo_ref, gf_ref, wr_hi_ref, wr_both_ref,
                           br_ref, tri_ref, x1_ref, hf_ref, idx_ref, gate_ref, rank_ref, cnt_ref,
                           carry_scr):
    @pl.when(pl.program_id(0) == 0)
    def _():
        carry_scr[...] = jnp.zeros_like(carry_scr)

    def rms(t, g):
        return t * lax.rsqrt(jnp.mean(t * t, axis=-1, keepdims=True) + EPS) * g

    mix = jnp.concatenate([rms(oa_ref[...], ga_ref[...]), rms(ob_ref[...], gb_ref[...])], axis=-1)
    x1 = x_ref[...] + jnp.dot(mix.astype(BF16), wo_ref[...], preferred_element_type=F32)
    x1_ref[...] = x1
    hf = rms(x1, gf_ref[...])
    _store_token_tiles(hf_ref, hf)

    h_hi, h_lo = _split_bf16(hf)
    nt = (((1,), (1,)), ((), ()))
    both = lax.dot_general(wr_both_ref[...], h_hi, nt, preferred_element_type=F32)
    E = both.shape[0] // 2
    logits = (both[0:E] + both[E:] + lax.dot_general(wr_hi_ref[...], h_lo, nt, preferred_element_type=F32)
              + br_ref[...])
    tm = logits.shape[1]
    eidx = lax.broadcasted_iota(jnp.int32, (E, tm), 0)
    work = logits
    vals, idxs, sel = [], [], jnp.zeros((E, tm), F32)
    for _ in range(TOP_K):
        mx = jnp.max(work, axis=0, keepdims=True)
        first = jnp.min(jnp.where(work == mx, eidx, E), axis=0, keepdims=True)
        hit = eidx == first
        vals.append(mx)
        idxs.append(first)
        sel = jnp.where(hit, 1.0, sel)
        work = jnp.where(hit, -jnp.inf, work)
    ex = [jnp.exp(v - vals[0]) for v in vals]
    den = ex[0] + ex[1] + ex[2] + ex[3]
    gate_ref[...] = jnp.concatenate(ex, axis=0) / den
    idx_ref[...] = jnp.concatenate(idxs, axis=0)

    before = jnp.dot(sel.astype(BF16), tri_ref[...], preferred_element_type=F32) + carry_scr[...]
    ranks = [jnp.sum(jnp.where(eidx == i, before, 0.0), axis=0, keepdims=True) for i in idxs]
    rank_ref[...] = jnp.concatenate(ranks, axis=0).astype(jnp.int32)
    carry_scr[...] = carry_scr[...] + jnp.sum(sel, axis=1, keepdims=True)
    cnt_ref[...] = jnp.broadcast_to(carry_scr[...], cnt_ref.shape).astype(jnp.int32)


def _outproj_router(x2, oa, ob, out_norm_a, out_norm_b, w_out, ffn_norm, w_router, b_router, tm=512):
    N, D = x2.shape
    tm = min(tm, N)
    assert N % tm == 0
    E = w_router.shape[1]
    wr_hi, wr_lo = _split_bf16(w_router.T)
    tri = (jnp.arange(tm)[:, None] < jnp.arange(tm)[None, :]).astype(BF16)
    row = lambda i: (i, 0)
    col = lambda i: (0, i)
    const = lambda i: (0, 0)
    return pl.pallas_call(
        _outproj_router_kernel,
        grid=(N // tm,),
        in_specs=[pl.BlockSpec((tm, D), row), pl.BlockSpec((tm, QA_W), row), pl.BlockSpec((tm, QB_W), row),
                  pl.BlockSpec((1, QA_W), const), pl.BlockSpec((1, QB_W), const),
                  pl.BlockSpec((QA_W + QB_W, D), const), pl.BlockSpec((1, D), const),
                  pl.BlockSpec((E, D), const), pl.BlockSpec((2 * E, D), const), pl.BlockSpec((E, 1), const),
                  pl.BlockSpec((tm, tm), const)],
        out_specs=[pl.BlockSpec((tm, D), row), pl.BlockSpec((tm * SUBLANES, LANES), row),
                   pl.BlockSpec((TOP_K, tm), col), pl.BlockSpec((TOP_K, tm), col),
                   pl.BlockSpec((TOP_K, tm), col), pl.BlockSpec((E, LANES), const)],
        out_shape=[jax.ShapeDtypeStruct((N, D), F32), jax.ShapeDtypeStruct((N * SUBLANES, LANES), F32),
                   jax.ShapeDtypeStruct((TOP_K, N), jnp.int32), jax.ShapeDtypeStruct((TOP_K, N), F32),
                   jax.ShapeDtypeStruct((TOP_K, N), jnp.int32), jax.ShapeDtypeStruct((E, LANES), jnp.int32)],
        scratch_shapes=[pltpu.VMEM((E, 1), F32)],
        compiler_params=_cparams("arbitrary"),
        name="outproj_router",
    )(x2, oa, ob, out_norm_a.reshape(1, -1), out_norm_b.reshape(1, -1), w_out.astype(BF16),
      ffn_norm.reshape(1, D), wr_hi, jnp.concatenate([wr_hi, wr_lo], axis=0), b_router.reshape(E, 1), tri)


EXPERT_ROWS = 256
DISPATCH_TM = 512


def _routing_plan(idx_t, rank_t, counts, n_blocks):
    pad_counts = (counts + EXPERT_ROWS - 1) // EXPERT_ROWS * EXPERT_ROWS
    pad_ends = jnp.cumsum(pad_counts)
    pad_starts = pad_ends - pad_counts
    eids = jnp.arange(N_EXPERTS, dtype=idx_t.dtype)
    dest = rank_t + jnp.sum(jnp.where(idx_t[..., None] == eids, pad_starts, 0), axis=-1)
    blk_start = jnp.arange(n_blocks, dtype=jnp.int32) * EXPERT_ROWS
    block_e = jnp.minimum(jnp.sum((pad_ends[None, :] <= blk_start[:, None]).astype(jnp.int32), axis=1),
                          N_EXPERTS - 1)
    n_used = (pad_ends[-1] // EXPERT_ROWS).astype(jnp.int32).reshape(1)
    nonempty = pad_counts > 0
    w_slot = ((jnp.cumsum(nonempty) - nonempty) % 2).astype(jnp.int32)
    later = (eids[None, :] > eids[:, None]) & nonempty[None, :]
    w_next = jnp.min(jnp.where(later, eids[None, :], N_EXPERTS), axis=1)
    w_next = jnp.where(w_next == N_EXPERTS, -1, w_next).astype(jnp.int32)
    return (dest.astype(jnp.int32), block_e, n_used, pad_starts.astype(jnp.int32), pad_counts.astype(jnp.int32),
            w_slot, w_next)


def _dispatch_kernel(dest_ref, cnt_ref, pstart_ref, pcnt_ref, nu_ref, hf_ref, xs_ref, zero_scr, sem, zsem, *,
                     n_tok):
    i = pl.program_id(0)
    tm = hf_ref.shape[0] // SUBLANES

    def tile(ref, row):
        return ref.at[pl.ds(pl.multiple_of(row * SUBLANES, SUBLANES), SUBLANES)]

    def tok(j, _):
        t = i * tm + j
        for k in range(TOP_K):
            pltpu.make_async_copy(tile(hf_ref, j), tile(xs_ref, dest_ref[k * n_tok + t]), sem).start(
                priority=k % 2)
        return 0
    lax.fori_loop(0, tm, tok, 0, unroll=2)

    @pl.when(i == 0)
    def _():
        zero_scr[...] = jnp.zeros_like(zero_scr)
        blk = EXPERT_ROWS * SUBLANES

        def pad_copy(row):
            return pltpu.make_async_copy(zero_scr.at[pl.ds(0, SUBLANES)], tile(xs_ref, row), zsem)

        def tail_copy(b):
            return pltpu.make_async_copy(zero_scr, xs_ref.at[pl.ds(pl.multiple_of(b * blk, blk), blk)], zsem)

        n_blocks = xs_ref.shape[0] // blk
        lax.fori_loop(nu_ref[0], n_blocks, lambda b, _: (tail_copy(b).start(), 0)[1], 0)
        lax.fori_loop(nu_ref[0], n_blocks, lambda b, _: (tail_copy(b).wait(), 0)[1], 0)

        def expert(e, _):
            base = pstart_ref[e]

            def pad_row(j, _):
                pad_copy(base + j).start()
                return 0
            lax.fori_loop(cnt_ref[e], pcnt_ref[e], pad_row, 0)

            def pad_wait(j, _):
                pad_copy(base + j).wait()
                return 0
            lax.fori_loop(cnt_ref[e], pcnt_ref[e], pad_wait, 0)
            return 0
        lax.fori_loop(0, N_EXPERTS, expert, 0)

    for k in range(TOP_K):
        pltpu.make_async_copy(hf_ref, xs_ref.at[pl.ds(0, tm * SUBLANES)], sem).wait()


def _dispatch(hf, dest, counts, pad_starts, pad_counts, n_used, rows_max):
    N = hf.shape[0] // SUBLANES
    tm = min(DISPATCH_TM, N)
    assert N % tm == 0
    return pl.pallas_call(
        functools.partial(_dispatch_kernel, n_tok=N),
        grid_spec=pltpu.PrefetchScalarGridSpec(
            num_scalar_prefetch=5, grid=(N // tm,),
            in_specs=[pl.BlockSpec((tm * SUBLANES, LANES), lambda i, *_: (i, 0))],
            out_specs=pl.BlockSpec(memory_space=pl.ANY),
            scratch_shapes=[pltpu.VMEM((EXPERT_ROWS * SUBLANES, LANES), F32), pltpu.SemaphoreType.DMA(()),
                            pltpu.SemaphoreType.DMA(())]),
        out_shape=jax.ShapeDtypeStruct((rows_max * SUBLANES, LANES), F32),
        compiler_params=_cparams("arbitrary"),
        name="expert_dispatch",
    )(dest.reshape(-1), counts, pad_starts, pad_counts, n_used, hf)


def _expert_kernel(be_ref, nu_ref, pstart_ref, wslot_ref, wnext_ref, xs_ref, wg_hbm, bg_ref, wu_hbm, bu_ref,
                   wd_hbm, bd_ref, ys_ref, wg_buf, wu_buf, wd_buf, wg_bf, wu_bf, wd_bf, sem):
    i = pl.program_id(0)
    e = be_ref[i]

    def weight_copies(expert, slot):
        return [pltpu.make_async_copy(src.at[expert], dst.at[slot], sem.at[slot, n])
                for n, (src, dst) in enumerate(((wg_hbm, wg_buf), (wu_hbm, wu_buf), (wd_hbm, wd_buf)))]

    @pl.when(i < nu_ref[0])
    def _():
        @pl.when(pstart_ref[e] == i * EXPERT_ROWS)
        def _():
            slot = wslot_ref[e]

            @pl.when(i == 0)
            def _():
                for c in weight_copies(e, slot):
                    c.start()
            for c in weight_copies(e, slot):
                c.wait()

            @pl.when(wnext_ref[e] >= 0)
            def _():
                for c in weight_copies(wnext_ref[e], 1 - slot):
                    c.start()
            wg_bf[...] = wg_buf[slot].astype(BF16)
            wu_bf[...] = wu_buf[slot].astype(BF16)
            wd_bf[...] = wd_buf[slot].astype(BF16)

        xb = _load_token_tiles(xs_ref, EXPERT_ROWS).astype(BF16)
        g = jnp.dot(xb, wg_bf[...], preferred_element_type=F32) + bg_ref[0]
        u = jnp.dot(xb, wu_bf[...], preferred_element_type=F32) + bu_ref[0]
        g = jnp.minimum(g, SWIGLU_LIMIT)
        u = jnp.clip(u, -SWIGLU_LIMIT, SWIGLU_LIMIT)
        act = (u + 1.0) * (g * jax.nn.sigmoid(SWIGLU_ALPHA * g))
        _store_token_tiles(ys_ref, jnp.dot(act.astype(BF16), wd_bf[...], preferred_element_type=F32)
                           + bd_ref[0])

    @pl.when(i >= nu_ref[0])
    def _():
        ys_ref[...] = jnp.zeros_like(ys_ref)


def _experts(xs, block_e, n_used, pad_starts, w_slot, w_next, w_gate, b_gate, w_up, b_up, w_down, b_down):
    E, D, F = w_gate.shape
    blk = EXPERT_ROWS * SUBLANES
    n_blocks = xs.shape[0] // blk
    xmap = lambda i, be, nu, *_: (jnp.minimum(i, nu[0] - 1), 0)
    bmap = lambda i, be, *_: (be[i], 0, 0)
    hbm = pl.BlockSpec(memory_space=pl.ANY)
    return pl.pallas_call(
        _expert_kernel,
        grid_spec=pltpu.PrefetchScalarGridSpec(
            num_scalar_prefetch=5, grid=(n_blocks,),
            in_specs=[pl.BlockSpec((blk, LANES), xmap),
                      hbm, pl.BlockSpec((1, 1, F), bmap),
                      hbm, pl.BlockSpec((1, 1, F), bmap),
                      hbm, pl.BlockSpec((1, 1, D), bmap)],
            out_specs=pl.BlockSpec((blk, LANES), lambda i, *_: (i, 0)),
            scratch_shapes=[pltpu.VMEM((2, D, F), F32), pltpu.VMEM((2, D, F), F32), pltpu.VMEM((2, F, D), F32),
                            pltpu.VMEM((D, F), BF16), pltpu.VMEM((D, F), BF16), pltpu.VMEM((F, D), BF16),
                            pltpu.SemaphoreType.DMA((2, 3))]),
        out_shape=jax.ShapeDtypeStruct(xs.shape, F32),
        compiler_params=_cparams("arbitrary"),
        name="expert_ffn",
    )(block_e, n_used, pad_starts, w_slot, w_next, xs, w_gate, b_gate.reshape(E, 1, F), w_up,
      b_up.reshape(E, 1, F), w_down, b_down.reshape(E, 1, D))


COMBINE_TM = 512


def _combine_kernel(dest_ref, x1_ref, gate_ref, fn_ref, ys_ref, o_ref, buf, sem, *, n_tok):
    i = pl.program_id(0)
    tm = x1_ref.shape[0]
    slot = i % 2

    def tile(row):
        return pl.ds(pl.multiple_of(row * SUBLANES, SUBLANES), SUBLANES)

    def gather(step, to_slot):
        def issue(j, _):
            t = step * tm + j
            for k in range(TOP_K):
                pltpu.make_async_copy(ys_ref.at[tile(dest_ref[k * n_tok + t])],
                                      buf.at[to_slot * TOP_K + k, tile(j)], sem.at[to_slot]).start(priority=k % 2)
            return 0
        lax.fori_loop(0, tm, issue, 0, unroll=2)

    @pl.when(i == 0)
    def _():
        gather(0, 0)

    @pl.when(i + 1 < pl.num_programs(0))
    def _():
        gather(i + 1, 1 - slot)

    for k in range(TOP_K):
        pltpu.make_async_copy(ys_ref.at[pl.ds(0, tm * SUBLANES)], buf.at[slot * TOP_K + k], sem.at[slot]).wait()

    gates = gate_ref[...]
    y = x1_ref[...]
    for k in range(TOP_K):
        y = y + _load_token_tiles(buf, tm, lead=slot * TOP_K + k) * gates[:, k:k + 1]
    ms = jnp.mean(y * y, axis=-1, keepdims=True)
    o_ref[...] = y * lax.rsqrt(ms + EPS) * fn_ref[...]


def _combine(ys, dest, x1, gates_t, final_norm):
    N, D = x1.shape
    tm = min(COMBINE_TM, N)
    assert N % tm == 0
    return pl.pallas_call(
        functools.partial(_combine_kernel, n_tok=N),
        grid_spec=pltpu.PrefetchScalarGridSpec(
            num_scalar_prefetch=1, grid=(N // tm,),
            in_specs=[pl.BlockSpec((tm, D), lambda i, d: (i, 0)),
                      pl.BlockSpec((tm, TOP_K), lambda i, d: (i, 0)),
                      pl.BlockSpec((1, D), lambda i, d: (0, 0)),
                      pl.BlockSpec(memory_space=pl.ANY)],
            out_specs=pl.BlockSpec((tm, D), lambda i, d: (i, 0)),
            scratch_shapes=[pltpu.VMEM((2 * TOP_K, tm * SUBLANES, LANES), F32),
                            pltpu.SemaphoreType.DMA((2,))]),
        out_shape=jax.ShapeDtypeStruct((N, D), F32),
        compiler_params=_cparams("arbitrary"),
        name="expert_combine",
    )(dest.reshape(-1), x1, gates_t.T, final_norm.reshape(1, D), ys)


def kernel(x, attn_norm, w_in, q_norm, k_norm, out_norm_a, out_norm_b, w_out, ffn_norm, w_router,
           b_router, w_gate, b_gate, w_up, b_up, w_down, b_down, final_norm):
    B, S, D = x.shape
    x2 = x.reshape(B * S, D)
    qa, ka2, va2, qb, kb, vb = _input_projection(x2, attn_norm[0], w_in[0], q_norm[0], k_norm[0], S)
    oa = _grid_attention(qa, ka2, va2, B, S)
    ob = _dilated_attention(qb, kb, vb, B, S)
    x1, hf, idx_t, gates_t, rank_t, cnt = _outproj_router(
        x2, oa, ob, out_norm_a[0], out_norm_b[0], w_out[0], ffn_norm[0], w_router[0], b_router[0])
    N = B * S
    n_blocks = (N * TOP_K) // EXPERT_ROWS + N_EXPERTS
    counts = cnt[:, 0]
    dest, block_e, n_used, pad_starts, pad_counts, w_slot, w_next = _routing_plan(
        idx_t, rank_t, counts, n_blocks)
    xs = _dispatch(hf, dest, counts, pad_starts, pad_counts, n_used, n_blocks * EXPERT_ROWS)
    ys = _experts(xs, block_e, n_used, pad_starts, w_slot, w_next,
                  w_gate[0], b_gate[0], w_up[0], b_up[0], w_down[0], b_down[0])
    out = _combine(ys, dest, x1, gates_t, final_norm)
    return out.reshape(B, S, D)
```

```python
import functools

import jax
import jax.numpy as jnp
from jax import lax
from jax.experimental import pallas as pl
from jax.experimental.pallas import tpu as pltpu

F32 = jnp.float32
BF16 = jnp.bfloat16

HEAD_DIM = 64
N_HEADS_A = 8
N_KV_HEADS_A = 2
N_HEADS_B = 8
DILATED_BRANCHES = ((128, 1), (512, 4), (2048, 16))
GRID_W = 64
ROPE_THETA = 10000.0
N_EXPERTS = 32
TOP_K = 4
SWIGLU_LIMIT = 7.0
SWIGLU_ALPHA = 1.702
EPS = 1e-6

LANES = 128
QA_W = N_HEADS_A * HEAD_DIM
KA_W = N_KV_HEADS_A * HEAD_DIM
QB_W = N_HEADS_B * HEAD_DIM
VMEM_LIMIT = 56 * 1024 * 1024


def _cparams(*sem):
    return pltpu.CompilerParams(dimension_semantics=sem, vmem_limit_bytes=VMEM_LIMIT)


def _rope_tables(S, tm):
    assert tm % GRID_W == 0 and S % tm == 0
    lane = jnp.arange(LANES)
    i = lane % HEAD_DIM
    half = HEAD_DIM // 2
    t0 = (jnp.arange(S // tm) * tm).astype(F32)[:, None]
    j = jnp.arange(tm).astype(F32)[:, None]
    inv_a = (ROPE_THETA ** (-jnp.arange(0, half, 2, dtype=F32) / half))[i % (half // 2)][None, :]
    is_row = (i < half)[None, :]
    base_a = jnp.where(is_row, jnp.floor(t0 / GRID_W) * inv_a, 0.0)
    jrow = jnp.floor(j / GRID_W)
    offs_a = jnp.where(is_row, jrow, j - jrow * GRID_W) * inv_a
    sgn_a = jnp.where((i // (half // 2)) % 2 == 0, -1.0, 1.0)[None, :].astype(F32)
    inv_b = (ROPE_THETA ** (-jnp.arange(0, HEAD_DIM, 2, dtype=F32) / HEAD_DIM))[i % half][None, :]
    base_b = t0 * inv_b
    offs_b = j * inv_b
    sgn_b = jnp.where(i < half, -1.0, 1.0)[None, :].astype(F32)
    cs = lambda a: jnp.concatenate([jnp.cos(a), jnp.sin(a)], axis=-1)
    return (cs(base_a)[:, None, :], cs(offs_a), sgn_a), (cs(base_b)[:, None, :], cs(offs_b), sgn_b)


def _rope_cos_sin(base_ref, offs_ref, sgn_ref):
    cb, sb = base_ref[0, :, 0:LANES], base_ref[0, :, LANES:2 * LANES]
    co, so = offs_ref[:, 0:LANES], offs_ref[:, LANES:2 * LANES]
    return cb * co - sb * so, (sb * co + cb * so) * sgn_ref[...]


def _rope_tile(x, cos, sin_signed, half):
    lane = lax.broadcasted_iota(jnp.int32, x.shape, 1)
    first = (lane % (2 * half)) < half
    partner = jnp.where(first, pltpu.roll(x, LANES - half, 1), pltpu.roll(x, half, 1))
    return x * cos + partner * sin_signed


def _head_rms(x, seg_mean, gain):
    ms = jnp.dot((x * x).astype(BF16), seg_mean, preferred_element_type=F32)
    return x * lax.rsqrt(ms + EPS) * gain


def _inproj_kernel(x_ref, g_ref, w_ref, qn_ref, kn_ref, seg_ref, base_a_ref, offs_a_ref, sgn_a_ref,
                   base_b_ref, offs_b_ref, sgn_b_ref, qa_ref, ka_ref, va_ref, qb_ref, kb_ref, vb_ref):
    x = x_ref[...]
    ms = jnp.mean(x * x, axis=-1, keepdims=True)
    hn = (x * lax.rsqrt(ms + EPS) * g_ref[...]).astype(BF16)
    proj = jnp.dot(hn, w_ref[...], preferred_element_type=F32)
    seg = seg_ref[...]
    cosa, sina = _rope_cos_sin(base_a_ref, offs_a_ref, sgn_a_ref)
    cosb, sinb = _rope_cos_sin(base_b_ref, offs_b_ref, sgn_b_ref)
    scale = HEAD_DIM ** -0.5
    lane = lax.broadcasted_iota(jnp.int32, (x.shape[0], LANES), 1)
    lo = lane < HEAD_DIM

    off = 0
    for j in range(QA_W // LANES):
        t = proj[:, off + j * LANES: off + (j + 1) * LANES]
        t = _rope_tile(_head_rms(t, seg, qn_ref[...]), cosa, sina, HEAD_DIM // 4)
        qa_ref[:, j * LANES:(j + 1) * LANES] = (t * (scale * LOG2E)).astype(qa_ref.dtype)
    off += QA_W
    k = _rope_tile(_head_rms(proj[:, off:off + LANES], seg, kn_ref[...]), cosa, sina, HEAD_DIM // 4)
    off += KA_W
    v = proj[:, off:off + LANES]
    off += KA_W
    sw = pltpu.roll(k, HEAD_DIM, 1)
    ka_ref[:, 0:LANES] = jnp.where(lo, k, sw).astype(ka_ref.dtype)
    ka_ref[:, LANES:2 * LANES] = jnp.where(lo, sw, k).astype(ka_ref.dtype)
    vt = v.T
    extra = (lax.broadcasted_iota(jnp.int32, (VT_ROWS - HEAD_DIM, v.shape[0]), 0) == 0).astype(F32)
    va_ref[0] = jnp.concatenate([vt[0:HEAD_DIM], extra], axis=0).astype(va_ref.dtype)
    va_ref[1] = jnp.concatenate([vt[HEAD_DIM:2 * HEAD_DIM], extra], axis=0).astype(va_ref.dtype)
    for j in range(QB_W // LANES):
        t = proj[:, off + j * LANES: off + (j + 1) * LANES]
        qb_ref[:, j * LANES:(j + 1) * LANES] = (
            _rope_tile(t, cosb, sinb, HEAD_DIM // 2) * (scale * LOG2E)).astype(qb_ref.dtype)
    off += QB_W
    for j in range(QB_W // LANES):
        t = proj[:, off + j * LANES: off + (j + 1) * LANES]
        kb_ref[:, j * LANES:(j + 1) * LANES] = _rope_tile(t, cosb, sinb, HEAD_DIM // 2).astype(kb_ref.dtype)
    off += QB_W
    vb_ref[...] = proj[:, off:off + QB_W].astype(vb_ref.dtype)


def _input_projection(x2, attn_norm, w_in, q_norm, k_norm, S, tm=512):
    N, D = x2.shape
    tm = min(tm, S)
    assert S % tm == 0 and N % S == 0
    n_s = S // tm
    rope_a, rope_b = _rope_tables(S, tm)
    seg = jnp.kron(jnp.eye(2, dtype=F32), jnp.full((HEAD_DIM, HEAD_DIM), 1.0 / HEAD_DIM, F32)).astype(BF16)
    two = lambda g: jnp.tile(g.reshape(1, HEAD_DIM), (1, 2))
    row = lambda i: (i, 0)
    const = lambda i: (0, 0)
    rope_specs = [pl.BlockSpec((1, 1, 2 * LANES), lambda i: (i % n_s, 0, 0)),
                  pl.BlockSpec((tm, 2 * LANES), const), pl.BlockSpec((1, LANES), const)]
    w = w_in.astype(BF16)
    out_w = (QA_W, 2 * LANES, None, QB_W, QB_W, QB_W)
    out_dt = (BF16, BF16, BF16, F32, F32, F32)
    vt_spec = pl.BlockSpec((N_KV_HEADS_A, VT_ROWS, tm), lambda i: (0, 0, i))
    vt_shape = jax.ShapeDtypeStruct((N_KV_HEADS_A, VT_ROWS, N), BF16)
    return pl.pallas_call(
        _inproj_kernel,
        grid=(N // tm,),
        in_specs=[pl.BlockSpec((tm, D), row), pl.BlockSpec((1, D), const),
                  pl.BlockSpec(w.shape, const), pl.BlockSpec((1, LANES), const),
                  pl.BlockSpec((1, LANES), const), pl.BlockSpec((LANES, LANES), const)]
                 + rope_specs + rope_specs,
        out_specs=[vt_spec if wd is None else pl.BlockSpec((tm, wd), row) for wd in out_w],
        out_shape=[vt_shape if wd is None else jax.ShapeDtypeStruct((N, wd), dt)
                   for wd, dt in zip(out_w, out_dt)],
        compiler_params=_cparams("parallel"),
        name="input_projection",
    )(x2, attn_norm.reshape(1, D), w, two(q_norm), two(k_norm), seg, *rope_a, *rope_b)


VT_ROWS = 80
LOG2E = 1.4426950408889634
ATTN_UNROLL = 32


def _attn_a_kernel(q_ref, k_ref, vt_ref, o_ref, st_scr, pt_scr, *, tk):
    tq = q_ref.shape[0]
    S = k_ref.shape[0]
    sub = lax.broadcasted_iota(jnp.int32, (LANES, tq), 0)
    lo = sub < HEAD_DIM
    q0 = q_ref[:, 0:LANES].astype(F32).T
    q1 = q_ref[:, LANES:2 * LANES].astype(F32).T
    zero = jnp.zeros_like(q0)
    qst = jnp.concatenate([jnp.where(lo, q0, zero), jnp.where(lo, zero, q0),
                           jnp.where(lo, q1, zero), jnp.where(lo, zero, q1)], axis=1).astype(BF16)
    cols = 4 * tq
    n_chunks = S // tk
    assert n_chunks % 2 == 0

    def scores(j, slot):
        start = pl.multiple_of(j * tk, tk)
        st_scr[slot] = jnp.dot(k_ref[pl.ds(start, tk), :], qst, preferred_element_type=F32)

    def softmax_pv(j, slot, m, acc):
        mx = st_scr[slot, 0:8, :]
        for r in range(1, tk // 8):
            mx = jnp.maximum(mx, st_scr[slot, 8 * r:8 * r + 8, :])
        m_new = jnp.maximum(m, jnp.max(mx, axis=0, keepdims=True))
        alpha = jnp.exp2(m - m_new)
        mb = jnp.broadcast_to(m_new, (16, cols))
        for r in range(tk // 16):
            blk = st_scr[slot, 16 * r:16 * r + 16, :]
            pt_scr[slot, 16 * r:16 * r + 16, :] = jnp.exp2((blk - mb).astype(BF16))
        start = pl.multiple_of(j * tk, tk)
        vt = vt_ref[0, :, pl.ds(start, tk)]
        acc = alpha * acc + jnp.dot(vt, pt_scr[slot], preferred_element_type=F32)
        return m_new, acc

    def body(jj, carry):
        m, acc = carry
        j0 = ATTN_UNROLL * jj
        for u in range(ATTN_UNROLL):
            scores(jnp.minimum(j0 + u + 1, n_chunks - 1), (u + 1) % 2)
            m, acc = softmax_pv(j0 + u, u % 2, m, acc)
        return m, acc

    assert n_chunks % ATTN_UNROLL == 0 and ATTN_UNROLL % 2 == 0
    scores(0, 0)
    init = (jnp.full((1, cols), -jnp.inf, F32), jnp.zeros((VT_ROWS, cols), F32))
    _, acc = lax.fori_loop(0, n_chunks // ATTN_UNROLL, body, init)
    ot = acc[0:HEAD_DIM] / acc[HEAD_DIM:HEAD_DIM + 1]
    for t in range(2):
        pair = jnp.concatenate([ot[:, (2 * t) * tq:(2 * t + 1) * tq],
                                ot[:, (2 * t + 1) * tq:(2 * t + 2) * tq]], axis=0)
        o_ref[:, t * LANES:(t + 1) * LANES] = pair.T.astype(o_ref.dtype)


def _grid_attention(qa, ka2, vat, B, S, tq=256, tk=256):
    N = qa.shape[0]
    tq, tk = min(tq, S), min(tk, S)
    assert S % tq == 0 and S % tk == 0
    nq = S // tq
    return pl.pallas_call(
        functools.partial(_attn_a_kernel, tk=tk),
        grid=(B, N_KV_HEADS_A, nq),
        in_specs=[pl.BlockSpec((tq, 2 * LANES), lambda b, h, i: (b * nq + i, h)),
                  pl.BlockSpec((S, LANES), lambda b, h, i: (b, h)),
                  pl.BlockSpec((1, VT_ROWS, S), lambda b, h, i: (h, 0, b))],
        out_specs=pl.BlockSpec((tq, 2 * LANES), lambda b, h, i: (b * nq + i, h)),
        out_shape=jax.ShapeDtypeStruct((N, QA_W), F32),
        scratch_shapes=[pltpu.VMEM((2, tk, 4 * tq), F32), pltpu.VMEM((2, tk, 4 * tq), BF16)],
        compiler_params=_cparams("parallel", "parallel", "parallel"),
        name="grid_attention",
    )(qa, ka2, vat)


DIL_QB = 128
DIL_R = 64
DIL_KW = DIL_QB + 2 * DIL_R
DIL_SB = 2048
DIL_UNROLL = 16


DIL_WINDOW_OFFS = (-DIL_R, 0, -2 * DIL_R)


def _dilated_kernel(q_ref, k_ref, v_ref, o_ref, acc_scr, m_scr, l_scr, mask_scr, *, seq):
    c = pl.program_id(2)
    lane = lax.broadcasted_iota(jnp.int32, (DIL_QB, LANES), 1)
    lo = lane < HEAD_DIM
    rel = (lax.broadcasted_iota(jnp.int32, (2 * DIL_QB, DIL_KW), 1)
           - lax.broadcasted_iota(jnp.int32, (2 * DIL_QB, DIL_KW), 0) % DIL_QB)
    for n, off in enumerate(DIL_WINDOW_OFFS):
        mask_scr[n] = jnp.where(jnp.abs(rel + off) <= DIL_R, 0.0, -1e30)
    ones = jnp.ones((DIL_KW, LANES), BF16)

    for bi, (window, d) in enumerate(DILATED_BRANCHES[::-1]):
        assert window // (2 * d) == DIL_R
        nb = DIL_SB // (DIL_QB * d)
        n_m = seq // d

        def block(it, _, d=d, nb=nb, n_m=n_m, first=(bi == 0)):
            r = it // nb
            i = it % nb
            row0 = r + d * DIL_QB * i
            m0 = (c * DIL_SB) // d + DIL_QB * i
            ks = jnp.clip(m0 - DIL_R, 0, n_m - DIL_KW)
            q = q_ref[pl.ds(row0, DIL_QB, stride=d), :]
            k = k_ref[pl.ds(r + d * ks, DIL_KW, stride=d), :].astype(BF16)
            v = v_ref[pl.ds(r + d * ks, DIL_KW, stride=d), :].astype(BF16)
            zero = jnp.zeros_like(q)
            qs = jnp.concatenate([jnp.where(lo, q, zero), jnp.where(lo, zero, q)], axis=0).astype(BF16)
            s = lax.dot_general(qs, k, (((1,), (1,)), ((), ())), preferred_element_type=F32)
            off = ks - m0
            s = s + mask_scr[jnp.where(off == DIL_WINDOW_OFFS[0], 0, jnp.where(off == DIL_WINDOW_OFFS[1], 1, 2))]
            mb = jnp.max(s, axis=-1, keepdims=True)
            p = jnp.exp2((s - mb).astype(BF16))
            pv = jnp.dot(p, jnp.concatenate([v, ones], axis=1), preferred_element_type=F32)
            acc_b = jnp.where(lo, pv[0:DIL_QB, 0:LANES], pv[DIL_QB:, 0:LANES])
            m_b = jnp.where(lo, mb[0:DIL_QB], mb[DIL_QB:])
            l_b = jnp.where(lo, pv[0:DIL_QB, LANES:], pv[DIL_QB:, LANES:])
            rows = pl.ds(row0, DIL_QB, stride=d)
            if first:
                acc_scr[rows, :] = acc_b
                m_scr[rows, :] = m_b
                l_scr[rows, :] = l_b
            else:
                m_old = m_scr[rows, :]
                m_new = jnp.maximum(m_old, m_b)
                a_old = jnp.exp2(m_old - m_new)
                a_new = jnp.exp2(m_b - m_new)
                acc_scr[rows, :] = acc_scr[rows, :] * a_old + acc_b * a_new
                l_scr[rows, :] = l_scr[rows, :] * a_old + l_b * a_new
                m_scr[rows, :] = m_new
            return 0

        lax.fori_loop(0, d * nb, block, 0, unroll=DIL_UNROLL)

    o_ref[...] = (acc_scr[...] / l_scr[...]).astype(o_ref.dtype)


def _dilated_attention(qb, kb, vb, B, S):
    N = qb.shape[0]
    assert S % DIL_SB == 0 and S // DILATED_BRANCHES[-1][1] >= DIL_KW
    nsb = S // DIL_SB
    return pl.pallas_call(
        functools.partial(_dilated_kernel, seq=S),
        grid=(B, QB_W // LANES, nsb),
        in_specs=[pl.BlockSpec((DIL_SB, LANES), lambda b, h, i: (b * nsb + i, h)),
                  pl.BlockSpec((S, LANES), lambda b, h, i: (b, h)),
                  pl.BlockSpec((S, LANES), lambda b, h, i: (b, h))],
        out_specs=pl.BlockSpec((DIL_SB, LANES), lambda b, h, i: (b * nsb + i, h)),
        out_shape=jax.ShapeDtypeStruct((N, QB_W), F32),
        scratch_shapes=[pltpu.VMEM((DIL_SB, LANES), F32)] * 3
                       + [pltpu.VMEM((len(DIL_WINDOW_OFFS), 2 * DIL_QB, DIL_KW), F32)],
        compiler_params=_cparams("parallel", "parallel", "parallel"),
        name="dilated_attention",
    )(qb, kb, vb)


SUBLANES = 8


def _store_token_tiles(ref, val):
    rows, d = val.shape
    assert d == SUBLANES * LANES
    for j in range(SUBLANES):
        ref[pl.ds(j, rows, stride=SUBLANES), :] = val[:, j * LANES:(j + 1) * LANES]


def _load_token_tiles(ref, rows, lead=None):
    idx = (lambda j: (pl.ds(j, rows, stride=SUBLANES), slice(None))) if lead is None else (
        lambda j: (lead, pl.ds(j, rows, stride=SUBLANES), slice(None)))
    return jnp.concatenate([ref[idx(j)] for j in range(SUBLANES)], axis=-1)


def _split_bf16(a):
    hi = a.astype(BF16)
    return hi, (a - hi.astype(F32)).astype(BF16)


def _outproj_router_kernel(x_ref, oa_ref, ob_ref, ga_ref, gb_ref, wo_ref, gf_ref, wr_hi_ref, wr_both_ref,
                           br_ref, tri_ref, x1_ref, hf_ref, idx_ref, gate_ref, rank_ref, cnt_ref,
                           carry_scr):
    @pl.when(pl.program_id(0) == 0)
    def _():
        carry_scr[...] = jnp.zeros_like(carry_scr)

    def rms(t, g):
        return t * lax.rsqrt(jnp.mean(t * t, axis=-1, keepdims=True) + EPS) * g

    mix = jnp.concatenate([rms(oa_ref[...], ga_ref[...]), rms(ob_ref[...], gb_ref[...])], axis=-1)
    x1 = x_ref[...] + jnp.dot(mix.astype(BF16), wo_ref[...], preferred_element_type=F32)
    x1_ref[...] = x1
    hf = rms(x1, gf_ref[...])
    _store_token_tiles(hf_ref, hf)

    h_hi, h_lo = _split_bf16(hf)
    nt = (((1,), (1,)), ((), ()))
    both = lax.dot_general(wr_both_ref[...], h_hi, nt, preferred_element_type=F32)
    E = both.shape[0] // 2
    logits = (both[0:E] + both[E:] + lax.dot_general(wr_hi_ref[...], h_lo, nt, preferred_element_type=F32)
              + br_ref[...])
    tm = logits.shape[1]
    eidx = lax.broadcasted_iota(jnp.int32, (E, tm), 0)
    work = logits
    vals, idxs, sel = [], [], jnp.zeros((E, tm), F32)
    for _ in range(TOP_K):
        mx = jnp.max(work, axis=0, keepdims=True)
        first = jnp.min(jnp.where(work == mx, eidx, E), axis=0, keepdims=True)
        hit = eidx == first
        vals.append(mx)
        idxs.append(first)
        sel = jnp.where(hit, 1.0, sel)
        work = jnp.where(hit, -jnp.inf, work)
    ex = [jnp.exp(v - vals[0]) for v in vals]
    den = ex[0] + ex[1] + ex[2] + ex[3]
    gate_ref[...] = jnp.concatenate(ex, axis=0) / den
    idx_ref[...] = jnp.concatenate(idxs, axis=0)

    before = jnp.dot(sel.astype(BF16), tri_ref[...], preferred_element_type=F32) + carry_scr[...]
    ranks = [jnp.sum(jnp.where(eidx == i, before, 0.0), axis=0, keepdims=True) for i in idxs]
    rank_ref[...] = jnp.concatenate(ranks, axis=0).astype(jnp.int32)
    carry_scr[...] = carry_scr[...] + jnp.sum(sel, axis=1, keepdims=True)
    cnt_ref[...] = jnp.broadcast_to(carry_scr[...], cnt_ref.shape).astype(jnp.int32)


def _outproj_router(x2, oa, ob, out_norm_a, out_norm_b, w_out, ffn_norm, w_router, b_router, tm=512):
    N, D = x2.shape
    tm = min(tm, N)
    assert N % tm == 0
    E = w_router.shape[1]
    wr_hi, wr_lo = _split_bf16(w_router.T)
    tri = (jnp.arange(tm)[:, None] < jnp.arange(tm)[None, :]).astype(BF16)
    row = lambda i: (i, 0)
    col = lambda i: (0, i)
    const = lambda i: (0, 0)
    return pl.pallas_call(
        _outproj_router_kernel,
        grid=(N // tm,),
        in_specs=[pl.BlockSpec((tm, D), row), pl.BlockSpec((tm, QA_W), row), pl.BlockSpec((tm, QB_W), row),
                  pl.BlockSpec((1, QA_W), const), pl.BlockSpec((1, QB_W), const),
                  pl.BlockSpec((QA_W + QB_W, D), const), pl.BlockSpec((1, D), const),
                  pl.BlockSpec((E, D), const), pl.BlockSpec((2 * E, D), const), pl.BlockSpec((E, 1), const),
                  pl.BlockSpec((tm, tm), const)],
        out_specs=[pl.BlockSpec((tm, D), row), pl.BlockSpec((tm * SUBLANES, LANES), row),
                   pl.BlockSpec((TOP_K, tm), col), pl.BlockSpec((TOP_K, tm), col),
                   pl.BlockSpec((TOP_K, tm), col), pl.BlockSpec((E, LANES), const)],
        out_shape=[jax.ShapeDtypeStruct((N, D), F32), jax.ShapeDtypeStruct((N * SUBLANES, LANES), F32),
                   jax.ShapeDtypeStruct((TOP_K, N), jnp.int32), jax.ShapeDtypeStruct((TOP_K, N), F32),
                   jax.ShapeDtypeStruct((TOP_K, N), jnp.int32), jax.ShapeDtypeStruct((E, LANES), jnp.int32)],
        scratch_shapes=[pltpu.VMEM((E, 1), F32)],
        compiler_params=_cparams("arbitrary"),
        name="outproj_router",
    )(x2, oa, ob, out_norm_a.reshape(1, -1), out_norm_b.reshape(1, -1), w_out.astype(BF16),
      ffn_norm.reshape(1, D), wr_hi, jnp.concatenate([wr_hi, wr_lo], axis=0), b_router.reshape(E, 1), tri)


EXPERT_ROWS = 256
DISPATCH_TM = 512


def _routing_plan(idx_t, rank_t, counts, n_blocks):
    pad_counts = (counts + EXPERT_ROWS - 1) // EXPERT_ROWS * EXPERT_ROWS
    pad_ends = jnp.cumsum(pad_counts)
    pad_starts = pad_ends - pad_counts
    eids = jnp.arange(N_EXPERTS, dtype=idx_t.dtype)
    dest = rank_t + jnp.sum(jnp.where(idx_t[..., None] == eids, pad_starts, 0), axis=-1)
    blk_start = jnp.arange(n_blocks, dtype=jnp.int32) * EXPERT_ROWS
    block_e = jnp.minimum(jnp.sum((pad_ends[None, :] <= blk_start[:, None]).astype(jnp.int32), axis=1),
                          N_EXPERTS - 1)
    n_used = (pad_ends[-1] // EXPERT_ROWS).astype(jnp.int32).reshape(1)
    nonempty = pad_counts > 0
    w_slot = ((jnp.cumsum(nonempty) - nonempty) % 2).astype(jnp.int32)
    later = (eids[None, :] > eids[:, None]) & nonempty[None, :]
    w_next = jnp.min(jnp.where(later, eids[None, :], N_EXPERTS), axis=1)
    w_next = jnp.where(w_next == N_EXPERTS, -1, w_next).astype(jnp.int32)
    return (dest.astype(jnp.int32), block_e, n_used, pad_starts.astype(jnp.int32), pad_counts.astype(jnp.int32),
            w_slot, w_next)


def _dispatch_kernel(dest_ref, cnt_ref, pstart_ref, pcnt_ref, nu_ref, hf_ref, xs_ref, zero_scr, sem, zsem, *,
                     n_tok):
    i = pl.program_id(0)
    tm = hf_ref.shape[0] // SUBLANES

    def tile(ref, row):
        return ref.at[pl.ds(pl.multiple_of(row * SUBLANES, SUBLANES), SUBLANES)]

    def tok(j, _):
        t = i * tm + j
        for k in range(TOP_K):
            pltpu.make_async_copy(tile(hf_ref, j), tile(xs_ref, dest_ref[k * n_tok + t]), sem).start(
                priority=k % 2)
        return 0
    lax.fori_loop(0, tm, tok, 0, unroll=2)

    @pl.when(i == 0)
    def _():
        zero_scr[...] = jnp.zeros_like(zero_scr)
        blk = EXPERT_ROWS * SUBLANES

        def pad_copy(row):
            return pltpu.make_async_copy(zero_scr.at[pl.ds(0, SUBLANES)], tile(xs_ref, row), zsem)

        def tail_copy(b):
            return pltpu.make_async_copy(zero_scr, xs_ref.at[pl.ds(pl.multiple_of(b * blk, blk), blk)], zsem)

        n_blocks = xs_ref.shape[0] // blk
        lax.fori_loop(nu_ref[0], n_blocks, lambda b, _: (tail_copy(b).start(), 0)[1], 0)
        lax.fori_loop(nu_ref[0], n_blocks, lambda b, _: (tail_copy(b).wait(), 0)[1], 0)

        def expert(e, _):
            base = pstart_ref[e]

            def pad_row(j, _):
                pad_copy(base + j).start()
                return 0
            lax.fori_loop(cnt_ref[e], pcnt_ref[e], pad_row, 0)

            def pad_wait(j, _):
                pad_copy(base + j).wait()
                return 0
            lax.fori_loop(cnt_ref[e], pcnt_ref[e], pad_wait, 0)
            return 0
        lax.fori_loop(0, N_EXPERTS, expert, 0)

    for k in range(TOP_K):
        pltpu.make_async_copy(hf_ref, xs_ref.at[pl.ds(0, tm * SUBLANES)], sem).wait()


def _dispatch(hf, dest, counts, pad_starts, pad_counts, n_used, rows_max):
    N = hf.shape[0] // SUBLANES
    tm = min(DISPATCH_TM, N)
    assert N % tm == 0
    return pl.pallas_call(
        functools.partial(_dispatch_kernel, n_tok=N),
        grid_spec=pltpu.PrefetchScalarGridSpec(
            num_scalar_prefetch=5, grid=(N // tm,),
            in_specs=[pl.BlockSpec((tm * SUBLANES, LANES), lambda i, *_: (i, 0))],
            out_specs=pl.BlockSpec(memory_space=pl.ANY),
            scratch_shapes=[pltpu.VMEM((EXPERT_ROWS * SUBLANES, LANES), F32), pltpu.SemaphoreType.DMA(()),
                            pltpu.SemaphoreType.DMA(())]),
        out_shape=jax.ShapeDtypeStruct((rows_max * SUBLANES, LANES), F32),
        compiler_params=_cparams("arbitrary"),
        name="expert_dispatch",
    )(dest.reshape(-1), counts, pad_starts, pad_counts, n_used, hf)


def _expert_kernel(be_ref, nu_ref, pstart_ref, wslot_ref, wnext_ref, xs_ref, wg_hbm, bg_ref, wu_hbm, bu_ref,
                   wd_hbm, bd_ref, ys_ref, wg_buf, wu_buf, wd_buf, wg_bf, wu_bf, wd_bf, sem):
    i = pl.program_id(0)
    e = be_ref[i]

    def weight_copies(expert, slot):
        return [pltpu.make_async_copy(src.at[expert], dst.at[slot], sem.at[slot, n])
                for n, (src, dst) in enumerate(((wg_hbm, wg_buf), (wu_hbm, wu_buf), (wd_hbm, wd_buf)))]

    @pl.when(i < nu_ref[0])
    def _():
        @pl.when(pstart_ref[e] == i * EXPERT_ROWS)
        def _():
            slot = wslot_ref[e]

            @pl.when(i == 0)
            def _():
                for c in weight_copies(e, slot):
                    c.start()
            for c in weight_copies(e, slot):
                c.wait()

            @pl.when(wnext_ref[e] >= 0)
            def _():
                for c in weight_copies(wnext_ref[e], 1 - slot):
                    c.start()
            wg_bf[...] = wg_buf[slot].astype(BF16)
            wu_bf[...] = wu_buf[slot].astype(BF16)
            wd_bf[...] = wd_buf[slot].astype(BF16)

        xb = _load_token_tiles(xs_ref, EXPERT_ROWS).astype(BF16)
        g = jnp.dot(xb, wg_bf[...], preferred_element_type=F32) + bg_ref[0]
        u = jnp.dot(xb, wu_bf[...], preferred_element_type=F32) + bu_ref[0]
        g = jnp.minimum(g, SWIGLU_LIMIT)
        u = jnp.clip(u, -SWIGLU_LIMIT, SWIGLU_LIMIT)
        act = (u + 1.0) * (g * jax.nn.sigmoid(SWIGLU_ALPHA * g))
        _store_token_tiles(ys_ref, jnp.dot(act.astype(BF16), wd_bf[...], preferred_element_type=F32)
                           + bd_ref[0])

    @pl.when(i >= nu_ref[0])
    def _():
        ys_ref[...] = jnp.zeros_like(ys_ref)


def _experts(xs, block_e, n_used, pad_starts, w_slot, w_next, w_gate, b_gate, w_up, b_up, w_down, b_down):
    E, D, F = w_gate.shape
    blk = EXPERT_ROWS * SUBLANES
    n_blocks = xs.shape[0] // blk
    xmap = lambda i, be, nu, *_: (jnp.minimum(i, nu[0] - 1), 0)
    bmap = lambda i, be, *_: (be[i], 0, 0)
    hbm = pl.BlockSpec(memory_space=pl.ANY)
    return pl.pallas_call(
        _expert_kernel,
        grid_spec=pltpu.PrefetchScalarGridSpec(
            num_scalar_prefetch=5, grid=(n_blocks,),
            in_specs=[pl.BlockSpec((blk, LANES), xmap),
                      hbm, pl.BlockSpec((1, 1, F), bmap),
                      hbm, pl.BlockSpec((1, 1, F), bmap),
                      hbm, pl.BlockSpec((1, 1, D), bmap)],
            out_specs=pl.BlockSpec((blk, LANES), lambda i, *_: (i, 0)),
            scratch_shapes=[pltpu.VMEM((2, D, F), F32), pltpu.VMEM((2, D, F), F32), pltpu.VMEM((2, F, D), F32),
                            pltpu.VMEM((D, F), BF16), pltpu.VMEM((D, F), BF16), pltpu.VMEM((F, D), BF16),
                            pltpu.SemaphoreType.DMA((2, 3))]),
        out_shape=jax.ShapeDtypeStruct(xs.shape, F32),
        compiler_params=_cparams("arbitrary"),
        name="expert_ffn",
    )(block_e, n_used, pad_starts, w_slot, w_next, xs, w_gate, b_gate.reshape(E, 1, F), w_up,
      b_up.reshape(E, 1, F), w_down, b_down.reshape(E, 1, D))


COMBINE_TM = 512


def _combine_kernel(dest_ref, x1_ref, gate_ref, fn_ref, ys_ref, o_ref, buf, sem, *, n_tok):
    i = pl.program_id(0)
    tm = x1_ref.shape[0]
    slot = i % 2

    def tile(row):
        return pl.ds(pl.multiple_of(row * SUBLANES, SUBLANES), SUBLANES)

    def gather(step, to_slot):
        def issue(j, _):
            t = step * tm + j
            for k in range(TOP_K):
                pltpu.make_async_copy(ys_ref.at[tile(dest_ref[k * n_tok + t])],
                                      buf.at[to_slot * TOP_K + k, tile(j)], sem.at[to_slot]).start(priority=k % 2)
            return 0
        lax.fori_loop(0, tm, issue, 0, unroll=2)

    @pl.when(i == 0)
    def _():
        gather(0, 0)

    @pl.when(i + 1 < pl.num_programs(0))
    def _():
        gather(i + 1, 1 - slot)

    for k in range(TOP_K):
        pltpu.make_async_copy(ys_ref.at[pl.ds(0, tm * SUBLANES)], buf.at[slot * TOP_K + k], sem.at[slot]).wait()

    gates = gate_ref[...]
    y = x1_ref[...]
    for k in range(TOP_K):
        y = y + _load_token_tiles(buf, tm, lead=slot * TOP_K + k) * gates[:, k:k + 1]
    ms = jnp.mean(y * y, axis=-1, keepdims=True)
    o_ref[...] = y * lax.rsqrt(ms + EPS) * fn_ref[...]


def _combine(ys, dest, x1, gates_t, final_norm):
    N, D = x1.shape
    tm = min(COMBINE_TM, N)
    assert N % tm == 0
    return pl.pallas_call(
        functools.partial(_combine_kernel, n_tok=N),
        grid_spec=pltpu.PrefetchScalarGridSpec(
            num_scalar_prefetch=1, grid=(N // tm,),
            in_specs=[pl.BlockSpec((tm, D), lambda i, d: (i, 0)),
                      pl.BlockSpec((tm, TOP_K), lambda i, d: (i, 0)),
                      pl.BlockSpec((1, D), lambda i, d: (0, 0)),
                      pl.BlockSpec(memory_space=pl.ANY)],
            out_specs=pl.BlockSpec((tm, D), lambda i, d: (i, 0)),
            scratch_shapes=[pltpu.VMEM((2 * TOP_K, tm * SUBLANES, LANES), F32),
                            pltpu.SemaphoreType.DMA((2,))]),
        out_shape=jax.ShapeDtypeStruct((N, D), F32),
        compiler_params=_cparams("arbitrary"),
        name="expert_combine",
    )(dest.reshape(-1), x1, gates_t.T, final_norm.reshape(1, D), ys)


def kernel(x, attn_norm, w_in, q_norm, k_norm, out_norm_a, out_norm_b, w_out, ffn_norm, w_router,
           b_router, w_gate, b_gate, w_up, b_up, w_down, b_down, final_norm):
    B, S, D = x.shape
    x2 = x.reshape(B * S, D)
    qa, ka2, va2, qb, kb, vb = _input_projection(x2, attn_norm[0], w_in[0], q_norm[0], k_norm[0], S)
    oa = _grid_attention(qa, ka2, va2, B, S)
    ob = _dilated_attention(qb, kb, vb, B, S)
    x1, hf, idx_t, gates_t, rank_t, cnt = _outproj_router(
        x2, oa, ob, out_norm_a[0], out_norm_b[0], w_out[0], ffn_norm[0], w_router[0], b_router[0])
    N = B * S
    n_blocks = (N * TOP_K) // EXPERT_ROWS + N_EXPERTS
    counts = cnt[:, 0]
    dest, block_e, n_used, pad_starts, pad_counts, w_slot, w_next = _routing_plan(
        idx_t, rank_t, counts, n_blocks)
    xs = _dispatch(hf, dest, counts, pad_starts, pad_counts, n_used, n_blocks * EXPERT_ROWS)
    ys = _experts(xs, block_e, n_used, pad_starts, w_slot, w_next,
                  w_gate[0], b_gate[0], w_up[0], b_up[0], w_down[0], b_down[0])
    out = _combine(ys, dest, x1, gates_t, final_norm)
    return out.reshape(B, S, D)
```

```python
import functools
import math

import jax
import jax.numpy as jnp
from jax import lax
from jax.experimental import pallas as pl
from jax.experimental.pallas import tpu as pltpu

F32 = jnp.float32
BF16 = jnp.bfloat16

HEAD_DIM = 64
N_HEADS_A = 8
N_KV_HEADS_A = 2
N_HEADS_B = 8
DILATED_BRANCHES = ((128, 1), (512, 4), (2048, 16))
GRID_W = 64
ROPE_THETA = 10000.0
N_EXPERTS = 32
TOP_K = 4
SWIGLU_LIMIT = 7.0
SWIGLU_ALPHA = 1.702
EPS = 1e-6

LANES = 128
QA_W = N_HEADS_A * HEAD_DIM
KA_W = N_KV_HEADS_A * HEAD_DIM
QB_W = N_HEADS_B * HEAD_DIM
VMEM_LIMIT = 56 * 1024 * 1024


def _cparams(*sem):
    return pltpu.CompilerParams(dimension_semantics=sem, vmem_limit_bytes=VMEM_LIMIT)


def _rope_tables(S, tm):
    assert tm % GRID_W == 0 and S % tm == 0
    lane = jnp.arange(LANES)
    i = lane % HEAD_DIM
    half = HEAD_DIM // 2
    t0 = (jnp.arange(S // tm) * tm).astype(F32)[:, None]
    j = jnp.arange(tm).astype(F32)[:, None]
    inv_a = (ROPE_THETA ** (-jnp.arange(0, half, 2, dtype=F32) / half))[i % (half // 2)][None, :]
    is_row = (i < half)[None, :]
    base_a = jnp.where(is_row, jnp.floor(t0 / GRID_W) * inv_a, 0.0)
    jrow = jnp.floor(j / GRID_W)
    offs_a = jnp.where(is_row, jrow, j - jrow * GRID_W) * inv_a
    sgn_a = jnp.where((i // (half // 2)) % 2 == 0, -1.0, 1.0)[None, :].astype(F32)
    inv_b = (ROPE_THETA ** (-jnp.arange(0, HEAD_DIM, 2, dtype=F32) / HEAD_DIM))[i % half][None, :]
    base_b = t0 * inv_b
    offs_b = j * inv_b
    sgn_b = jnp.where(i < half, -1.0, 1.0)[None, :].astype(F32)
    cs = lambda a: jnp.concatenate([jnp.cos(a), jnp.sin(a)], axis=-1)
    return (cs(base_a)[:, None, :], cs(offs_a), sgn_a), (cs(base_b)[:, None, :], cs(offs_b), sgn_b)


def _rope_cos_sin(base_ref, offs_ref, sgn_ref):
    cb, sb = base_ref[0, :, 0:LANES], base_ref[0, :, LANES:2 * LANES]
    co, so = offs_ref[:, 0:LANES], offs_ref[:, LANES:2 * LANES]
    return cb * co - sb * so, (sb * co + cb * so) * sgn_ref[...]


def _rope_tile(x, cos, sin_signed, half):
    lane = lax.broadcasted_iota(jnp.int32, x.shape, 1)
    first = (lane % (2 * half)) < half
    partner = jnp.where(first, pltpu.roll(x, LANES - half, 1), pltpu.roll(x, half, 1))
    return x * cos + partner * sin_signed


def _head_rms(x, seg_mean, gain):
    ms = jnp.dot((x * x).astype(BF16), seg_mean, preferred_element_type=F32)
    return x * lax.rsqrt(ms + EPS) * gain


def _inproj_kernel(x_ref, g_ref, w_ref, qn_ref, kn_ref, seg_ref, base_a_ref, offs_a_ref, sgn_a_ref,
                   base_b_ref, offs_b_ref, sgn_b_ref, qa_ref, ka_ref, va_ref, qb_ref, kb_ref, vb_ref):
    x = x_ref[...]
    ms = jnp.mean(x * x, axis=-1, keepdims=True)
    hn = (x * lax.rsqrt(ms + EPS) * g_ref[...]).astype(BF16)
    proj = jnp.dot(hn, w_ref[...], preferred_element_type=F32)
    seg = seg_ref[...]
    cosa, sina = _rope_cos_sin(base_a_ref, offs_a_ref, sgn_a_ref)
    cosb, sinb = _rope_cos_sin(base_b_ref, offs_b_ref, sgn_b_ref)
    scale = HEAD_DIM ** -0.5
    lane = lax.broadcasted_iota(jnp.int32, (x.shape[0], LANES), 1)
    lo = lane < HEAD_DIM

    off = 0
    for j in range(QA_W // LANES):
        t = proj[:, off + j * LANES: off + (j + 1) * LANES]
        t = _rope_tile(_head_rms(t, seg, qn_ref[...]), cosa, sina, HEAD_DIM // 4)
        qa_ref[:, j * LANES:(j + 1) * LANES] = (t * (scale * LOG2E)).astype(qa_ref.dtype)
    off += QA_W
    k = _rope_tile(_head_rms(proj[:, off:off + LANES], seg, kn_ref[...]), cosa, sina, HEAD_DIM // 4)
    off += KA_W
    v = proj[:, off:off + LANES]
    off += KA_W
    sw = pltpu.roll(k, HEAD_DIM, 1)
    ka_ref[:, 0:LANES] = jnp.where(lo, k, sw).astype(ka_ref.dtype)
    ka_ref[:, LANES:2 * LANES] = jnp.where(lo, sw, k).astype(ka_ref.dtype)
    vt = v.T
    extra = (lax.broadcasted_iota(jnp.int32, (VT_ROWS - HEAD_DIM, v.shape[0]), 0) == 0).astype(F32)
    va_ref[0] = jnp.concatenate([vt[0:HEAD_DIM], extra], axis=0).astype(va_ref.dtype)
    va_ref[1] = jnp.concatenate([vt[HEAD_DIM:2 * HEAD_DIM], extra], axis=0).astype(va_ref.dtype)
    for j in range(QB_W // LANES):
        t = proj[:, off + j * LANES: off + (j + 1) * LANES]
        qb_ref[:, j * LANES:(j + 1) * LANES] = (
            _rope_tile(t, cosb, sinb, HEAD_DIM // 2) * (scale * LOG2E)).astype(qb_ref.dtype)
    off += QB_W
    for j in range(QB_W // LANES):
        t = proj[:, off + j * LANES: off + (j + 1) * LANES]
        kb_ref[:, j * LANES:(j + 1) * LANES] = _rope_tile(t, cosb, sinb, HEAD_DIM // 2).astype(kb_ref.dtype)
    off += QB_W
    vb_ref[...] = proj[:, off:off + QB_W].astype(vb_ref.dtype)


def _input_projection(x2, attn_norm, w_in, q_norm, k_norm, S, tm=512):
    N, D = x2.shape
    tm = min(tm, S)
    assert S % tm == 0 and N % S == 0
    n_s = S // tm
    rope_a, rope_b = _rope_tables(S, tm)
    seg = jnp.kron(jnp.eye(2, dtype=F32), jnp.full((HEAD_DIM, HEAD_DIM), 1.0 / HEAD_DIM, F32)).astype(BF16)
    two = lambda g: jnp.tile(g.reshape(1, HEAD_DIM), (1, 2))
    row = lambda i: (i, 0)
    const = lambda i: (0, 0)
    rope_specs = [pl.BlockSpec((1, 1, 2 * LANES), lambda i: (i % n_s, 0, 0)),
                  pl.BlockSpec((tm, 2 * LANES), const), pl.BlockSpec((1, LANES), const)]
    w = w_in.astype(BF16)
    out_w = (QA_W, 2 * LANES, None, QB_W, QB_W, QB_W)
    out_dt = (BF16, BF16, BF16, F32, F32, F32)
    vt_spec = pl.BlockSpec((N_KV_HEADS_A, VT_ROWS, tm), lambda i: (0, 0, i))
    vt_shape = jax.ShapeDtypeStruct((N_KV_HEADS_A, VT_ROWS, N), BF16)
    return pl.pallas_call(
        _inproj_kernel,
        grid=(N // tm,),
        in_specs=[pl.BlockSpec((tm, D), row), pl.BlockSpec((1, D), const),
                  pl.BlockSpec(w.shape, const), pl.BlockSpec((1, LANES), const),
                  pl.BlockSpec((1, LANES), const), pl.BlockSpec((LANES, LANES), const)]
                 + rope_specs + rope_specs,
        out_specs=[vt_spec if wd is None else pl.BlockSpec((tm, wd), row) for wd in out_w],
        out_shape=[vt_shape if wd is None else jax.ShapeDtypeStruct((N, wd), dt)
                   for wd, dt in zip(out_w, out_dt)],
        compiler_params=_cparams("parallel"),
        name="input_projection",
    )(x2, attn_norm.reshape(1, D), w, two(q_norm), two(k_norm), seg, *rope_a, *rope_b)


VT_ROWS = 80
LOG2E = 1.4426950408889634
ATTN_UNROLL = 32


def _attn_a_kernel(q_ref, k_ref, vt_ref, o_ref, st_scr, pt_scr, *, tk):
    tq = q_ref.shape[0]
    S = k_ref.shape[0]
    sub = lax.broadcasted_iota(jnp.int32, (LANES, tq), 0)
    lo = sub < HEAD_DIM
    q0 = q_ref[:, 0:LANES].astype(F32).T
    q1 = q_ref[:, LANES:2 * LANES].astype(F32).T
    zero = jnp.zeros_like(q0)
    qst = jnp.concatenate([jnp.where(lo, q0, zero), jnp.where(lo, zero, q0),
                           jnp.where(lo, q1, zero), jnp.where(lo, zero, q1)], axis=1).astype(BF16)
    cols = 4 * tq
    n_chunks = S // tk
    assert n_chunks % 2 == 0

    def scores(j, slot):
        start = pl.multiple_of(j * tk, tk)
        st_scr[slot] = jnp.dot(k_ref[pl.ds(start, tk), :], qst, preferred_element_type=F32)

    def softmax_pv(j, slot, m, acc):
        mx = st_scr[slot, 0:8, :]
        for r in range(1, tk // 8):
            mx = jnp.maximum(mx, st_scr[slot, 8 * r:8 * r + 8, :])
        m_new = jnp.maximum(m, jnp.max(mx, axis=0, keepdims=True))
        alpha = jnp.exp2(m - m_new)
        mb = jnp.broadcast_to(m_new, (16, cols))
        for r in range(tk // 16):
            blk = st_scr[slot, 16 * r:16 * r + 16, :]
            pt_scr[slot, 16 * r:16 * r + 16, :] = jnp.exp2((blk - mb).astype(BF16))
        start = pl.multiple_of(j * tk, tk)
        vt = vt_ref[0, :, pl.ds(start, tk)]
        acc = alpha * acc + jnp.dot(vt, pt_scr[slot], preferred_element_type=F32)
        return m_new, acc

    unroll = math.gcd(ATTN_UNROLL, n_chunks)
    assert unroll % 2 == 0

    def body(jj, carry):
        m, acc = carry
        j0 = unroll * jj
        for u in range(unroll):
            scores(jnp.minimum(j0 + u + 1, n_chunks - 1), (u + 1) % 2)
            m, acc = softmax_pv(j0 + u, u % 2, m, acc)
        return m, acc

    scores(0, 0)
    init = (jnp.full((1, cols), -jnp.inf, F32), jnp.zeros((VT_ROWS, cols), F32))
    _, acc = lax.fori_loop(0, n_chunks // unroll, body, init)
    ot = acc[0:HEAD_DIM] / acc[HEAD_DIM:HEAD_DIM + 1]
    for t in range(2):
        pair = jnp.concatenate([ot[:, (2 * t) * tq:(2 * t + 1) * tq],
                                ot[:, (2 * t + 1) * tq:(2 * t + 2) * tq]], axis=0)
        o_ref[:, t * LANES:(t + 1) * LANES] = pair.T.astype(o_ref.dtype)


def _grid_attention(qa, ka2, vat, B, S, tq=256, tk=256):
    N = qa.shape[0]
    tq, tk = min(tq, S), min(tk, S)
    assert S % tq == 0 and S % tk == 0
    nq = S // tq
    return pl.pallas_call(
        functools.partial(_attn_a_kernel, tk=tk),
        grid=(B, N_KV_HEADS_A, nq),
        in_specs=[pl.BlockSpec((tq, 2 * LANES), lambda b, h, i: (b * nq + i, h)),
                  pl.BlockSpec((S, LANES), lambda b, h, i: (b, h)),
                  pl.BlockSpec((1, VT_ROWS, S), lambda b, h, i: (h, 0, b))],
        out_specs=pl.BlockSpec((tq, 2 * LANES), lambda b, h, i: (b * nq + i, h)),
        out_shape=jax.ShapeDtypeStruct((N, QA_W), F32),
        scratch_shapes=[pltpu.VMEM((2, tk, 4 * tq), F32), pltpu.VMEM((2, tk, 4 * tq), BF16)],
        compiler_params=_cparams("parallel", "parallel", "parallel"),
        name="grid_attention",
    )(qa, ka2, vat)


DIL_QB = 128
DIL_R = 64
DIL_KW = DIL_QB + 2 * DIL_R
DIL_SB = 2048
DIL_UNROLL = 16


DIL_WINDOW_OFFS = (-DIL_R, 0, -2 * DIL_R)


def _dilated_kernel(q_ref, k_ref, v_ref, o_ref, acc_scr, m_scr, l_scr, mask_scr, *, seq):
    c = pl.program_id(2)
    lane = lax.broadcasted_iota(jnp.int32, (DIL_QB, LANES), 1)
    lo = lane < HEAD_DIM
    rel = (lax.broadcasted_iota(jnp.int32, (2 * DIL_QB, DIL_KW), 1)
           - lax.broadcasted_iota(jnp.int32, (2 * DIL_QB, DIL_KW), 0) % DIL_QB)
    for n, off in enumerate(DIL_WINDOW_OFFS):
        mask_scr[n] = jnp.where(jnp.abs(rel + off) <= DIL_R, 0.0, -1e30)
    ones = jnp.ones((DIL_KW, LANES), BF16)

    for bi, (window, d) in enumerate(DILATED_BRANCHES[::-1]):
        assert window // (2 * d) == DIL_R
        nb = DIL_SB // (DIL_QB * d)
        n_m = seq // d

        def block(it, _, d=d, nb=nb, n_m=n_m, first=(bi == 0)):
            r = it // nb
            i = it % nb
            row0 = r + d * DIL_QB * i
            m0 = (c * DIL_SB) // d + DIL_QB * i
            ks = jnp.clip(m0 - DIL_R, 0, n_m - DIL_KW)
            q = q_ref[pl.ds(row0, DIL_QB, stride=d), :]
            k = k_ref[pl.ds(r + d * ks, DIL_KW, stride=d), :].astype(BF16)
            v = v_ref[pl.ds(r + d * ks, DIL_KW, stride=d), :].astype(BF16)
            zero = jnp.zeros_like(q)
            qs = jnp.concatenate([jnp.where(lo, q, zero), jnp.where(lo, zero, q)], axis=0).astype(BF16)
            s = lax.dot_general(qs, k, (((1,), (1,)), ((), ())), preferred_element_type=F32)
            off = ks - m0
            s = s + mask_scr[jnp.where(off == DIL_WINDOW_OFFS[0], 0, jnp.where(off == DIL_WINDOW_OFFS[1], 1, 2))]
            mb = jnp.max(s, axis=-1, keepdims=True)
            p = jnp.exp2((s - mb).astype(BF16))
            pv = jnp.dot(p, jnp.concatenate([v, ones], axis=1), preferred_element_type=F32)
            acc_b = jnp.where(lo, pv[0:DIL_QB, 0:LANES], pv[DIL_QB:, 0:LANES])
            m_b = jnp.where(lo, mb[0:DIL_QB], mb[DIL_QB:])
            l_b = jnp.where(lo, pv[0:DIL_QB, LANES:], pv[DIL_QB:, LANES:])
            rows = pl.ds(row0, DIL_QB, stride=d)
            if first:
                acc_scr[rows, :] = acc_b
                m_scr[rows, :] = m_b
                l_scr[rows, :] = l_b
            else:
                m_old = m_scr[rows, :]
                m_new = jnp.maximum(m_old, m_b)
                a_old = jnp.exp2(m_old - m_new)
                a_new = jnp.exp2(m_b - m_new)
                acc_scr[rows, :] = acc_scr[rows, :] * a_old + acc_b * a_new
                l_scr[rows, :] = l_scr[rows, :] * a_old + l_b * a_new
                m_scr[rows, :] = m_new
            return 0

        lax.fori_loop(0, d * nb, block, 0, unroll=DIL_UNROLL)

    o_ref[...] = (acc_scr[...] / l_scr[...]).astype(o_ref.dtype)


def _dilated_attention(qb, kb, vb, B, S):
    N = qb.shape[0]
    assert S % DIL_SB == 0 and S // DILATED_BRANCHES[-1][1] >= DIL_KW
    nsb = S // DIL_SB
    return pl.pallas_call(
        functools.partial(_dilated_kernel, seq=S),
        grid=(B, QB_W // LANES, nsb),
        in_specs=[pl.BlockSpec((DIL_SB, LANES), lambda b, h, i: (b * nsb + i, h)),
                  pl.BlockSpec((S, LANES), lambda b, h, i: (b, h)),
                  pl.BlockSpec((S, LANES), lambda b, h, i: (b, h))],
        out_specs=pl.BlockSpec((DIL_SB, LANES), lambda b, h, i: (b * nsb + i, h)),
        out_shape=jax.ShapeDtypeStruct((N, QB_W), F32),
        scratch_shapes=[pltpu.VMEM((DIL_SB, LANES), F32)] * 3
                       + [pltpu.VMEM((len(DIL_WINDOW_OFFS), 2 * DIL_QB, DIL_KW), F32)],
        compiler_params=_cparams("parallel", "parallel", "parallel"),
        name="dilated_attention",
    )(qb, kb, vb)


SUBLANES = 8


def _store_token_tiles(ref, val):
    rows, d = val.shape
    assert d == SUBLANES * LANES
    for j in range(SUBLANES):
        ref[pl.ds(j, rows, stride=SUBLANES), :] = val[:, j * LANES:(j + 1) * LANES]


def _load_token_tiles(ref, rows, lead=None):
    idx = (lambda j: (pl.ds(j, rows, stride=SUBLANES), slice(None))) if lead is None else (
        lambda j: (lead, pl.ds(j, rows, stride=SUBLANES), slice(None)))
    return jnp.concatenate([ref[idx(j)] for j in range(SUBLANES)], axis=-1)


def _split_bf16(a):
    hi = a.astype(BF16)
    return hi, (a - hi.astype(F32)).astype(BF16)


def _outproj_router_kernel(x_ref, oa_ref, ob_ref, ga_ref, gb_ref, wo_ref, gf_ref, wr_hi_ref, wr_both_ref,
                           br_ref, tri_ref, x1_ref, hf_ref, idx_ref, gate_ref, rank_ref, cnt_ref,
                           carry_scr):
    @pl.when(pl.program_id(0) == 0)
    def _():
        carry_scr[...] = jnp.zeros_like(carry_scr)

    def rms(t, g):
        return t * lax.rsqrt(jnp.mean(t * t, axis=-1, keepdims=True) + EPS) * g

    mix = jnp.concatenate([rms(oa_ref[...], ga_ref[...]), rms(ob_ref[...], gb_ref[...])], axis=-1)
    x1 = x_ref[...] + jnp.dot(mix.astype(BF16), wo_ref[...], preferred_element_type=F32)
    x1_ref[...] = x1
    hf = rms(x1, gf_ref[...])
    _store_token_tiles(hf_ref, hf)

    h_hi, h_lo = _split_bf16(hf)
    nt = (((1,), (1,)), ((), ()))
    both = lax.dot_general(wr_both_ref[...], h_hi, nt, preferred_element_type=F32)
    E = both.shape[0] // 2
    logits = (both[0:E] + both[E:] + lax.dot_general(wr_hi_ref[...], h_lo, nt, preferred_element_type=F32)
              + br_ref[...])
    tm = logits.shape[1]
    eidx = lax.broadcasted_iota(jnp.int32, (E, tm), 0)
    work = logits
    vals, idxs, sel = [], [], jnp.zeros((E, tm), F32)
    for _ in range(TOP_K):
        mx = jnp.max(work, axis=0, keepdims=True)
        first = jnp.min(jnp.where(work == mx, eidx, E), axis=0, keepdims=True)
        hit = eidx == first
        vals.append(mx)
        idxs.append(first)
        sel = jnp.where(hit, 1.0, sel)
        work = jnp.where(hit, -jnp.inf, work)
    ex = [jnp.exp(v - vals[0]) for v in vals]
    den = ex[0] + ex[1] + ex[2] + ex[3]
    gate_ref[...] = jnp.concatenate(ex, axis=0) / den
    idx_ref[...] = jnp.concatenate(idxs, axis=0)

    before = jnp.dot(sel.astype(BF16), tri_ref[...], preferred_element_type=F32) + carry_scr[...]
    ranks = [jnp.sum(jnp.where(eidx == i, before, 0.0), axis=0, keepdims=True) for i in idxs]
    rank_ref[...] = jnp.concatenate(ranks, axis=0).astype(jnp.int32)
    carry_scr[...] = carry_scr[...] + jnp.sum(sel, axis=1, keepdims=True)
    cnt_ref[...] = jnp.broadcast_to(carry_scr[...], cnt_ref.shape).astype(jnp.int32)


def _outproj_router(x2, oa, ob, out_norm_a, out_norm_b, w_out, ffn_norm, w_router, b_router, tm=512):
    N, D = x2.shape
    tm = min(tm, N)
    assert N % tm == 0
    E = w_router.shape[1]
    wr_hi, wr_lo = _split_bf16(w_router.T)
    tri = (jnp.arange(tm)[:, None] < jnp.arange(tm)[None, :]).astype(BF16)
    row = lambda i: (i, 0)
    col = lambda i: (0, i)
    const = lambda i: (0, 0)
    return pl.pallas_call(
        _outproj_router_kernel,
        grid=(N // tm,),
        in_specs=[pl.BlockSpec((tm, D), row), pl.BlockSpec((tm, QA_W), row), pl.BlockSpec((tm, QB_W), row),
                  pl.BlockSpec((1, QA_W), const), pl.BlockSpec((1, QB_W), const),
                  pl.BlockSpec((QA_W + QB_W, D), const), pl.BlockSpec((1, D), const),
                  pl.BlockSpec((E, D), const), pl.BlockSpec((2 * E, D), const), pl.BlockSpec((E, 1), const),
                  pl.BlockSpec((tm, tm), const)],
        out_specs=[pl.BlockSpec((tm, D), row), pl.BlockSpec((tm * SUBLANES, LANES), row),
                   pl.BlockSpec((TOP_K, tm), col), pl.BlockSpec((TOP_K, tm), col),
                   pl.BlockSpec((TOP_K, tm), col), pl.BlockSpec((E, LANES), const)],
        out_shape=[jax.ShapeDtypeStruct((N, D), F32), jax.ShapeDtypeStruct((N * SUBLANES, LANES), F32),
                   jax.ShapeDtypeStruct((TOP_K, N), jnp.int32), jax.ShapeDtypeStruct((TOP_K, N), F32),
                   jax.ShapeDtypeStruct((TOP_K, N), jnp.int32), jax.ShapeDtypeStruct((E, LANES), jnp.int32)],
        scratch_shapes=[pltpu.VMEM((E, 1), F32)],
        compiler_params=_cparams("arbitrary"),
        name="outproj_router",
    )(x2, oa, ob, out_norm_a.reshape(1, -1), out_norm_b.reshape(1, -1), w_out.astype(BF16),
      ffn_norm.reshape(1, D), wr_hi, jnp.concatenate([wr_hi, wr_lo], axis=0), b_router.reshape(E, 1), tri)


EXPERT_ROWS = 256
DISPATCH_TM = 512


def _routing_plan(idx_t, rank_t, counts, n_blocks):
    pad_counts = (counts + EXPERT_ROWS - 1) // EXPERT_ROWS * EXPERT_ROWS
    pad_ends = jnp.cumsum(pad_counts)
    pad_starts = pad_ends - pad_counts
    eids = jnp.arange(N_EXPERTS, dtype=idx_t.dtype)
    dest = rank_t + jnp.sum(jnp.where(idx_t[..., None] == eids, pad_starts, 0), axis=-1)
    blk_start = jnp.arange(n_blocks, dtype=jnp.int32) * EXPERT_ROWS
    block_e = jnp.minimum(jnp.sum((pad_ends[None, :] <= blk_start[:, None]).astype(jnp.int32), axis=1),
                          N_EXPERTS - 1)
    n_used = (pad_ends[-1] // EXPERT_ROWS).astype(jnp.int32).reshape(1)
    nonempty = pad_counts > 0
    w_slot = ((jnp.cumsum(nonempty) - nonempty) % 2).astype(jnp.int32)
    later = (eids[None, :] > eids[:, None]) & nonempty[None, :]
    w_next = jnp.min(jnp.where(later, eids[None, :], N_EXPERTS), axis=1)
    w_next = jnp.where(w_next == N_EXPERTS, -1, w_next).astype(jnp.int32)
    return (dest.astype(jnp.int32), block_e, n_used, pad_starts.astype(jnp.int32), pad_counts.astype(jnp.int32),
            w_slot, w_next)


def _dispatch_kernel(dest_ref, cnt_ref, pstart_ref, pcnt_ref, nu_ref, hf_ref, xs_ref, zero_scr, sem, zsem, *,
                     n_tok):
    i = pl.program_id(0)
    tm = hf_ref.shape[0] // SUBLANES

    def tile(ref, row):
        return ref.at[pl.ds(pl.multiple_of(row * SUBLANES, SUBLANES), SUBLANES)]

    def tok(j, _):
        t = i * tm + j
        for k in range(TOP_K):
            pltpu.make_async_copy(tile(hf_ref, j), tile(xs_ref, dest_ref[k * n_tok + t]), sem).start(
                priority=k % 2)
        return 0
    lax.fori_loop(0, tm, tok, 0, unroll=2)

    @pl.when(i == 0)
    def _():
        zero_scr[...] = jnp.zeros_like(zero_scr)
        blk = EXPERT_ROWS * SUBLANES

        def pad_copy(row):
            return pltpu.make_async_copy(zero_scr.at[pl.ds(0, SUBLANES)], tile(xs_ref, row), zsem)

        def tail_copy(b):
            return pltpu.make_async_copy(zero_scr, xs_ref.at[pl.ds(pl.multiple_of(b * blk, blk), blk)], zsem)

        n_blocks = xs_ref.shape[0] // blk
        lax.fori_loop(nu_ref[0], n_blocks, lambda b, _: (tail_copy(b).start(), 0)[1], 0)
        lax.fori_loop(nu_ref[0], n_blocks, lambda b, _: (tail_copy(b).wait(), 0)[1], 0)

        def expert(e, _):
            base = pstart_ref[e]

            def pad_row(j, _):
                pad_copy(base + j).start()
                return 0
            lax.fori_loop(cnt_ref[e], pcnt_ref[e], pad_row, 0)

            def pad_wait(j, _):
                pad_copy(base + j).wait()
                return 0
            lax.fori_loop(cnt_ref[e], pcnt_ref[e], pad_wait, 0)
            return 0
        lax.fori_loop(0, N_EXPERTS, expert, 0)

    for k in range(TOP_K):
        pltpu.make_async_copy(hf_ref, xs_ref.at[pl.ds(0, tm * SUBLANES)], sem).wait()


def _dispatch(hf, dest, counts, pad_starts, pad_counts, n_used, rows_max):
    N = hf.shape[0] // SUBLANES
    tm = min(DISPATCH_TM, N)
    assert N % tm == 0
    return pl.pallas_call(
        functools.partial(_dispatch_kernel, n_tok=N),
        grid_spec=pltpu.PrefetchScalarGridSpec(
            num_scalar_prefetch=5, grid=(N // tm,),
            in_specs=[pl.BlockSpec((tm * SUBLANES, LANES), lambda i, *_: (i, 0))],
            out_specs=pl.BlockSpec(memory_space=pl.ANY),
            scratch_shapes=[pltpu.VMEM((EXPERT_ROWS * SUBLANES, LANES), F32), pltpu.SemaphoreType.DMA(()),
                            pltpu.SemaphoreType.DMA(())]),
        out_shape=jax.ShapeDtypeStruct((rows_max * SUBLANES, LANES), F32),
        compiler_params=_cparams("arbitrary"),
        name="expert_dispatch",
    )(dest.reshape(-1), counts, pad_starts, pad_counts, n_used, hf)


def _expert_kernel(be_ref, nu_ref, pstart_ref, wslot_ref, wnext_ref, xs_ref, wg_hbm, bg_ref, wu_hbm, bu_ref,
                   wd_hbm, bd_ref, ys_ref, wg_buf, wu_buf, wd_buf, wg_bf, wu_bf, wd_bf, sem):
    i = pl.program_id(0)
    e = be_ref[i]

    def weight_copies(expert, slot):
        return [pltpu.make_async_copy(src.at[expert], dst.at[slot], sem.at[slot, n])
                for n, (src, dst) in enumerate(((wg_hbm, wg_buf), (wu_hbm, wu_buf), (wd_hbm, wd_buf)))]

    @pl.when(i < nu_ref[0])
    def _():
        @pl.when(pstart_ref[e] == i * EXPERT_ROWS)
        def _():
            slot = wslot_ref[e]

            @pl.when(i == 0)
            def _():
                for c in weight_copies(e, slot):
                    c.start()
            for c in weight_copies(e, slot):
                c.wait()

            @pl.when(wnext_ref[e] >= 0)
            def _():
                for c in weight_copies(wnext_ref[e], 1 - slot):
                    c.start()
            wg_bf[...] = wg_buf[slot].astype(BF16)
            wu_bf[...] = wu_buf[slot].astype(BF16)
            wd_bf[...] = wd_buf[slot].astype(BF16)

        xb = _load_token_tiles(xs_ref, EXPERT_ROWS).astype(BF16)
        g = jnp.dot(xb, wg_bf[...], preferred_element_type=F32) + bg_ref[0]
        u = jnp.dot(xb, wu_bf[...], preferred_element_type=F32) + bu_ref[0]
        g = jnp.minimum(g, SWIGLU_LIMIT)
        u = jnp.clip(u, -SWIGLU_LIMIT, SWIGLU_LIMIT)
        act = (u + 1.0) * (g * jax.nn.sigmoid(SWIGLU_ALPHA * g))
        _store_token_tiles(ys_ref, jnp.dot(act.astype(BF16), wd_bf[...], preferred_element_type=F32)
                           + bd_ref[0])

    @pl.when(i >= nu_ref[0])
    def _():
        ys_ref[...] = jnp.zeros_like(ys_ref)


def _experts(xs, block_e, n_used, pad_starts, w_slot, w_next, w_gate, b_gate, w_up, b_up, w_down, b_down):
    E, D, F = w_gate.shape
    blk = EXPERT_ROWS * SUBLANES
    n_blocks = xs.shape[0] // blk
    xmap = lambda i, be, nu, *_: (jnp.minimum(i, nu[0] - 1), 0)
    bmap = lambda i, be, *_: (be[i], 0, 0)
    hbm = pl.BlockSpec(memory_space=pl.ANY)
    return pl.pallas_call(
        _expert_kernel,
        grid_spec=pltpu.PrefetchScalarGridSpec(
            num_scalar_prefetch=5, grid=(n_blocks,),
            in_specs=[pl.BlockSpec((blk, LANES), xmap),
                      hbm, pl.BlockSpec((1, 1, F), bmap),
                      hbm, pl.BlockSpec((1, 1, F), bmap),
                      hbm, pl.BlockSpec((1, 1, D), bmap)],
            out_specs=pl.BlockSpec((blk, LANES), lambda i, *_: (i, 0)),
            scratch_shapes=[pltpu.VMEM((2, D, F), F32), pltpu.VMEM((2, D, F), F32), pltpu.VMEM((2, F, D), F32),
                            pltpu.VMEM((D, F), BF16), pltpu.VMEM((D, F), BF16), pltpu.VMEM((F, D), BF16),
                            pltpu.SemaphoreType.DMA((2, 3))]),
        out_shape=jax.ShapeDtypeStruct(xs.shape, F32),
        compiler_params=_cparams("arbitrary"),
        name="expert_ffn",
    )(block_e, n_used, pad_starts, w_slot, w_next, xs, w_gate, b_gate.reshape(E, 1, F), w_up,
      b_up.reshape(E, 1, F), w_down, b_down.reshape(E, 1, D))


COMBINE_TM = 512


def _combine_kernel(dest_ref, x1_ref, gate_ref, fn_ref, ys_ref, o_ref, buf, sem, *, n_tok):
    i = pl.program_id(0)
    tm = x1_ref.shape[0]
    slot = i % 2

    def tile(row):
        return pl.ds(pl.multiple_of(row * SUBLANES, SUBLANES), SUBLANES)

    def gather(step, to_slot):
        def issue(j, _):
            t = step * tm + j
            for k in range(TOP_K):
                pltpu.make_async_copy(ys_ref.at[tile(dest_ref[k * n_tok + t])],
                                      buf.at[to_slot * TOP_K + k, tile(j)], sem.at[to_slot]).start(priority=k % 2)
            return 0
        lax.fori_loop(0, tm, issue, 0, unroll=2)

    @pl.when(i == 0)
    def _():
        gather(0, 0)

    @pl.when(i + 1 < pl.num_programs(0))
    def _():
        gather(i + 1, 1 - slot)

    for k in range(TOP_K):
        pltpu.make_async_copy(ys_ref.at[pl.ds(0, tm * SUBLANES)], buf.at[slot * TOP_K + k], sem.at[slot]).wait()

    gates = gate_ref[...]
    y = x1_ref[...]
    for k in range(TOP_K):
        y = y + _load_token_tiles(buf, tm, lead=slot * TOP_K + k) * gates[:, k:k + 1]
    ms = jnp.mean(y * y, axis=-1, keepdims=True)
    o_ref[...] = y * lax.rsqrt(ms + EPS) * fn_ref[...]


def _combine(ys, dest, x1, gates_t, final_norm):
    N, D = x1.shape
    tm = min(COMBINE_TM, N)
    assert N % tm == 0
    return pl.pallas_call(
        functools.partial(_combine_kernel, n_tok=N),
        grid_spec=pltpu.PrefetchScalarGridSpec(
            num_scalar_prefetch=1, grid=(N // tm,),
            in_specs=[pl.BlockSpec((tm, D), lambda i, d: (i, 0)),
                      pl.BlockSpec((tm, TOP_K), lambda i, d: (i, 0)),
                      pl.BlockSpec((1, D), lambda i, d: (0, 0)),
                      pl.BlockSpec(memory_space=pl.ANY)],
            out_specs=pl.BlockSpec((tm, D), lambda i, d: (i, 0)),
            scratch_shapes=[pltpu.VMEM((2 * TOP_K, tm * SUBLANES, LANES), F32),
                            pltpu.SemaphoreType.DMA((2,))]),
        out_shape=jax.ShapeDtypeStruct((N, D), F32),
        compiler_params=_cparams("arbitrary"),
        name="expert_combine",
    )(dest.reshape(-1), x1, gates_t.T, final_norm.reshape(1, D), ys)


def kernel(x, attn_norm, w_in, q_norm, k_norm, out_norm_a, out_norm_b, w_out, ffn_norm, w_router,
           b_router, w_gate, b_gate, w_up, b_up, w_down, b_down, final_norm):
    B, S, D = x.shape
    x2 = x.reshape(B * S, D)
    qa, ka2, va2, qb, kb, vb = _input_projection(x2, attn_norm[0], w_in[0], q_norm[0], k_norm[0], S)
    oa = _grid_attention(qa, ka2, va2, B, S)
    ob = _dilated_attention(qb, kb, vb, B, S)
    x1, hf, idx_t, gates_t, rank_t, cnt = _outproj_router(
        x2, oa, ob, out_norm_a[0], out_norm_b[0], w_out[0], ffn_norm[0], w_router[0], b_router[0])
    N = B * S
    n_blocks = (N * TOP_K) // EXPERT_ROWS + N_EXPERTS
    counts = cnt[:, 0]
    dest, block_e, n_used, pad_starts, pad_counts, w_slot, w_next = _routing_plan(
        idx_t, rank_t, counts, n_blocks)
    xs = _dispatch(hf, dest, counts, pad_starts, pad_counts, n_used, n_blocks * EXPERT_ROWS)
    ys = _experts(xs, block_e, n_used, pad_starts, w_slot, w_next,
                  w_gate[0], b_gate[0], w_up[0], b_up[0], w_down[0], b_down[0])
    out = _combine(ys, dest, x1, gates_t, final_norm)
    return out.reshape(B, S, D)
```

```python
import functools
import math

import jax
import jax.numpy as jnp
from jax import lax
from jax.experimental import pallas as pl
from jax.experimental.pallas import tpu as pltpu

F32 = jnp.float32
BF16 = jnp.bfloat16

HEAD_DIM = 64
N_HEADS_A = 8
N_KV_HEADS_A = 2
N_HEADS_B = 8
DILATED_BRANCHES = ((128, 1), (512, 4), (2048, 16))
GRID_W = 64
ROPE_THETA = 10000.0
N_EXPERTS = 32
TOP_K = 4
SWIGLU_LIMIT = 7.0
SWIGLU_ALPHA = 1.702
EPS = 1e-6

LANES = 128
QA_W = N_HEADS_A * HEAD_DIM
KA_W = N_KV_HEADS_A * HEAD_DIM
QB_W = N_HEADS_B * HEAD_DIM
VMEM_LIMIT = 56 * 1024 * 1024


def _cparams(*sem):
    return pltpu.CompilerParams(dimension_semantics=sem, vmem_limit_bytes=VMEM_LIMIT)


def _rope_tables(S, tm):
    assert tm % GRID_W == 0 and S % tm == 0
    lane = jnp.arange(LANES)
    i = lane % HEAD_DIM
    half = HEAD_DIM // 2
    t0 = (jnp.arange(S // tm) * tm).astype(F32)[:, None]
    j = jnp.arange(tm).astype(F32)[:, None]
    inv_a = (ROPE_THETA ** (-jnp.arange(0, half, 2, dtype=F32) / half))[i % (half // 2)][None, :]
    is_row = (i < half)[None, :]
    base_a = jnp.where(is_row, jnp.floor(t0 / GRID_W) * inv_a, 0.0)
    jrow = jnp.floor(j / GRID_W)
    offs_a = jnp.where(is_row, jrow, j - jrow * GRID_W) * inv_a
    sgn_a = jnp.where((i // (half // 2)) % 2 == 0, -1.0, 1.0)[None, :].astype(F32)
    inv_b = (ROPE_THETA ** (-jnp.arange(0, HEAD_DIM, 2, dtype=F32) / HEAD_DIM))[i % half][None, :]
    base_b = t0 * inv_b
    offs_b = j * inv_b
    sgn_b = jnp.where(i < half, -1.0, 1.0)[None, :].astype(F32)
    cs = lambda a: jnp.concatenate([jnp.cos(a), jnp.sin(a)], axis=-1)
    return (cs(base_a)[:, None, :], cs(offs_a), sgn_a), (cs(base_b)[:, None, :], cs(offs_b), sgn_b)


def _rope_cos_sin(base_ref, offs_ref, sgn_ref):
    cb, sb = base_ref[0, :, 0:LANES], base_ref[0, :, LANES:2 * LANES]
    co, so = offs_ref[:, 0:LANES], offs_ref[:, LANES:2 * LANES]
    return cb * co - sb * so, (sb * co + cb * so) * sgn_ref[...]


def _rope_tile(x, cos, sin_signed, half):
    lane = lax.broadcasted_iota(jnp.int32, x.shape, 1)
    first = (lane % (2 * half)) < half
    partner = jnp.where(first, pltpu.roll(x, LANES - half, 1), pltpu.roll(x, half, 1))
    return x * cos + partner * sin_signed


def _head_rms(x, seg_mean, gain):
    ms = jnp.dot((x * x).astype(BF16), seg_mean, preferred_element_type=F32)
    return x * lax.rsqrt(ms + EPS) * gain


def _inproj_kernel(x_ref, g_ref, w_ref, qn_ref, kn_ref, seg_ref, base_a_ref, offs_a_ref, sgn_a_ref,
                   base_b_ref, offs_b_ref, sgn_b_ref, qa_ref, ka_ref, va_ref, qb_ref, kb_ref, vb_ref):
    x = x_ref[...]
    ms = jnp.mean(x * x, axis=-1, keepdims=True)
    hn = (x * lax.rsqrt(ms + EPS) * g_ref[...]).astype(BF16)
    proj = jnp.dot(hn, w_ref[...], preferred_element_type=F32)
    seg = seg_ref[...]
    cosa, sina = _rope_cos_sin(base_a_ref, offs_a_ref, sgn_a_ref)
    cosb, sinb = _rope_cos_sin(base_b_ref, offs_b_ref, sgn_b_ref)
    scale = HEAD_DIM ** -0.5
    lane = lax.broadcasted_iota(jnp.int32, (x.shape[0], LANES), 1)
    lo = lane < HEAD_DIM

    off = 0
    for j in range(QA_W // LANES):
        t = proj[:, off + j * LANES: off + (j + 1) * LANES]
        t = _rope_tile(_head_rms(t, seg, qn_ref[...]), cosa, sina, HEAD_DIM // 4)
        qa_ref[:, j * LANES:(j + 1) * LANES] = (t * (scale * LOG2E)).astype(qa_ref.dtype)
    off += QA_W
    k = _rope_tile(_head_rms(proj[:, off:off + LANES], seg, kn_ref[...]), cosa, sina, HEAD_DIM // 4)
    off += KA_W
    v = proj[:, off:off + LANES]
    off += KA_W
    sw = pltpu.roll(k, HEAD_DIM, 1)
    ka_ref[:, 0:LANES] = jnp.where(lo, k, sw).astype(ka_ref.dtype)
    ka_ref[:, LANES:2 * LANES] = jnp.where(lo, sw, k).astype(ka_ref.dtype)
    vt = v.T
    extra = (lax.broadcasted_iota(jnp.int32, (VT_ROWS - HEAD_DIM, v.shape[0]), 0) == 0).astype(F32)
    va_ref[0] = jnp.concatenate([vt[0:HEAD_DIM], extra], axis=0).astype(va_ref.dtype)
    va_ref[1] = jnp.concatenate([vt[HEAD_DIM:2 * HEAD_DIM], extra], axis=0).astype(va_ref.dtype)
    for j in range(QB_W // LANES):
        t = proj[:, off + j * LANES: off + (j + 1) * LANES]
        qb_ref[:, j * LANES:(j + 1) * LANES] = (
            _rope_tile(t, cosb, sinb, HEAD_DIM // 2) * (scale * LOG2E)).astype(qb_ref.dtype)
    off += QB_W
    for j in range(QB_W // LANES):
        t = proj[:, off + j * LANES: off + (j + 1) * LANES]
        kb_ref[:, j * LANES:(j + 1) * LANES] = _rope_tile(t, cosb, sinb, HEAD_DIM // 2).astype(kb_ref.dtype)
    off += QB_W
    vb_ref[...] = proj[:, off:off + QB_W].astype(vb_ref.dtype)


def _input_projection(x2, attn_norm, w_in, q_norm, k_norm, S, tm=512):
    N, D = x2.shape
    tm = min(tm, S)
    assert S % tm == 0 and N % S == 0
    n_s = S // tm
    rope_a, rope_b = _rope_tables(S, tm)
    seg = jnp.kron(jnp.eye(2, dtype=F32), jnp.full((HEAD_DIM, HEAD_DIM), 1.0 / HEAD_DIM, F32)).astype(BF16)
    two = lambda g: jnp.tile(g.reshape(1, HEAD_DIM), (1, 2))
    row = lambda i: (i, 0)
    const = lambda i: (0, 0)
    rope_specs = [pl.BlockSpec((1, 1, 2 * LANES), lambda i: (i % n_s, 0, 0)),
                  pl.BlockSpec((tm, 2 * LANES), const), pl.BlockSpec((1, LANES), const)]
    w = w_in.astype(BF16)
    out_w = (QA_W, 2 * LANES, None, QB_W, QB_W, QB_W)
    out_dt = (BF16, BF16, BF16, F32, F32, F32)
    vt_spec = pl.BlockSpec((N_KV_HEADS_A, VT_ROWS, tm), lambda i: (0, 0, i))
    vt_shape = jax.ShapeDtypeStruct((N_KV_HEADS_A, VT_ROWS, N), BF16)
    return pl.pallas_call(
        _inproj_kernel,
        grid=(N // tm,),
        in_specs=[pl.BlockSpec((tm, D), row), pl.BlockSpec((1, D), const),
                  pl.BlockSpec(w.shape, const), pl.BlockSpec((1, LANES), const),
                  pl.BlockSpec((1, LANES), const), pl.BlockSpec((LANES, LANES), const)]
                 + rope_specs + rope_specs,
        out_specs=[vt_spec if wd is None else pl.BlockSpec((tm, wd), row) for wd in out_w],
        out_shape=[vt_shape if wd is None else jax.ShapeDtypeStruct((N, wd), dt)
                   for wd, dt in zip(out_w, out_dt)],
        compiler_params=_cparams("parallel"),
        name="input_projection",
    )(x2, attn_norm.reshape(1, D), w, two(q_norm), two(k_norm), seg, *rope_a, *rope_b)


VT_ROWS = 80
LOG2E = 1.4426950408889634
ATTN_UNROLL = 32


def _attn_a_kernel(q_ref, k_ref, vt_ref, o_ref, st_scr, pt_scr, *, tk):
    tq = q_ref.shape[0]
    S = k_ref.shape[0]
    sub = lax.broadcasted_iota(jnp.int32, (LANES, tq), 0)
    lo = sub < HEAD_DIM
    q0 = q_ref[:, 0:LANES].astype(F32).T
    q1 = q_ref[:, LANES:2 * LANES].astype(F32).T
    zero = jnp.zeros_like(q0)
    qst = jnp.concatenate([jnp.where(lo, q0, zero), jnp.where(lo, zero, q0),
                           jnp.where(lo, q1, zero), jnp.where(lo, zero, q1)], axis=1).astype(BF16)
    cols = 4 * tq
    n_chunks = S // tk
    assert n_chunks % 2 == 0

    def scores(j, slot):
        start = pl.multiple_of(j * tk, tk)
        st_scr[slot] = jnp.dot(k_ref[pl.ds(start, tk), :], qst, preferred_element_type=F32)

    def softmax_pv(j, slot, m, acc):
        mx = st_scr[slot, 0:8, :]
        for r in range(1, tk // 8):
            mx = jnp.maximum(mx, st_scr[slot, 8 * r:8 * r + 8, :])
        m_new = jnp.maximum(m, jnp.max(mx, axis=0, keepdims=True))
        alpha = jnp.exp2(m - m_new)
        mb = jnp.broadcast_to(m_new, (16, cols))
        for r in range(tk // 16):
            blk = st_scr[slot, 16 * r:16 * r + 16, :]
            pt_scr[slot, 16 * r:16 * r + 16, :] = jnp.exp2((blk - mb).astype(BF16))
        start = pl.multiple_of(j * tk, tk)
        vt = vt_ref[0, :, pl.ds(start, tk)]
        acc = alpha * acc + jnp.dot(vt, pt_scr[slot], preferred_element_type=F32)
        return m_new, acc

    unroll = math.gcd(ATTN_UNROLL, n_chunks)
    assert unroll % 2 == 0

    def body(jj, carry):
        m, acc = carry
        j0 = unroll * jj
        for u in range(unroll):
            scores(jnp.minimum(j0 + u + 1, n_chunks - 1), (u + 1) % 2)
            m, acc = softmax_pv(j0 + u, u % 2, m, acc)
        return m, acc

    scores(0, 0)
    init = (jnp.full((1, cols), -jnp.inf, F32), jnp.zeros((VT_ROWS, cols), F32))
    _, acc = lax.fori_loop(0, n_chunks // unroll, body, init)
    ot = acc[0:HEAD_DIM] / acc[HEAD_DIM:HEAD_DIM + 1]
    for t in range(2):
        pair = jnp.concatenate([ot[:, (2 * t) * tq:(2 * t + 1) * tq],
                                ot[:, (2 * t + 1) * tq:(2 * t + 2) * tq]], axis=0)
        o_ref[:, t * LANES:(t + 1) * LANES] = pair.T.astype(o_ref.dtype)


def _grid_attention(qa, ka2, vat, B, S, tq=256, tk=256):
    N = qa.shape[0]
    tq, tk = min(tq, S), min(tk, S)
    assert S % tq == 0 and S % tk == 0
    nq = S // tq
    return pl.pallas_call(
        functools.partial(_attn_a_kernel, tk=tk),
        grid=(B, N_KV_HEADS_A, nq),
        in_specs=[pl.BlockSpec((tq, 2 * LANES), lambda b, h, i: (b * nq + i, h)),
                  pl.BlockSpec((S, LANES), lambda b, h, i: (b, h)),
                  pl.BlockSpec((1, VT_ROWS, S), lambda b, h, i: (h, 0, b))],
        out_specs=pl.BlockSpec((tq, 2 * LANES), lambda b, h, i: (b * nq + i, h)),
        out_shape=jax.ShapeDtypeStruct((N, QA_W), F32),
        scratch_shapes=[pltpu.VMEM((2, tk, 4 * tq), F32), pltpu.VMEM((2, tk, 4 * tq), BF16)],
        compiler_params=_cparams("parallel", "parallel", "parallel"),
        name="grid_attention",
    )(qa, ka2, vat)


DIL_QB = 128
DIL_R = 64
DIL_KW = DIL_QB + 2 * DIL_R
DIL_SB = 2048
DIL_UNROLL = 16


DIL_WINDOW_OFFS = (-DIL_R, 0, -2 * DIL_R)


def _dilated_kernel(q_ref, k_ref, v_ref, o_ref, acc_scr, m_scr, l_scr, mask_scr, *, seq):
    c = pl.program_id(2)
    lane = lax.broadcasted_iota(jnp.int32, (DIL_QB, LANES), 1)
    lo = lane < HEAD_DIM
    rel = (lax.broadcasted_iota(jnp.int32, (2 * DIL_QB, DIL_KW), 1)
           - lax.broadcasted_iota(jnp.int32, (2 * DIL_QB, DIL_KW), 0) % DIL_QB)
    for n, off in enumerate(DIL_WINDOW_OFFS):
        mask_scr[n] = jnp.where(jnp.abs(rel + off) <= DIL_R, 0.0, -1e30)
    ones = jnp.ones((DIL_KW, LANES), BF16)

    for bi, (window, d) in enumerate(DILATED_BRANCHES[::-1]):
        assert window // (2 * d) == DIL_R
        nb = DIL_SB // (DIL_QB * d)
        n_m = seq // d

        def block(it, _, d=d, nb=nb, n_m=n_m, first=(bi == 0)):
            r = it // nb
            i = it % nb
            row0 = r + d * DIL_QB * i
            m0 = (c * DIL_SB) // d + DIL_QB * i
            ks = jnp.clip(m0 - DIL_R, 0, n_m - DIL_KW)
            q = q_ref[pl.ds(row0, DIL_QB, stride=d), :]
            k = k_ref[pl.ds(r + d * ks, DIL_KW, stride=d), :].astype(BF16)
            v = v_ref[pl.ds(r + d * ks, DIL_KW, stride=d), :].astype(BF16)
            zero = jnp.zeros_like(q)
            qs = jnp.concatenate([jnp.where(lo, q, zero), jnp.where(lo, zero, q)], axis=0).astype(BF16)
            s = lax.dot_general(qs, k, (((1,), (1,)), ((), ())), preferred_element_type=F32)
            off = ks - m0
            s = s + mask_scr[jnp.where(off == DIL_WINDOW_OFFS[0], 0, jnp.where(off == DIL_WINDOW_OFFS[1], 1, 2))]
            mb = jnp.max(s, axis=-1, keepdims=True)
            p = jnp.exp2((s - mb).astype(BF16))
            pv = jnp.dot(p, jnp.concatenate([v, ones], axis=1), preferred_element_type=F32)
            acc_b = jnp.where(lo, pv[0:DIL_QB, 0:LANES], pv[DIL_QB:, 0:LANES])
            m_b = jnp.where(lo, mb[0:DIL_QB], mb[DIL_QB:])
            l_b = jnp.where(lo, pv[0:DIL_QB, LANES:], pv[DIL_QB:, LANES:])
            rows = pl.ds(row0, DIL_QB, stride=d)
            if first:
                acc_scr[rows, :] = acc_b
                m_scr[rows, :] = m_b
                l_scr[rows, :] = l_b
            else:
                m_old = m_scr[rows, :]
                m_new = jnp.maximum(m_old, m_b)
                a_old = jnp.exp2(m_old - m_new)
                a_new = jnp.exp2(m_b - m_new)
                acc_scr[rows, :] = acc_scr[rows, :] * a_old + acc_b * a_new
                l_scr[rows, :] = l_scr[rows, :] * a_old + l_b * a_new
                m_scr[rows, :] = m_new
            return 0

        lax.fori_loop(0, d * nb, block, 0, unroll=DIL_UNROLL)

    o_ref[...] = (acc_scr[...] / l_scr[...]).astype(o_ref.dtype)


def _dilated_attention(qb, kb, vb, B, S):
    N = qb.shape[0]
    assert S % DIL_SB == 0 and S // DILATED_BRANCHES[-1][1] >= DIL_KW
    nsb = S // DIL_SB
    return pl.pallas_call(
        functools.partial(_dilated_kernel, seq=S),
        grid=(B, QB_W // LANES, nsb),
        in_specs=[pl.BlockSpec((DIL_SB, LANES), lambda b, h, i: (b * nsb + i, h)),
                  pl.BlockSpec((S, LANES), lambda b, h, i: (b, h)),
                  pl.BlockSpec((S, LANES), lambda b, h, i: (b, h))],
        out_specs=pl.BlockSpec((DIL_SB, LANES), lambda b, h, i: (b * nsb + i, h)),
        out_shape=jax.ShapeDtypeStruct((N, QB_W), F32),
        scratch_shapes=[pltpu.VMEM((DIL_SB, LANES), F32)] * 3
                       + [pltpu.VMEM((len(DIL_WINDOW_OFFS), 2 * DIL_QB, DIL_KW), F32)],
        compiler_params=_cparams("parallel", "parallel", "parallel"),
        name="dilated_attention",
    )(qb, kb, vb)


SUBLANES = 8


def _store_token_tiles(ref, val):
    rows, d = val.shape
    assert d == SUBLANES * LANES
    for j in range(SUBLANES):
        ref[pl.ds(j, rows, stride=SUBLANES), :] = val[:, j * LANES:(j + 1) * LANES]


def _load_token_tiles(ref, rows, lead=None):
    idx = (lambda j: (pl.ds(j, rows, stride=SUBLANES), slice(None))) if lead is None else (
        lambda j: (lead, pl.ds(j, rows, stride=SUBLANES), slice(None)))
    return jnp.concatenate([ref[idx(j)] for j in range(SUBLANES)], axis=-1)


def _split_bf16(a):
    hi = a.astype(BF16)
    return hi, (a - hi.astype(F32)).astype(BF16)


def _outproj_router_kernel(x_ref, oa_ref, ob_ref, ga_ref, gb_ref, wo_ref, gf_ref, wr_hi_ref, wr_both_ref,
                           br_ref, tri_ref, x1_ref, hf_ref, idx_ref, gate_ref, rank_ref, cnt_ref,
                           carry_scr):
    @pl.when(pl.program_id(0) == 0)
    def _():
        carry_scr[...] = jnp.zeros_like(carry_scr)

    def rms(t, g):
        return t * lax.rsqrt(jnp.mean(t * t, axis=-1, keepdims=True) + EPS) * g

    mix = jnp.concatenate([rms(oa_ref[...], ga_ref[...]), rms(ob_ref[...], gb_ref[...])], axis=-1)
    x1 = x_ref[...] + jnp.dot(mix.astype(BF16), wo_ref[...], preferred_element_type=F32)
    x1_ref[...] = x1
    hf = rms(x1, gf_ref[...])
    _store_token_tiles(hf_ref, hf)

    h_hi, h_lo = _split_bf16(hf)
    nt = (((1,), (1,)), ((), ()))
    both = lax.dot_general(wr_both_ref[...], h_hi, nt, preferred_element_type=F32)
    E = both.shape[0] // 2
    logits = (both[0:E] + both[E:] + lax.dot_general(wr_hi_ref[...], h_lo, nt, preferred_element_type=F32)
              + br_ref[...])
    tm = logits.shape[1]
    eidx = lax.broadcasted_iota(jnp.int32, (E, tm), 0)
    work = logits
    vals, idxs, sel = [], [], jnp.zeros((E, tm), F32)
    for _ in range(TOP_K):
        mx = jnp.max(work, axis=0, keepdims=True)
        first = jnp.min(jnp.where(work == mx, eidx, E), axis=0, keepdims=True)
        hit = eidx == first
        vals.append(mx)
        idxs.append(first)
        sel = jnp.where(hit, 1.0, sel)
        work = jnp.where(hit, -jnp.inf, work)
    ex = [jnp.exp(v - vals[0]) for v in vals]
    den = ex[0] + ex[1] + ex[2] + ex[3]
    gate_ref[...] = jnp.concatenate(ex, axis=0) / den
    idx_ref[...] = jnp.concatenate(idxs, axis=0)

    before = jnp.dot(sel.astype(BF16), tri_ref[...], preferred_element_type=F32) + carry_scr[...]
    ranks = [jnp.sum(jnp.where(eidx == i, before, 0.0), axis=0, keepdims=True) for i in idxs]
    rank_ref[...] = jnp.concatenate(ranks, axis=0).astype(jnp.int32)
    carry_scr[...] = carry_scr[...] + jnp.sum(sel, axis=1, keepdims=True)
    cnt_ref[...] = jnp.broadcast_to(carry_scr[...], cnt_ref.shape).astype(jnp.int32)


def _outproj_router(x2, oa, ob, out_norm_a, out_norm_b, w_out, ffn_norm, w_router, b_router, tm=512):
    N, D = x2.shape
    tm = min(tm, N)
    assert N % tm == 0
    E = w_router.shape[1]
    wr_hi, wr_lo = _split_bf16(w_router.T)
    tri = (jnp.arange(tm)[:, None] < jnp.arange(tm)[None, :]).astype(BF16)
    row = lambda i: (i, 0)
    col = lambda i: (0, i)
    const = lambda i: (0, 0)
    return pl.pallas_call(
        _outproj_router_kernel,
        grid=(N // tm,),
        in_specs=[pl.BlockSpec((tm, D), row), pl.BlockSpec((tm, QA_W), row), pl.BlockSpec((tm, QB_W), row),
                  pl.BlockSpec((1, QA_W), const), pl.BlockSpec((1, QB_W), const),
                  pl.BlockSpec((QA_W + QB_W, D), const), pl.BlockSpec((1, D), const),
                  pl.BlockSpec((E, D), const), pl.BlockSpec((2 * E, D), const), pl.BlockSpec((E, 1), const),
                  pl.BlockSpec((tm, tm), const)],
        out_specs=[pl.BlockSpec((tm, D), row), pl.BlockSpec((tm * SUBLANES, LANES), row),
                   pl.BlockSpec((TOP_K, tm), col), pl.BlockSpec((TOP_K, tm), col),
                   pl.BlockSpec((TOP_K, tm), col), pl.BlockSpec((E, LANES), const)],
        out_shape=[jax.ShapeDtypeStruct((N, D), F32), jax.ShapeDtypeStruct((N * SUBLANES, LANES), F32),
                   jax.ShapeDtypeStruct((TOP_K, N), jnp.int32), jax.ShapeDtypeStruct((TOP_K, N), F32),
                   jax.ShapeDtypeStruct((TOP_K, N), jnp.int32), jax.ShapeDtypeStruct((E, LANES), jnp.int32)],
        scratch_shapes=[pltpu.VMEM((E, 1), F32)],
        compiler_params=_cparams("arbitrary"),
        name="outproj_router",
    )(x2, oa, ob, out_norm_a.reshape(1, -1), out_norm_b.reshape(1, -1), w_out.astype(BF16),
      ffn_norm.reshape(1, D), wr_hi, jnp.concatenate([wr_hi, wr_lo], axis=0), b_router.reshape(E, 1), tri)


EXPERT_ROWS = 256
DISPATCH_TM = 512


def _routing_plan(idx_t, rank_t, counts, n_blocks):
    pad_counts = (counts + EXPERT_ROWS - 1) // EXPERT_ROWS * EXPERT_ROWS
    pad_ends = jnp.cumsum(pad_counts)
    pad_starts = pad_ends - pad_counts
    eids = jnp.arange(N_EXPERTS, dtype=idx_t.dtype)
    dest = rank_t + jnp.sum(jnp.where(idx_t[..., None] == eids, pad_starts, 0), axis=-1)
    blk_start = jnp.arange(n_blocks, dtype=jnp.int32) * EXPERT_ROWS
    block_e = jnp.minimum(jnp.sum((pad_ends[None, :] <= blk_start[:, None]).astype(jnp.int32), axis=1),
                          N_EXPERTS - 1)
    n_used = (pad_ends[-1] // EXPERT_ROWS).astype(jnp.int32).reshape(1)
    nonempty = pad_counts > 0
    w_slot = ((jnp.cumsum(nonempty) - nonempty) % 2).astype(jnp.int32)
    later = (eids[None, :] > eids[:, None]) & nonempty[None, :]
    w_next = jnp.min(jnp.where(later, eids[None, :], N_EXPERTS), axis=1)
    w_next = jnp.where(w_next == N_EXPERTS, -1, w_next).astype(jnp.int32)
    return (dest.astype(jnp.int32), block_e, n_used, pad_starts.astype(jnp.int32), pad_counts.astype(jnp.int32),
            w_slot, w_next)


def _dispatch_kernel(dest_ref, cnt_ref, pstart_ref, pcnt_ref, nu_ref, hf_ref, xs_ref, zero_scr, sem, zsem, *,
                     n_tok):
    i = pl.program_id(0)
    tm = hf_ref.shape[0] // SUBLANES

    def tile(ref, row):
        return ref.at[pl.ds(pl.multiple_of(row * SUBLANES, SUBLANES), SUBLANES)]

    def tok(j, _):
        t = i * tm + j
        for k in range(TOP_K):
            pltpu.make_async_copy(tile(hf_ref, j), tile(xs_ref, dest_ref[k * n_tok + t]), sem).start(
                priority=k % 2)
        return 0
    lax.fori_loop(0, tm, tok, 0, unroll=2)

    @pl.when(i == 0)
    def _():
        zero_scr[...] = jnp.zeros_like(zero_scr)
        blk = EXPERT_ROWS * SUBLANES

        def pad_copy(row):
            return pltpu.make_async_copy(zero_scr.at[pl.ds(0, SUBLANES)], tile(xs_ref, row), zsem)

        def tail_copy(b):
            return pltpu.make_async_copy(zero_scr, xs_ref.at[pl.ds(pl.multiple_of(b * blk, blk), blk)], zsem)

        n_blocks = xs_ref.shape[0] // blk
        lax.fori_loop(nu_ref[0], n_blocks, lambda b, _: (tail_copy(b).start(), 0)[1], 0)
        lax.fori_loop(nu_ref[0], n_blocks, lambda b, _: (tail_copy(b).wait(), 0)[1], 0)

        def expert(e, _):
            base = pstart_ref[e]

            def pad_row(j, _):
                pad_copy(base + j).start()
                return 0
            lax.fori_loop(cnt_ref[e], pcnt_ref[e], pad_row, 0)

            def pad_wait(j, _):
                pad_copy(base + j).wait()
                return 0
            lax.fori_loop(cnt_ref[e], pcnt_ref[e], pad_wait, 0)
            return 0
        lax.fori_loop(0, N_EXPERTS, expert, 0)

    for k in range(TOP_K):
        pltpu.make_async_copy(hf_ref, xs_ref.at[pl.ds(0, tm * SUBLANES)], sem).wait()


def _dispatch(hf, dest, counts, pad_starts, pad_counts, n_used, rows_max):
    N = hf.shape[0] // SUBLANES
    tm = min(DISPATCH_TM, N)
    assert N % tm == 0
    return pl.pallas_call(
        functools.partial(_dispatch_kernel, n_tok=N),
        grid_spec=pltpu.PrefetchScalarGridSpec(
            num_scalar_prefetch=5, grid=(N // tm,),
            in_specs=[pl.BlockSpec((tm * SUBLANES, LANES), lambda i, *_: (i, 0))],
            out_specs=pl.BlockSpec(memory_space=pl.ANY),
            scratch_shapes=[pltpu.VMEM((EXPERT_ROWS * SUBLANES, LANES), F32), pltpu.SemaphoreType.DMA(()),
                            pltpu.SemaphoreType.DMA(())]),
        out_shape=jax.ShapeDtypeStruct((rows_max * SUBLANES, LANES), F32),
        compiler_params=_cparams("arbitrary"),
        name="expert_dispatch",
    )(dest.reshape(-1), counts, pad_starts, pad_counts, n_used, hf)


def _expert_kernel(be_ref, nu_ref, pstart_ref, wslot_ref, wnext_ref, xs_ref, wg_hbm, bg_ref, wu_hbm, bu_ref,
                   wd_hbm, bd_ref, ys_ref, wg_buf, wu_buf, wd_buf, wg_bf, wu_bf, wd_bf, sem):
    i = pl.program_id(0)
    e = be_ref[i]

    def weight_copies(expert, slot):
        return [pltpu.make_async_copy(src.at[expert], dst.at[slot], sem.at[slot, n])
                for n, (src, dst) in enumerate(((wg_hbm, wg_buf), (wu_hbm, wu_buf), (wd_hbm, wd_buf)))]

    @pl.when(i < nu_ref[0])
    def _():
        @pl.when(pstart_ref[e] == i * EXPERT_ROWS)
        def _():
            slot = wslot_ref[e]

            @pl.when(i == 0)
            def _():
                for c in weight_copies(e, slot):
                    c.start()
            for c in weight_copies(e, slot):
                c.wait()

            @pl.when(wnext_ref[e] >= 0)
            def _():
                for c in weight_copies(wnext_ref[e], 1 - slot):
                    c.start(priority=1)
            wg_bf[...] = wg_buf[slot].astype(BF16)
            wu_bf[...] = wu_buf[slot].astype(BF16)
            wd_bf[...] = wd_buf[slot].astype(BF16)

        xb = _load_token_tiles(xs_ref, EXPERT_ROWS).astype(BF16)
        g = jnp.dot(xb, wg_bf[...], preferred_element_type=F32) + bg_ref[0]
        u = jnp.dot(xb, wu_bf[...], preferred_element_type=F32) + bu_ref[0]
        g = jnp.minimum(g, SWIGLU_LIMIT)
        u = jnp.clip(u, -SWIGLU_LIMIT, SWIGLU_LIMIT)
        act = (u + 1.0) * (g * jax.nn.sigmoid(SWIGLU_ALPHA * g))
        _store_token_tiles(ys_ref, jnp.dot(act.astype(BF16), wd_bf[...], preferred_element_type=F32)
                           + bd_ref[0])

    @pl.when(i >= nu_ref[0])
    def _():
        ys_ref[...] = jnp.zeros_like(ys_ref)


def _experts(xs, block_e, n_used, pad_starts, w_slot, w_next, w_gate, b_gate, w_up, b_up, w_down, b_down):
    E, D, F = w_gate.shape
    blk = EXPERT_ROWS * SUBLANES
    n_blocks = xs.shape[0] // blk
    xmap = lambda i, be, nu, *_: (jnp.minimum(i, nu[0] - 1), 0)
    bmap = lambda i, be, *_: (be[i], 0, 0)
    hbm = pl.BlockSpec(memory_space=pl.ANY)
    return pl.pallas_call(
        _expert_kernel,
        grid_spec=pltpu.PrefetchScalarGridSpec(
            num_scalar_prefetch=5, grid=(n_blocks,),
            in_specs=[pl.BlockSpec((blk, LANES), xmap),
                      hbm, pl.BlockSpec((1, 1, F), bmap),
                      hbm, pl.BlockSpec((1, 1, F), bmap),
                      hbm, pl.BlockSpec((1, 1, D), bmap)],
            out_specs=pl.BlockSpec((blk, LANES), lambda i, *_: (i, 0)),
            scratch_shapes=[pltpu.VMEM((2, D, F), F32), pltpu.VMEM((2, D, F), F32), pltpu.VMEM((2, F, D), F32),
                            pltpu.VMEM((D, F), BF16), pltpu.VMEM((D, F), BF16), pltpu.VMEM((F, D), BF16),
                            pltpu.SemaphoreType.DMA((2, 3))]),
        out_shape=jax.ShapeDtypeStruct(xs.shape, F32),
        compiler_params=_cparams("arbitrary"),
        name="expert_ffn",
    )(block_e, n_used, pad_starts, w_slot, w_next, xs, w_gate, b_gate.reshape(E, 1, F), w_up,
      b_up.reshape(E, 1, F), w_down, b_down.reshape(E, 1, D))


COMBINE_TM = 512


def _combine_kernel(dest_ref, x1_ref, gate_ref, fn_ref, ys_ref, o_ref, buf, sem, *, n_tok):
    i = pl.program_id(0)
    tm = x1_ref.shape[0]
    slot = i % 2

    def tile(row):
        return pl.ds(pl.multiple_of(row * SUBLANES, SUBLANES), SUBLANES)

    def gather(step, to_slot):
        def issue(j, _):
            t = step * tm + j
            for k in range(TOP_K):
                pltpu.make_async_copy(ys_ref.at[tile(dest_ref[k * n_tok + t])],
                                      buf.at[to_slot * TOP_K + k, tile(j)], sem.at[to_slot]).start(priority=k % 2)
            return 0
        lax.fori_loop(0, tm, issue, 0, unroll=2)

    @pl.when(i == 0)
    def _():
        gather(0, 0)

    @pl.when(i + 1 < pl.num_programs(0))
    def _():
        gather(i + 1, 1 - slot)

    for k in range(TOP_K):
        pltpu.make_async_copy(ys_ref.at[pl.ds(0, tm * SUBLANES)], buf.at[slot * TOP_K + k], sem.at[slot]).wait()

    gates = gate_ref[...]
    y = x1_ref[...]
    for k in range(TOP_K):
        y = y + _load_token_tiles(buf, tm, lead=slot * TOP_K + k) * gates[:, k:k + 1]
    ms = jnp.mean(y * y, axis=-1, keepdims=True)
    o_ref[...] = y * lax.rsqrt(ms + EPS) * fn_ref[...]


def _combine(ys, dest, x1, gates_t, final_norm):
    N, D = x1.shape
    tm = min(COMBINE_TM, N)
    assert N % tm == 0
    return pl.pallas_call(
        functools.partial(_combine_kernel, n_tok=N),
        grid_spec=pltpu.PrefetchScalarGridSpec(
            num_scalar_prefetch=1, grid=(N // tm,),
            in_specs=[pl.BlockSpec((tm, D), lambda i, d: (i, 0)),
                      pl.BlockSpec((tm, TOP_K), lambda i, d: (i, 0)),
                      pl.BlockSpec((1, D), lambda i, d: (0, 0)),
                      pl.BlockSpec(memory_space=pl.ANY)],
            out_specs=pl.BlockSpec((tm, D), lambda i, d: (i, 0)),
            scratch_shapes=[pltpu.VMEM((2 * TOP_K, tm * SUBLANES, LANES), F32),
                            pltpu.SemaphoreType.DMA((2,))]),
        out_shape=jax.ShapeDtypeStruct((N, D), F32),
        compiler_params=_cparams("arbitrary"),
        name="expert_combine",
    )(dest.reshape(-1), x1, gates_t.T, final_norm.reshape(1, D), ys)


def kernel(x, attn_norm, w_in, q_norm, k_norm, out_norm_a, out_norm_b, w_out, ffn_norm, w_router,
           b_router, w_gate, b_gate, w_up, b_up, w_down, b_down, final_norm):
    B, S, D = x.shape
    x2 = x.reshape(B * S, D)
    qa, ka2, va2, qb, kb, vb = _input_projection(x2, attn_norm[0], w_in[0], q_norm[0], k_norm[0], S)
    oa = _grid_attention(qa, ka2, va2, B, S)
    ob = _dilated_attention(qb, kb, vb, B, S)
    x1, hf, idx_t, gates_t, rank_t, cnt = _outproj_router(
        x2, oa, ob, out_norm_a[0], out_norm_b[0], w_out[0], ffn_norm[0], w_router[0], b_router[0])
    N = B * S
    n_blocks = (N * TOP_K) // EXPERT_ROWS + N_EXPERTS
    counts = cnt[:, 0]
    dest, block_e, n_used, pad_starts, pad_counts, w_slot, w_next = _routing_plan(
        idx_t, rank_t, counts, n_blocks)
    xs = _dispatch(hf, dest, counts, pad_starts, pad_counts, n_used, n_blocks * EXPERT_ROWS)
    ys = _experts(xs, block_e, n_used, pad_starts, w_slot, w_next,
                  w_gate[0], b_gate[0], w_up[0], b_up[0], w_down[0], b_down[0])
    out = _combine(ys, dest, x1, gates_t, final_norm)
    return out.reshape(B, S, D)
```

```python
import functools
import math

import jax
import jax.numpy as jnp
from jax import lax
from jax.experimental import pallas as pl
from jax.experimental.pallas import tpu as pltpu

F32 = jnp.float32
BF16 = jnp.bfloat16

HEAD_DIM = 64
N_HEADS_A = 8
N_KV_HEADS_A = 2
N_HEADS_B = 8
DILATED_BRANCHES = ((128, 1), (512, 4), (2048, 16))
GRID_W = 64
ROPE_THETA = 10000.0
N_EXPERTS = 32
TOP_K = 4
SWIGLU_LIMIT = 7.0
SWIGLU_ALPHA = 1.702
EPS = 1e-6

LANES = 128
QA_W = N_HEADS_A * HEAD_DIM
KA_W = N_KV_HEADS_A * HEAD_DIM
QB_W = N_HEADS_B * HEAD_DIM
VMEM_LIMIT = 56 * 1024 * 1024


def _cparams(*sem):
    return pltpu.CompilerParams(dimension_semantics=sem, vmem_limit_bytes=VMEM_LIMIT)


def _rope_tables(S, tm):
    assert tm % GRID_W == 0 and S % tm == 0
    lane = jnp.arange(LANES)
    i = lane % HEAD_DIM
    half = HEAD_DIM // 2
    t0 = (jnp.arange(S // tm) * tm).astype(F32)[:, None]
    j = jnp.arange(tm).astype(F32)[:, None]
    inv_a = (ROPE_THETA ** (-jnp.arange(0, half, 2, dtype=F32) / half))[i % (half // 2)][None, :]
    is_row = (i < half)[None, :]
    base_a = jnp.where(is_row, jnp.floor(t0 / GRID_W) * inv_a, 0.0)
    jrow = jnp.floor(j / GRID_W)
    offs_a = jnp.where(is_row, jrow, j - jrow * GRID_W) * inv_a
    sgn_a = jnp.where((i // (half // 2)) % 2 == 0, -1.0, 1.0)[None, :].astype(F32)
    inv_b = (ROPE_THETA ** (-jnp.arange(0, HEAD_DIM, 2, dtype=F32) / HEAD_DIM))[i % half][None, :]
    base_b = t0 * inv_b
    offs_b = j * inv_b
    sgn_b = jnp.where(i < half, -1.0, 1.0)[None, :].astype(F32)
    cs = lambda a: jnp.concatenate([jnp.cos(a), jnp.sin(a)], axis=-1)
    return (cs(base_a)[:, None, :], cs(offs_a), sgn_a), (cs(base_b)[:, None, :], cs(offs_b), sgn_b)


def _rope_cos_sin(base_ref, offs_ref, sgn_ref):
    cb, sb = base_ref[0, :, 0:LANES], base_ref[0, :, LANES:2 * LANES]
    co, so = offs_ref[:, 0:LANES], offs_ref[:, LANES:2 * LANES]
    return cb * co - sb * so, (sb * co + cb * so) * sgn_ref[...]


def _rope_tile(x, cos, sin_signed, half):
    lane = lax.broadcasted_iota(jnp.int32, x.shape, 1)
    first = (lane % (2 * half)) < half
    partner = jnp.where(first, pltpu.roll(x, LANES - half, 1), pltpu.roll(x, half, 1))
    return x * cos + partner * sin_signed


def _head_rms(x, seg_mean, gain):
    ms = jnp.dot((x * x).astype(BF16), seg_mean, preferred_element_type=F32)
    return x * lax.rsqrt(ms + EPS) * gain


def _inproj_kernel(x_ref, g_ref, w_ref, qn_ref, kn_ref, seg_ref, base_a_ref, offs_a_ref, sgn_a_ref,
                   base_b_ref, offs_b_ref, sgn_b_ref, qa_ref, ka_ref, va_ref, qb_ref, kb_ref, vb_ref):
    x = x_ref[...]
    ms = jnp.mean(x * x, axis=-1, keepdims=True)
    hn = (x * lax.rsqrt(ms + EPS) * g_ref[...]).astype(BF16)
    proj = jnp.dot(hn, w_ref[...], preferred_element_type=F32)
    seg = seg_ref[...]
    cosa, sina = _rope_cos_sin(base_a_ref, offs_a_ref, sgn_a_ref)
    cosb, sinb = _rope_cos_sin(base_b_ref, offs_b_ref, sgn_b_ref)
    scale = HEAD_DIM ** -0.5
    lane = lax.broadcasted_iota(jnp.int32, (x.shape[0], LANES), 1)
    lo = lane < HEAD_DIM

    off = 0
    for j in range(QA_W // LANES):
        t = proj[:, off + j * LANES: off + (j + 1) * LANES]
        t = _rope_tile(_head_rms(t, seg, qn_ref[...]), cosa, sina, HEAD_DIM // 4)
        qa_ref[:, j * LANES:(j + 1) * LANES] = (t * (scale * LOG2E)).astype(qa_ref.dtype)
    off += QA_W
    k = _rope_tile(_head_rms(proj[:, off:off + LANES], seg, kn_ref[...]), cosa, sina, HEAD_DIM // 4)
    off += KA_W
    v = proj[:, off:off + LANES]
    off += KA_W
    sw = pltpu.roll(k, HEAD_DIM, 1)
    ka_ref[:, 0:LANES] = jnp.where(lo, k, sw).astype(ka_ref.dtype)
    ka_ref[:, LANES:2 * LANES] = jnp.where(lo, sw, k).astype(ka_ref.dtype)
    vt = v.T
    extra = (lax.broadcasted_iota(jnp.int32, (VT_ROWS - HEAD_DIM, v.shape[0]), 0) == 0).astype(F32)
    va_ref[0] = jnp.concatenate([vt[0:HEAD_DIM], extra], axis=0).astype(va_ref.dtype)
    va_ref[1] = jnp.concatenate([vt[HEAD_DIM:2 * HEAD_DIM], extra], axis=0).astype(va_ref.dtype)
    for j in range(QB_W // LANES):
        t = proj[:, off + j * LANES: off + (j + 1) * LANES]
        qb_ref[:, j * LANES:(j + 1) * LANES] = (
            _rope_tile(t, cosb, sinb, HEAD_DIM // 2) * (scale * LOG2E)).astype(qb_ref.dtype)
    off += QB_W
    for j in range(QB_W // LANES):
        t = proj[:, off + j * LANES: off + (j + 1) * LANES]
        kb_ref[:, j * LANES:(j + 1) * LANES] = _rope_tile(t, cosb, sinb, HEAD_DIM // 2).astype(kb_ref.dtype)
    off += QB_W
    vb_ref[...] = proj[:, off:off + QB_W].astype(vb_ref.dtype)


def _input_projection(x2, attn_norm, w_in, q_norm, k_norm, S, tm=512):
    N, D = x2.shape
    tm = min(tm, S)
    assert S % tm == 0 and N % S == 0
    n_s = S // tm
    rope_a, rope_b = _rope_tables(S, tm)
    seg = jnp.kron(jnp.eye(2, dtype=F32), jnp.full((HEAD_DIM, HEAD_DIM), 1.0 / HEAD_DIM, F32)).astype(BF16)
    two = lambda g: jnp.tile(g.reshape(1, HEAD_DIM), (1, 2))
    row = lambda i: (i, 0)
    const = lambda i: (0, 0)
    rope_specs = [pl.BlockSpec((1, 1, 2 * LANES), lambda i: (i % n_s, 0, 0)),
                  pl.BlockSpec((tm, 2 * LANES), const), pl.BlockSpec((1, LANES), const)]
    w = w_in.astype(BF16)
    out_w = (QA_W, 2 * LANES, None, QB_W, QB_W, QB_W)
    out_dt = (BF16, BF16, BF16, F32, F32, F32)
    vt_spec = pl.BlockSpec((N_KV_HEADS_A, VT_ROWS, tm), lambda i: (0, 0, i))
    vt_shape = jax.ShapeDtypeStruct((N_KV_HEADS_A, VT_ROWS, N), BF16)
    return pl.pallas_call(
        _inproj_kernel,
        grid=(N // tm,),
        in_specs=[pl.BlockSpec((tm, D), row), pl.BlockSpec((1, D), const),
                  pl.BlockSpec(w.shape, const), pl.BlockSpec((1, LANES), const),
                  pl.BlockSpec((1, LANES), const), pl.BlockSpec((LANES, LANES), const)]
                 + rope_specs + rope_specs,
        out_specs=[vt_spec if wd is None else pl.BlockSpec((tm, wd), row) for wd in out_w],
        out_shape=[vt_shape if wd is None else jax.ShapeDtypeStruct((N, wd), dt)
                   for wd, dt in zip(out_w, out_dt)],
        compiler_params=_cparams("parallel"),
        name="input_projection",
    )(x2, attn_norm.reshape(1, D), w, two(q_norm), two(k_norm), seg, *rope_a, *rope_b)


VT_ROWS = 80
LOG2E = 1.4426950408889634
ATTN_UNROLL = 32


def _attn_a_kernel(q_ref, k_ref, vt_ref, o_ref, st_scr, pt_scr, *, tk):
    tq = q_ref.shape[0]
    S = k_ref.shape[0]
    sub = lax.broadcasted_iota(jnp.int32, (LANES, tq), 0)
    lo = sub < HEAD_DIM
    q0 = q_ref[:, 0:LANES].astype(F32).T
    q1 = q_ref[:, LANES:2 * LANES].astype(F32).T
    zero = jnp.zeros_like(q0)
    qst = jnp.concatenate([jnp.where(lo, q0, zero), jnp.where(lo, zero, q0),
                           jnp.where(lo, q1, zero), jnp.where(lo, zero, q1)], axis=1).astype(BF16)
    cols = 4 * tq
    n_chunks = S // tk
    assert n_chunks % 2 == 0

    def scores(j, slot):
        start = pl.multiple_of(j * tk, tk)
        st_scr[slot] = jnp.dot(k_ref[pl.ds(start, tk), :], qst, preferred_element_type=F32)

    def softmax_pv(j, slot, m, acc):
        mx = st_scr[slot, 0:8, :]
        for r in range(1, tk // 8):
            mx = jnp.maximum(mx, st_scr[slot, 8 * r:8 * r + 8, :])
        m_new = jnp.maximum(m, jnp.max(mx, axis=0, keepdims=True))
        alpha = jnp.exp2(m - m_new)
        mb = jnp.broadcast_to(m_new, (16, cols))
        for r in range(tk // 16):
            blk = st_scr[slot, 16 * r:16 * r + 16, :]
            pt_scr[slot, 16 * r:16 * r + 16, :] = jnp.exp2((blk - mb).astype(BF16))
        start = pl.multiple_of(j * tk, tk)
        vt = vt_ref[0, :, pl.ds(start, tk)]
        acc = alpha * acc + jnp.dot(vt, pt_scr[slot], preferred_element_type=F32)
        return m_new, acc

    unroll = math.gcd(ATTN_UNROLL, n_chunks)
    assert unroll % 2 == 0

    def body(jj, carry):
        m, acc = carry
        j0 = unroll * jj
        for u in range(unroll):
            scores(jnp.minimum(j0 + u + 1, n_chunks - 1), (u + 1) % 2)
            m, acc = softmax_pv(j0 + u, u % 2, m, acc)
        return m, acc

    scores(0, 0)
    init = (jnp.full((1, cols), -jnp.inf, F32), jnp.zeros((VT_ROWS, cols), F32))
    _, acc = lax.fori_loop(0, n_chunks // unroll, body, init)
    ot = acc[0:HEAD_DIM] / acc[HEAD_DIM:HEAD_DIM + 1]
    for t in range(2):
        pair = jnp.concatenate([ot[:, (2 * t) * tq:(2 * t + 1) * tq],
                                ot[:, (2 * t + 1) * tq:(2 * t + 2) * tq]], axis=0)
        o_ref[:, t * LANES:(t + 1) * LANES] = pair.T.astype(o_ref.dtype)


def _grid_attention(qa, ka2, vat, B, S, tq=256, tk=256):
    N = qa.shape[0]
    tq, tk = min(tq, S), min(tk, S)
    assert S % tq == 0 and S % tk == 0
    nq = S // tq
    return pl.pallas_call(
        functools.partial(_attn_a_kernel, tk=tk),
        grid=(B, N_KV_HEADS_A, nq),
        in_specs=[pl.BlockSpec((tq, 2 * LANES), lambda b, h, i: (b * nq + i, h)),
                  pl.BlockSpec((S, LANES), lambda b, h, i: (b, h)),
                  pl.BlockSpec((1, VT_ROWS, S), lambda b, h, i: (h, 0, b))],
        out_specs=pl.BlockSpec((tq, 2 * LANES), lambda b, h, i: (b * nq + i, h)),
        out_shape=jax.ShapeDtypeStruct((N, QA_W), F32),
        scratch_shapes=[pltpu.VMEM((2, tk, 4 * tq), F32), pltpu.VMEM((2, tk, 4 * tq), BF16)],
        compiler_params=_cparams("parallel", "parallel", "parallel"),
        name="grid_attention",
    )(qa, ka2, vat)


DIL_QB = 128
DIL_R = 64
DIL_KW = DIL_QB + 2 * DIL_R
DIL_SB = 2048
DIL_UNROLL = 16


DIL_WINDOW_OFFS = (-DIL_R, 0, -2 * DIL_R)


def _dilated_kernel(q_ref, k_ref, v_ref, o_ref, acc_scr, m_scr, l_scr, mask_scr, *, seq):
    c = pl.program_id(2)
    lane = lax.broadcasted_iota(jnp.int32, (DIL_QB, LANES), 1)
    lo = lane < HEAD_DIM
    rel = (lax.broadcasted_iota(jnp.int32, (2 * DIL_QB, DIL_KW), 1)
           - lax.broadcasted_iota(jnp.int32, (2 * DIL_QB, DIL_KW), 0) % DIL_QB)
    for n, off in enumerate(DIL_WINDOW_OFFS):
        mask_scr[n] = jnp.where(jnp.abs(rel + off) <= DIL_R, 0.0, -1e30)
    ones = jnp.ones((DIL_KW, LANES), BF16)

    for bi, (window, d) in enumerate(DILATED_BRANCHES[::-1]):
        assert window // (2 * d) == DIL_R
        nb = DIL_SB // (DIL_QB * d)
        n_m = seq // d

        def block(it, _, d=d, nb=nb, n_m=n_m, first=(bi == 0)):
            r = it // nb
            i = it % nb
            row0 = r + d * DIL_QB * i
            m0 = (c * DIL_SB) // d + DIL_QB * i
            ks = jnp.clip(m0 - DIL_R, 0, n_m - DIL_KW)
            q = q_ref[pl.ds(row0, DIL_QB, stride=d), :]
            k = k_ref[pl.ds(r + d * ks, DIL_KW, stride=d), :].astype(BF16)
            v = v_ref[pl.ds(r + d * ks, DIL_KW, stride=d), :].astype(BF16)
            zero = jnp.zeros_like(q)
            qs = jnp.concatenate([jnp.where(lo, q, zero), jnp.where(lo, zero, q)], axis=0).astype(BF16)
            s = lax.dot_general(qs, k, (((1,), (1,)), ((), ())), preferred_element_type=F32)
            off = ks - m0
            s = s + mask_scr[jnp.where(off == DIL_WINDOW_OFFS[0], 0, jnp.where(off == DIL_WINDOW_OFFS[1], 1, 2))]
            mb = jnp.max(s, axis=-1, keepdims=True)
            p = jnp.exp2((s - mb).astype(BF16))
            pv = jnp.dot(p, jnp.concatenate([v, ones], axis=1), preferred_element_type=F32)
            acc_b = jnp.where(lo, pv[0:DIL_QB, 0:LANES], pv[DIL_QB:, 0:LANES])
            m_b = jnp.where(lo, mb[0:DIL_QB], mb[DIL_QB:])
            l_b = jnp.where(lo, pv[0:DIL_QB, LANES:], pv[DIL_QB:, LANES:])
            rows = pl.ds(row0, DIL_QB, stride=d)
            if first:
                acc_scr[rows, :] = acc_b
                m_scr[rows, :] = m_b
                l_scr[rows, :] = l_b
            else:
                m_old = m_scr[rows, :]
                m_new = jnp.maximum(m_old, m_b)
                a_old = jnp.exp2(m_old - m_new)
                a_new = jnp.exp2(m_b - m_new)
                acc_scr[rows, :] = acc_scr[rows, :] * a_old + acc_b * a_new
                l_scr[rows, :] = l_scr[rows, :] * a_old + l_b * a_new
                m_scr[rows, :] = m_new
            return 0

        lax.fori_loop(0, d * nb, block, 0, unroll=DIL_UNROLL)

    o_ref[...] = (acc_scr[...] / l_scr[...]).astype(o_ref.dtype)


def _dilated_attention(qb, kb, vb, B, S):
    N = qb.shape[0]
    assert S % DIL_SB == 0 and S // DILATED_BRANCHES[-1][1] >= DIL_KW
    nsb = S // DIL_SB
    return pl.pallas_call(
        functools.partial(_dilated_kernel, seq=S),
        grid=(B, QB_W // LANES, nsb),
        in_specs=[pl.BlockSpec((DIL_SB, LANES), lambda b, h, i: (b * nsb + i, h)),
                  pl.BlockSpec((S, LANES), lambda b, h, i: (b, h)),
                  pl.BlockSpec((S, LANES), lambda b, h, i: (b, h))],
        out_specs=pl.BlockSpec((DIL_SB, LANES), lambda b, h, i: (b * nsb + i, h)),
        out_shape=jax.ShapeDtypeStruct((N, QB_W), F32),
        scratch_shapes=[pltpu.VMEM((DIL_SB, LANES), F32)] * 3
                       + [pltpu.VMEM((len(DIL_WINDOW_OFFS), 2 * DIL_QB, DIL_KW), F32)],
        compiler_params=_cparams("parallel", "parallel", "parallel"),
        name="dilated_attention",
    )(qb, kb, vb)


SUBLANES = 8


def _store_token_tiles(ref, val):
    rows, d = val.shape
    assert d == SUBLANES * LANES
    for j in range(SUBLANES):
        ref[pl.ds(j, rows, stride=SUBLANES), :] = val[:, j * LANES:(j + 1) * LANES]


def _load_token_tiles(ref, rows, lead=None):
    idx = (lambda j: (pl.ds(j, rows, stride=SUBLANES), slice(None))) if lead is None else (
        lambda j: (lead, pl.ds(j, rows, stride=SUBLANES), slice(None)))
    return jnp.concatenate([ref[idx(j)] for j in range(SUBLANES)], axis=-1)


def _split_bf16(a):
    hi = a.astype(BF16)
    return hi, (a - hi.astype(F32)).astype(BF16)


def _outproj_router_kernel(x_ref, oa_ref, ob_ref, ga_ref, gb_ref, wo_ref, gf_ref, wr_hi_ref, wr_both_ref,
                           br_ref, tri_ref, x1_ref, hf_ref, idx_ref, gate_ref, rank_ref, cnt_ref,
                           carry_scr):
    @pl.when(pl.program_id(0) == 0)
    def _():
        carry_scr[...] = jnp.zeros_like(carry_scr)

    def rms(t, g):
        return t * lax.rsqrt(jnp.mean(t * t, axis=-1, keepdims=True) + EPS) * g

    mix = jnp.concatenate([rms(oa_ref[...], ga_ref[...]), rms(ob_ref[...], gb_ref[...])], axis=-1)
    x1 = x_ref[...] + jnp.dot(mix.astype(BF16), wo_ref[...], preferred_element_type=F32)
    x1_ref[...] = x1
    hf = rms(x1, gf_ref[...])
    _store_token_tiles(hf_ref, hf)

    h_hi, h_lo = _split_bf16(hf)
    nt = (((1,), (1,)), ((), ()))
    both = lax.dot_general(wr_both_ref[...], h_hi, nt, preferred_element_type=F32)
    E = both.shape[0] // 2
    logits = (both[0:E] + both[E:] + lax.dot_general(wr_hi_ref[...], h_lo, nt, preferred_element_type=F32)
              + br_ref[...])
    tm = logits.shape[1]
    eidx = lax.broadcasted_iota(jnp.int32, (E, tm), 0)
    work = logits
    vals, idxs, sel = [], [], jnp.zeros((E, tm), F32)
    for _ in range(TOP_K):
        mx = jnp.max(work, axis=0, keepdims=True)
        first = jnp.min(jnp.where(work == mx, eidx, E), axis=0, keepdims=True)
        hit = eidx == first
        vals.append(mx)
        idxs.append(first)
        sel = jnp.where(hit, 1.0, sel)
        work = jnp.where(hit, -jnp.inf, work)
    ex = [jnp.exp(v - vals[0]) for v in vals]
    den = ex[0] + ex[1] + ex[2] + ex[3]
    gate_ref[...] = jnp.concatenate(ex, axis=0) / den
    idx_ref[...] = jnp.concatenate(idxs, axis=0)

    before = jnp.dot(sel.astype(BF16), tri_ref[...], preferred_element_type=F32) + carry_scr[...]
    ranks = [jnp.sum(jnp.where(eidx == i, before, 0.0), axis=0, keepdims=True) for i in idxs]
    rank_ref[...] = jnp.concatenate(ranks, axis=0).astype(jnp.int32)
    carry_scr[...] = carry_scr[...] + jnp.sum(sel, axis=1, keepdims=True)
    cnt_ref[...] = jnp.broadcast_to(carry_scr[...], cnt_ref.shape).astype(jnp.int32)


def _outproj_router(x2, oa, ob, out_norm_a, out_norm_b, w_out, ffn_norm, w_router, b_router, tm=512):
    N, D = x2.shape
    tm = min(tm, N)
    assert N % tm == 0
    E = w_router.shape[1]
    wr_hi, wr_lo = _split_bf16(w_router.T)
    tri = (jnp.arange(tm)[:, None] < jnp.arange(tm)[None, :]).astype(BF16)
    row = lambda i: (i, 0)
    col = lambda i: (0, i)
    const = lambda i: (0, 0)
    return pl.pallas_call(
        _outproj_router_kernel,
        grid=(N // tm,),
        in_specs=[pl.BlockSpec((tm, D), row), pl.BlockSpec((tm, QA_W), row), pl.BlockSpec((tm, QB_W), row),
                  pl.BlockSpec((1, QA_W), const), pl.BlockSpec((1, QB_W), const),
                  pl.BlockSpec((QA_W + QB_W, D), const), pl.BlockSpec((1, D), const),
                  pl.BlockSpec((E, D), const), pl.BlockSpec((2 * E, D), const), pl.BlockSpec((E, 1), const),
                  pl.BlockSpec((tm, tm), const)],
        out_specs=[pl.BlockSpec((tm, D), row), pl.BlockSpec((tm * SUBLANES, LANES), row),
                   pl.BlockSpec((TOP_K, tm), col), pl.BlockSpec((TOP_K, tm), col),
                   pl.BlockSpec((TOP_K, tm), col), pl.BlockSpec((E, LANES), const)],
        out_shape=[jax.ShapeDtypeStruct((N, D), F32), jax.ShapeDtypeStruct((N * SUBLANES, LANES), F32),
                   jax.ShapeDtypeStruct((TOP_K, N), jnp.int32), jax.ShapeDtypeStruct((TOP_K, N), F32),
                   jax.ShapeDtypeStruct((TOP_K, N), jnp.int32), jax.ShapeDtypeStruct((E, LANES), jnp.int32)],
        scratch_shapes=[pltpu.VMEM((E, 1), F32)],
        compiler_params=_cparams("arbitrary"),
        name="outproj_router",
    )(x2, oa, ob, out_norm_a.reshape(1, -1), out_norm_b.reshape(1, -1), w_out.astype(BF16),
      ffn_norm.reshape(1, D), wr_hi, jnp.concatenate([wr_hi, wr_lo], axis=0), b_router.reshape(E, 1), tri)


EXPERT_ROWS = 256
DISPATCH_TM = 512


def _routing_plan(idx_t, rank_t, counts, n_blocks):
    pad_counts = (counts + EXPERT_ROWS - 1) // EXPERT_ROWS * EXPERT_ROWS
    pad_ends = jnp.cumsum(pad_counts)
    pad_starts = pad_ends - pad_counts
    eids = jnp.arange(N_EXPERTS, dtype=idx_t.dtype)
    dest = rank_t + jnp.sum(jnp.where(idx_t[..., None] == eids, pad_starts, 0), axis=-1)
    blk_start = jnp.arange(n_blocks, dtype=jnp.int32) * EXPERT_ROWS
    block_e = jnp.minimum(jnp.sum((pad_ends[None, :] <= blk_start[:, None]).astype(jnp.int32), axis=1),
                          N_EXPERTS - 1)
    n_used = (pad_ends[-1] // EXPERT_ROWS).astype(jnp.int32).reshape(1)
    nonempty = pad_counts > 0
    w_slot = ((jnp.cumsum(nonempty) - nonempty) % 2).astype(jnp.int32)
    later = (eids[None, :] > eids[:, None]) & nonempty[None, :]
    w_next = jnp.min(jnp.where(later, eids[None, :], N_EXPERTS), axis=1)
    w_next = jnp.where(w_next == N_EXPERTS, -1, w_next).astype(jnp.int32)
    return (dest.astype(jnp.int32), block_e, n_used, pad_starts.astype(jnp.int32), pad_counts.astype(jnp.int32),
            w_slot, w_next)


def _dispatch_kernel(dest_ref, cnt_ref, pstart_ref, pcnt_ref, nu_ref, hf_ref, xs_ref, zero_scr, sem, zsem, *,
                     n_tok):
    i = pl.program_id(0)
    tm = hf_ref.shape[0] // SUBLANES

    def tile(ref, row):
        return ref.at[pl.ds(pl.multiple_of(row * SUBLANES, SUBLANES), SUBLANES)]

    def tok(j, _):
        t = i * tm + j
        for k in range(TOP_K):
            pltpu.make_async_copy(tile(hf_ref, j), tile(xs_ref, dest_ref[k * n_tok + t]), sem).start(
                priority=k % 2)
        return 0
    lax.fori_loop(0, tm, tok, 0, unroll=2)

    @pl.when(i == 0)
    def _():
        zero_scr[...] = jnp.zeros_like(zero_scr)
        blk = EXPERT_ROWS * SUBLANES

        def pad_copy(row):
            return pltpu.make_async_copy(zero_scr.at[pl.ds(0, SUBLANES)], tile(xs_ref, row), zsem)

        def tail_copy(b):
            return pltpu.make_async_copy(zero_scr, xs_ref.at[pl.ds(pl.multiple_of(b * blk, blk), blk)], zsem)

        n_blocks = xs_ref.shape[0] // blk
        lax.fori_loop(nu_ref[0], n_blocks, lambda b, _: (tail_copy(b).start(), 0)[1], 0)
        lax.fori_loop(nu_ref[0], n_blocks, lambda b, _: (tail_copy(b).wait(), 0)[1], 0)

        def expert(e, _):
            base = pstart_ref[e]

            def pad_row(j, _):
                pad_copy(base + j).start()
                return 0
            lax.fori_loop(cnt_ref[e], pcnt_ref[e], pad_row, 0)

            def pad_wait(j, _):
                pad_copy(base + j).wait()
                return 0
            lax.fori_loop(cnt_ref[e], pcnt_ref[e], pad_wait, 0)
            return 0
        lax.fori_loop(0, N_EXPERTS, expert, 0)

    for k in range(TOP_K):
        pltpu.make_async_copy(hf_ref, xs_ref.at[pl.ds(0, tm * SUBLANES)], sem).wait()


def _dispatch(hf, dest, counts, pad_starts, pad_counts, n_used, rows_max):
    N = hf.shape[0] // SUBLANES
    tm = min(DISPATCH_TM, N)
    assert N % tm == 0
    return pl.pallas_call(
        functools.partial(_dispatch_kernel, n_tok=N),
        grid_spec=pltpu.PrefetchScalarGridSpec(
            num_scalar_prefetch=5, grid=(N // tm,),
            in_specs=[pl.BlockSpec((tm * SUBLANES, LANES), lambda i, *_: (i, 0))],
            out_specs=pl.BlockSpec(memory_space=pl.ANY),
            scratch_shapes=[pltpu.VMEM((EXPERT_ROWS * SUBLANES, LANES), F32), pltpu.SemaphoreType.DMA(()),
                            pltpu.SemaphoreType.DMA(())]),
        out_shape=jax.ShapeDtypeStruct((rows_max * SUBLANES, LANES), F32),
        compiler_params=_cparams("arbitrary"),
        name="expert_dispatch",
    )(dest.reshape(-1), counts, pad_starts, pad_counts, n_used, hf)


def _expert_kernel(be_ref, nu_ref, pstart_ref, wslot_ref, wnext_ref, xs_ref, wg_hbm, bg_ref, wu_hbm, bu_ref,
                   wd_hbm, bd_ref, ys_ref, wg_buf, wu_buf, wd_buf, wg_bf, wu_bf, wd_bf, sem):
    i = pl.program_id(0)
    e = be_ref[i]

    def weight_copies(expert, slot):
        return [pltpu.make_async_copy(src.at[expert], dst.at[slot], sem.at[slot, n])
                for n, (src, dst) in enumerate(((wg_hbm, wg_buf), (wu_hbm, wu_buf), (wd_hbm, wd_buf)))]

    @pl.when(i < nu_ref[0])
    def _():
        @pl.when(pstart_ref[e] == i * EXPERT_ROWS)
        def _():
            slot = wslot_ref[e]

            @pl.when(i == 0)
            def _():
                for c in weight_copies(e, slot):
                    c.start()
            for c in weight_copies(e, slot):
                c.wait()

            @pl.when(wnext_ref[e] >= 0)
            def _():
                for c in weight_copies(wnext_ref[e], 1 - slot):
                    c.start(priority=1)
            wg_bf[...] = wg_buf[slot].astype(BF16)
            wu_bf[...] = wu_buf[slot].astype(BF16)
            wd_bf[...] = wd_buf[slot].astype(BF16)

        xb = _load_token_tiles(xs_ref, EXPERT_ROWS).astype(BF16)
        g = jnp.dot(xb, wg_bf[...], preferred_element_type=F32) + bg_ref[0]
        u = jnp.dot(xb, wu_bf[...], preferred_element_type=F32) + bu_ref[0]
        g = jnp.minimum(g, SWIGLU_LIMIT)
        u = jnp.clip(u, -SWIGLU_LIMIT, SWIGLU_LIMIT)
        act = (u + 1.0) * (g * jax.nn.sigmoid(SWIGLU_ALPHA * g))
        _store_token_tiles(ys_ref, jnp.dot(act.astype(BF16), wd_bf[...], preferred_element_type=F32)
                           + bd_ref[0])

    @pl.when(i >= nu_ref[0])
    def _():
        ys_ref[...] = jnp.zeros_like(ys_ref)


def _experts(xs, block_e, n_used, pad_starts, w_slot, w_next, w_gate, b_gate, w_up, b_up, w_down, b_down):
    E, D, F = w_gate.shape
    blk = EXPERT_ROWS * SUBLANES
    n_blocks = xs.shape[0] // blk
    xmap = lambda i, be, nu, *_: (jnp.minimum(i, nu[0] - 1), 0)
    bmap = lambda i, be, *_: (be[i], 0, 0)
    hbm = pl.BlockSpec(memory_space=pl.ANY)
    return pl.pallas_call(
        _expert_kernel,
        grid_spec=pltpu.PrefetchScalarGridSpec(
            num_scalar_prefetch=5, grid=(n_blocks,),
            in_specs=[pl.BlockSpec((blk, LANES), xmap),
                      hbm, pl.BlockSpec((1, 1, F), bmap),
                      hbm, pl.BlockSpec((1, 1, F), bmap),
                      hbm, pl.BlockSpec((1, 1, D), bmap)],
            out_specs=pl.BlockSpec((blk, LANES), lambda i, *_: (i, 0)),
            scratch_shapes=[pltpu.VMEM((2, D, F), F32), pltpu.VMEM((2, D, F), F32), pltpu.VMEM((2, F, D), F32),
                            pltpu.VMEM((D, F), BF16), pltpu.VMEM((D, F), BF16), pltpu.VMEM((F, D), BF16),
                            pltpu.SemaphoreType.DMA((2, 3))]),
        out_shape=jax.ShapeDtypeStruct(xs.shape, F32),
        compiler_params=_cparams("arbitrary"),
        name="expert_ffn",
    )(block_e, n_used, pad_starts, w_slot, w_next, xs, w_gate, b_gate.reshape(E, 1, F), w_up,
      b_up.reshape(E, 1, F), w_down, b_down.reshape(E, 1, D))


COMBINE_TM = 512


def _combine_kernel(dest_ref, x1_ref, gate_ref, fn_ref, ys_ref, o_ref, buf, sem, *, n_tok):
    i = pl.program_id(0)
    tm = x1_ref.shape[0]
    slot = i % 2

    def tile(row):
        return pl.ds(pl.multiple_of(row * SUBLANES, SUBLANES), SUBLANES)

    def gather(step, to_slot):
        def issue(j, _):
            t = step * tm + j
            for k in range(TOP_K):
                pltpu.make_async_copy(ys_ref.at[tile(dest_ref[k * n_tok + t])],
                                      buf.at[to_slot * TOP_K + k, tile(j)], sem.at[to_slot]).start(priority=1)
            return 0
        lax.fori_loop(0, tm, issue, 0, unroll=2)

    @pl.when(i == 0)
    def _():
        gather(0, 0)

    @pl.when(i + 1 < pl.num_programs(0))
    def _():
        gather(i + 1, 1 - slot)

    for k in range(TOP_K):
        pltpu.make_async_copy(ys_ref.at[pl.ds(0, tm * SUBLANES)], buf.at[slot * TOP_K + k], sem.at[slot]).wait()

    gates = gate_ref[...]
    y = x1_ref[...]
    for k in range(TOP_K):
        y = y + _load_token_tiles(buf, tm, lead=slot * TOP_K + k) * gates[:, k:k + 1]
    ms = jnp.mean(y * y, axis=-1, keepdims=True)
    o_ref[...] = y * lax.rsqrt(ms + EPS) * fn_ref[...]


def _combine(ys, dest, x1, gates_t, final_norm):
    N, D = x1.shape
    tm = min(COMBINE_TM, N)
    assert N % tm == 0
    return pl.pallas_call(
        functools.partial(_combine_kernel, n_tok=N),
        grid_spec=pltpu.PrefetchScalarGridSpec(
            num_scalar_prefetch=1, grid=(N // tm,),
            in_specs=[pl.BlockSpec((tm, D), lambda i, d: (i, 0)),
                      pl.BlockSpec((tm, TOP_K), lambda i, d: (i, 0)),
                      pl.BlockSpec((1, D), lambda i, d: (0, 0)),
                      pl.BlockSpec(memory_space=pl.ANY)],
            out_specs=pl.BlockSpec((tm, D), lambda i, d: (i, 0)),
            scratch_shapes=[pltpu.VMEM((2 * TOP_K, tm * SUBLANES, LANES), F32),
                            pltpu.SemaphoreType.DMA((2,))]),
        out_shape=jax.ShapeDtypeStruct((N, D), F32),
        compiler_params=_cparams("arbitrary"),
        name="expert_combine",
    )(dest.reshape(-1), x1, gates_t.T, final_norm.reshape(1, D), ys)


def kernel(x, attn_norm, w_in, q_norm, k_norm, out_norm_a, out_norm_b, w_out, ffn_norm, w_router,
           b_router, w_gate, b_gate, w_up, b_up, w_down, b_down, final_norm):
    B, S, D = x.shape
    x2 = x.reshape(B * S, D)
    qa, ka2, va2, qb, kb, vb = _input_projection(x2, attn_norm[0], w_in[0], q_norm[0], k_norm[0], S)
    oa = _grid_attention(qa, ka2, va2, B, S)
    ob = _dilated_attention(qb, kb, vb, B, S)
    x1, hf, idx_t, gates_t, rank_t, cnt = _outproj_router(
        x2, oa, ob, out_norm_a[0], out_norm_b[0], w_out[0], ffn_norm[0], w_router[0], b_router[0])
    N = B * S
    n_blocks = (N * TOP_K) // EXPERT_ROWS + N_EXPERTS
    counts = cnt[:, 0]
    dest, block_e, n_used, pad_starts, pad_counts, w_slot, w_next = _routing_plan(
        idx_t, rank_t, counts, n_blocks)
    xs = _dispatch(hf, dest, counts, pad_starts, pad_counts, n_used, n_blocks * EXPERT_ROWS)
    ys = _experts(xs, block_e, n_used, pad_starts, w_slot, w_next,
                  w_gate[0], b_gate[0], w_up[0], b_up[0], w_down[0], b_down[0])
    out = _combine(ys, dest, x1, gates_t, final_norm)
    return out.reshape(B, S, D)
```
